```python
import jax
import jax.numpy as jnp
from jax import lax
import numpy as np

D_MODEL = 1024
BATCH = 16
SEQ = 4096
DEPTH = 2

F32 = jnp.float32
GRID_W = 64
CTX_LEN = 256
EPS = 1e-6

LRU_WIDTH = D_MODEL // 2
LRU_BLOCKS = 8
LRU_BLOCK = LRU_WIDTH // LRU_BLOCKS
LRU_C = 8.0
CONV_W = 4
CONV_PAD = (2, 1)
GLA_HEADS = 4
GLA_DK = 64
GLA_DV = 128
GLA_KEY = GLA_HEADS * GLA_DK
GLA_VAL = GLA_HEADS * GLA_DV
GLA_RANK = 16
GLA_TAU = 16.0
GLA_CHUNK = 64
MIX_WIDTH = LRU_WIDTH + GLA_VAL
EVEN_SPLITS = (LRU_WIDTH, LRU_WIDTH, GLA_KEY, GLA_KEY, GLA_VAL, GLA_VAL, GLA_RANK, GLA_RANK)
EVEN_IN = sum(EVEN_SPLITS)
ATT_HEADS = 8
KV_HEADS = 2
Q_PER_KV = ATT_HEADS // KV_HEADS
HEAD_DIM = 128
ROPE_THETA = 10000.0
Q_BLOCK = 128
QKV_SPLITS = (ATT_HEADS * HEAD_DIM, KV_HEADS * HEAD_DIM, KV_HEADS * HEAD_DIM)
QKV_WIDTH = sum(QKV_SPLITS)
N_GROUPS = 4
EXPERTS_PER_GROUP = 4
N_EXPERTS = N_GROUPS * EXPERTS_PER_GROUP
EXPERT_HIDDEN = 512
TOP_K = 2

kernel_name = "hybrid_rglru_gla_gqa_hmoe_dit"


def _split(z, sizes):
    cuts = np.cumsum(sizes)[:-1].tolist()
    return jnp.split(z, cuts, axis=-1)


def rmsnorm(x, g):
    xf = x.astype(F32)
    y = xf * lax.rsqrt(jnp.mean(xf * xf, axis=-1, keepdims=True) + EPS)
    return (y * g.astype(F32)).astype(x.dtype)


def modulate(h, shift, scale):
    return h * (1 + scale) + shift


def adaln(cvec, w, b):
    m = jax.nn.silu(cvec) @ w + b
    m = m.reshape(m.shape[0], 6, 1, D_MODEL)
    return tuple(m[:, j] for j in range(6))


def dwconv(u, w, b):
    y = lax.conv_general_dilated(u, w[:, None, :], window_strides=(1,), padding=[CONV_PAD],
                                 dimension_numbers=("NWC", "WIO", "NWC"),
                                 feature_group_count=u.shape[-1])
    return y + b


def linear_scan(a, b, h0):
    b = b.at[:, 0].add(a[:, 0] * h0)

    def combine(left, right):
        al, bl = left
        ar, br = right
        return al * ar, ar * bl + br

    _, h = lax.associative_scan(combine, (a, b), axis=1)
    return h


def rglru_coeffs(u, wa, ba, wi, bi, lam):
    ub = u.reshape(*u.shape[:-1], LRU_BLOCKS, LRU_BLOCK)
    r = jax.nn.sigmoid(jnp.einsum("blnc,ncd->blnd", ub, wa.astype(F32)).reshape(u.shape) + ba.astype(F32))
    ig = jax.nn.sigmoid(jnp.einsum("blnc,ncd->blnd", ub, wi.astype(F32)).reshape(u.shape) + bi.astype(F32))
    log_a = -LRU_C * r * jax.nn.softplus(-lam.astype(F32))
    a = jnp.exp(log_a)
    b = jnp.sqrt(-jnp.expm1(2.0 * log_a)) * (ig * u)
    return a, b


def gla_chunked(q, k, v, lg, s0):
    b_, n_tok = q.shape[:2]
    n_chunks = n_tok // GLA_CHUNK

    def chunk(t):
        return t.reshape(b_, n_chunks, GLA_CHUNK, *t.shape[2:])

    q, k, v, lg = chunk(q), chunk(k), chunk(v), chunk(lg)
    g = jnp.cumsum(lg, axis=2)
    g_last = g[:, :, -1:]
    qg = q * jnp.exp(g)
    kg = k * jnp.exp(-g)
    kd = k * jnp.exp(g_last - g)
    mask = jnp.tril(jnp.ones((GLA_CHUNK, GLA_CHUNK), bool))
    att = jnp.where(mask, jnp.einsum("bnihd,bnjhd->bnhij", qg, kg), 0.0)
    o_intra = jnp.einsum("bnhij,bnjhe->bnihe", att, v)
    ds = jnp.einsum("bnjhd,bnjhe->bnhde", kd, v)
    decay = jnp.exp(g_last[:, :, 0])

    def step(s, inp):
        dec, d_s = inp
        return dec[..., None] * s + d_s, s

    s_fin, s_prev = lax.scan(step, s0, (jnp.moveaxis(decay, 1, 0), jnp.moveaxis(ds, 1, 0)))
    s_prev = jnp.moveaxis(s_prev, 0, 1)
    o_inter = jnp.einsum("bnihd,bnhde->bnihe", qg, s_prev)
    return (o_intra + o_inter).reshape(b_, n_tok, GLA_HEADS, GLA_DV), s_fin


def rglru_gla_mixer(hc, hl, w_in, conv_w, conv_b, lru_wa, lru_ba, lru_wi, lru_bi, lru_lam,
                    gla_wg, gla_bg, gla_norm, w_out, ctx_out):
    dt = hl.dtype
    b_ = hl.shape[0]

    def project(h):
        n = h.shape[1]
        xa, ga, q, k, v, r, lr_f, lr_b = _split(h @ w_in, EVEN_SPLITS)
        u = dwconv(xa, conv_w, conv_b).astype(F32)
        q = q.astype(F32).reshape(b_, n, GLA_HEADS, GLA_DK) * GLA_DK ** -0.5
        k = k.astype(F32).reshape(b_, n, GLA_HEADS, GLA_DK)
        v = v.astype(F32).reshape(b_, n, GLA_HEADS, GLA_DV)
        return u, ga, q, k, v, r, (lr_f, lr_b)

    def direction(p, d):
        u, _, q, k, v, _, lrs = p
        a, b = rglru_coeffs(u, lru_wa[d], lru_ba[d], lru_wi[d], lru_bi[d], lru_lam[d])
        lg = jax.nn.log_sigmoid((lrs[d] @ gla_wg[d] + gla_bg[d]).astype(F32)) / GLA_TAU
        seqs = (a, b, q, k, v, lg.reshape(k.shape))
        if d == 1:
            seqs = tuple(jnp.flip(t, axis=1) for t in seqs)
        return seqs

    pc, pl = project(hc), project(hl)
    lru_c, lru_l, gla_c, gla_l = [], [], [], []
    for d in range(2):
        ac, bc, qc, kc, vc, gc = direction(pc, d)
        al, bl, ql, kl, vl, gl = direction(pl, d)
        h_c = linear_scan(ac, bc, jnp.zeros((b_, LRU_WIDTH), F32))
        h_l = linear_scan(al, bl, h_c[:, -1])
        o_c, s_c = gla_chunked(qc, kc, vc, gc, jnp.zeros((b_, GLA_HEADS, GLA_DK, GLA_DV), F32))
        o_l, _ = gla_chunked(ql, kl, vl, gl, s_c)
        if d == 1:
            h_c, h_l, o_c, o_l = (jnp.flip(t, axis=1) for t in (h_c, h_l, o_c, o_l))
        lru_c.append(h_c)
        lru_l.append(h_l)
        gla_c.append(o_c)
        gla_l.append(o_l)

    def finish(p, lru, gla):
        _, ga, _, _, _, r, _ = p
        n = lru.shape[1]
        ya = lru * jax.nn.gelu(ga.astype(F32))
        yb = rmsnorm(gla, gla_norm).reshape(b_, n, GLA_VAL) * jax.nn.silu(r.astype(F32))
        return jnp.concatenate([ya, yb], axis=-1).astype(dt) @ w_out

    yl = finish(pl, lru_l[0] + lru_l[1], gla_l[0] + gla_l[1])
    yc = finish(pc, lru_c[0] + lru_c[1], gla_c[0] + gla_c[1]) if ctx_out else None
    return yc, yl


def rope_tables(rows):
    row = jnp.repeat(jnp.arange(rows, dtype=F32), GRID_W)
    col = jnp.tile(jnp.arange(GRID_W, dtype=F32), rows)
    pairs_per_axis = HEAD_DIM // 4
    freqs = ROPE_THETA ** (-jnp.arange(pairs_per_axis, dtype=F32) / pairs_per_axis)
    ang = jnp.concatenate([row[:, None] * freqs, col[:, None] * freqs], axis=-1)
    return jnp.cos(ang), jnp.sin(ang)


def apply_rope(x, cos, sin):
    xf = x.astype(F32).reshape(*x.shape[:-1], HEAD_DIM // 2, 2)
    x1, x2 = xf[..., 0], xf[..., 1]
    out = jnp.stack([x1 * cos - x2 * sin, x1 * sin + x2 * cos], axis=-1)
    return out.reshape(x.shape).astype(x.dtype)


def attend(q, k, v):
    s = jnp.einsum("bqhgd,bkhd->bhgqk", q, k, preferred_element_type=F32) * HEAD_DIM ** -0.5
    p = jax.nn.softmax(s, axis=-1)
    return jnp.einsum("bhgqk,bkhd->bqhgd", p, v.astype(F32)).astype(v.dtype)


def gqa_mixer(hc, hl, w_qkv, q_norm, k_norm, w_o, cos, sin, ctx_out):
    b_ = hl.shape[0]

    def project(h):
        n = h.shape[1]
        q, k, v = _split(h @ w_qkv, QKV_SPLITS)
        q = rmsnorm(q.reshape(b_, n, KV_HEADS, Q_PER_KV, HEAD_DIM), q_norm)
        k = rmsnorm(k.reshape(b_, n, KV_HEADS, HEAD_DIM), k_norm)
        return q, k, v.reshape(b_, n, KV_HEADS, HEAD_DIM)

    qc, kc, vc = project(hc)
    ql, kl, vl = project(hl)
    ql = apply_rope(ql, cos[:, None, None], sin[:, None, None])
    kl = apply_rope(kl, cos[:, None], sin[:, None])
    k_all = jnp.concatenate([kc, kl], axis=1)
    v_all = jnp.concatenate([vc, vl], axis=1)
    n = hl.shape[1]
    qb = jnp.moveaxis(ql.reshape(b_, n // Q_BLOCK, Q_BLOCK, KV_HEADS, Q_PER_KV, HEAD_DIM), 1, 0)
    ob = lax.map(lambda qi: attend(qi, k_all, v_all), qb)
    yl = jnp.moveaxis(ob, 0, 1).reshape(b_, n, ATT_HEADS * HEAD_DIM) @ w_o
    yc = attend(qc, kc, vc).reshape(b_, hc.shape[1], ATT_HEADS * HEAD_DIM) @ w_o if ctx_out else None
    return yc, yl


def hier_moe(h, wg, bg, we, be, w1, w3, w2):
    b_, n, d = h.shape
    t = h.reshape(-1, d)
    n_tok = t.shape[0]
    p_group = jax.nn.softmax((t @ wg + bg).astype(F32), axis=-1)
    w_group, g_idx = lax.top_k(p_group, 1)
    e_logits = (t @ we + be).astype(F32).reshape(n_tok, N_GROUPS, EXPERTS_PER_GROUP)
    e_sel = e_logits[jnp.arange(n_tok), g_idx[:, 0]]
    e_val, e_idx = lax.top_k(e_sel, TOP_K)
    w_exp = jax.nn.softmax(e_val, axis=-1) * w_group
    global_idx = g_idx * EXPERTS_PER_GROUP + e_idx
    comb = jnp.sum(jax.nn.one_hot(global_idx, N_EXPERTS, dtype=F32) * w_exp[..., None], axis=1)
    out = jnp.zeros((n_tok, d), F32)
    for e in range(N_EXPERTS):
        he = jax.nn.silu(t @ w1[e]) * (t @ w3[e])
        out = out + comb[:, e:e + 1] * (he @ w2[e]).astype(F32)
    return out.astype(h.dtype).reshape(b_, n, d)


def setup_inputs(seed: int = 0) -> dict:
    key = jax.random.key(seed)
    ks = iter(jax.random.split(key, 48))
    d = D_MODEL
    n_even = (DEPTH + 1) // 2
    n_odd = DEPTH // 2

    def nrm(shape, fan_in, scale=1.0):
        return jax.random.normal(next(ks), shape, F32) * (scale * fan_in ** -0.5)

    def gain(shape):
        return 1.0 + 0.05 * jax.random.normal(next(ks), shape, F32)

    def bias(shape, s=0.02):
        return s * jax.random.normal(next(ks), shape, F32)

    x = jax.random.normal(next(ks), (BATCH, SEQ, d), F32)
    c = jax.random.normal(next(ks), (BATCH, d), F32)
    ctx = jax.random.normal(next(ks), (BATCH, CTX_LEN, d), F32)
    c_ctx = jax.random.normal(next(ks), (d,), F32)
    norm1 = gain((DEPTH, d))
    norm2 = gain((DEPTH, d))
    w_ada = nrm((DEPTH, d, 6 * d), d, 0.5)
    b_ada = bias((DEPTH, 6 * d))
    ev_w_in = nrm((n_even, d, EVEN_IN), d)
    ev_conv_w = nrm((n_even, CONV_W, LRU_WIDTH), CONV_W)
    ev_conv_b = bias((n_even, LRU_WIDTH))
    ev_lru_wa = nrm((n_even, 2, LRU_BLOCKS, LRU_BLOCK, LRU_BLOCK), LRU_BLOCK)
    ev_lru_ba = bias((n_even, 2, LRU_WIDTH))
    ev_lru_wi = nrm((n_even, 2, LRU_BLOCKS, LRU_BLOCK, LRU_BLOCK), LRU_BLOCK)
    ev_lru_bi = bias((n_even, 2, LRU_WIDTH))
    a_init = jax.random.uniform(next(ks), (n_even, 2, LRU_WIDTH), F32, 0.9, 0.999)
    p_init = a_init ** (1.0 / LRU_C)
    ev_lru_lam = jnp.log(p_init) - jnp.log1p(-p_init)
    ev_gla_wg = nrm((n_even, 2, GLA_RANK, GLA_KEY), GLA_RANK)
    ev_gla_bg = bias((n_even, 2, GLA_KEY), 0.5)
    ev_gla_norm = gain((n_even, GLA_DV))
    ev_w_out = nrm((n_even, MIX_WIDTH, d), MIX_WIDTH)
    od_w_qkv = nrm((n_odd, d, QKV_WIDTH), d)
    od_q_norm = gain((n_odd, HEAD_DIM))
    od_k_norm = gain((n_odd, HEAD_DIM))
    od_w_o = nrm((n_odd, ATT_HEADS * HEAD_DIM, d), ATT_HEADS * HEAD_DIM)
    moe_wg = nrm((DEPTH, d, N_GROUPS), d)
    moe_bg = bias((DEPTH, N_GROUPS), 0.01)
    moe_we = nrm((DEPTH, d, N_EXPERTS), d)
    moe_be = bias((DEPTH, N_EXPERTS), 0.01)
    moe_w1 = nrm((DEPTH, N_EXPERTS, d, EXPERT_HIDDEN), d)
    moe_w3 = nrm((DEPTH, N_EXPERTS, d, EXPERT_HIDDEN), d)
    moe_w2 = nrm((DEPTH, N_EXPERTS, EXPERT_HIDDEN, d), EXPERT_HIDDEN)
    return {"x": x, "c": c, "ctx": ctx, "c_ctx": c_ctx, "norm1": norm1, "norm2": norm2,
            "w_ada": w_ada, "b_ada": b_ada, "ev_w_in": ev_w_in, "ev_conv_w": ev_conv_w,
            "ev_conv_b": ev_conv_b, "ev_lru_wa": ev_lru_wa, "ev_lru_ba": ev_lru_ba,
            "ev_lru_wi": ev_lru_wi, "ev_lru_bi": ev_lru_bi, "ev_lru_lam": ev_lru_lam,
            "ev_gla_wg": ev_gla_wg, "ev_gla_bg": ev_gla_bg, "ev_gla_norm": ev_gla_norm,
            "ev_w_out": ev_w_out, "od_w_qkv": od_w_qkv, "od_q_norm": od_q_norm,
            "od_k_norm": od_k_norm, "od_w_o": od_w_o, "moe_wg": moe_wg, "moe_bg": moe_bg,
            "moe_we": moe_we, "moe_be": moe_be, "moe_w1": moe_w1, "moe_w3": moe_w3,
            "moe_w2": moe_w2}


def reference(x, c, ctx, c_ctx, norm1, norm2, w_ada, b_ada, ev_w_in, ev_conv_w, ev_conv_b,
              ev_lru_wa, ev_lru_ba, ev_lru_wi, ev_lru_bi, ev_lru_lam, ev_gla_wg, ev_gla_bg,
              ev_gla_norm, ev_w_out, od_w_qkv, od_q_norm, od_k_norm, od_w_o, moe_wg, moe_bg,
              moe_we, moe_be, moe_w1, moe_w3, moe_w2):
    ROWS = x.shape[1] // GRID_W
    cos, sin = rope_tables(ROWS)
    xl, xc = x, ctx
    for i in range(DEPTH):
        last = i == DEPTH - 1
        j = i // 2
        sh1l, sc1l, g1l, sh2l, sc2l, g2l = adaln(c, w_ada[i], b_ada[i])
        sh1c, sc1c, g1c, sh2c, sc2c, g2c = adaln(c_ctx[None], w_ada[i], b_ada[i])
        hl = modulate(rmsnorm(xl, norm1[i]), sh1l, sc1l)
        hc = modulate(rmsnorm(xc, norm1[i]), sh1c, sc1c)
        if i % 2 == 0:
            yc, yl = rglru_gla_mixer(hc, hl, ev_w_in[j], ev_conv_w[j], ev_conv_b[j], ev_lru_wa[j],
                                     ev_lru_ba[j], ev_lru_wi[j], ev_lru_bi[j], ev_lru_lam[j],
                                     ev_gla_wg[j], ev_gla_bg[j], ev_gla_norm[j], ev_w_out[j],
                                     not last)
        else:
            yc, yl = gqa_mixer(hc, hl, od_w_qkv[j], od_q_norm[j], od_k_norm[j], od_w_o[j],
                               cos, sin, not last)
        xl = xl + g1l * yl
        hl2 = modulate(rmsnorm(xl, norm2[i]), sh2l, sc2l)
        xl = xl + g2l * hier_moe(hl2, moe_wg[i], moe_bg[i], moe_we[i], moe_be[i],
                                 moe_w1[i], moe_w3[i], moe_w2[i])
        if not last:
            xc = xc + g1c * yc
            hc2 = modulate(rmsnorm(xc, norm2[i]), sh2c, sc2c)
            xc = xc + g2c * hier_moe(hc2, moe_wg[i], moe_bg[i], moe_we[i], moe_be[i],
                                     moe_w1[i], moe_w3[i], moe_w2[i])
    return xl
```

```python
import functools

import numpy as np
import jax
import jax.numpy as jnp
from jax import lax
from jax.experimental import pallas as pl
from jax.experimental.pallas import tpu as pltpu

F32 = jnp.float32
BF16 = jnp.bfloat16
I32 = jnp.int32
HIGHEST = lax.Precision.HIGHEST

EPS = 1e-6
GRID_W = 64
LRU_BLOCKS = 8
LRU_C = 8.0
GLA_HEADS = 4
GLA_DK = 64
GLA_DV = 128
GLA_KEY = GLA_HEADS * GLA_DK
GLA_VAL = GLA_HEADS * GLA_DV
GLA_RANK = 16
GLA_TAU = 16.0
GLA_CHUNK = 64
_CHUNK_SHIFT = GLA_CHUNK.bit_length() - 1
_DK_SHIFT = GLA_DK.bit_length() - 1
ATT_HEADS = 8
KV_HEADS = 2
Q_PER_KV = ATT_HEADS // KV_HEADS
HEAD_DIM = 128
ROPE_THETA = 10000.0
N_GROUPS = 4
EXPERTS_PER_GROUP = 4
N_EXPERTS = N_GROUPS * EXPERTS_PER_GROUP
PAIRS_PER_GROUP = 6
N_CLASSES = N_GROUPS * PAIRS_PER_GROUP

LANES = 128
SUBLANES = 8
TOK = 256
MOE_TILE = 256
ATT_Q = 128
PERM_ROWS = 2048
WEXT = LANES
VMEM_LIMIT = 56 * 1024 * 1024


def _cparams(sem):
    return pltpu.CompilerParams(dimension_semantics=sem, vmem_limit_bytes=VMEM_LIMIT)


def _rms(x, g):
    return x * lax.rsqrt(jnp.mean(x * x, axis=-1, keepdims=True) + EPS) * g


def _sigmoid(x):
    return 1.0 / (1.0 + jnp.exp(-x))


def _silu(x):
    return x * _sigmoid(x)


def _gelu_tanh(x):
    return 0.5 * x * (1.0 + jnp.tanh(np.sqrt(2.0 / np.pi).astype(np.float32) * (x + 0.044715 * (x * x * x))))


def _softplus(x):
    return jnp.maximum(x, 0.0) + jnp.log(1.0 + jnp.exp(-jnp.abs(x)))


def _log_sigmoid(x):
    return -_softplus(-x)


def _adaln_kernel(cv_ref, w_ref, b_ref, o_ref):
    s = _silu(cv_ref[...])
    o_ref[0] = jnp.dot(s, w_ref[0], precision=HIGHEST, preferred_element_type=F32) + b_ref[0]


def _adaln(cv, w_ada, b_ada):
    depth, d, n6 = w_ada.shape
    rows = cv.shape[0]
    tn = 6 * d // 4
    return pl.pallas_call(
        _adaln_kernel,
        grid=(depth, n6 // tn),
        in_specs=[pl.BlockSpec((rows, d), lambda l, j: (0, 0)),
                  pl.BlockSpec((1, d, tn), lambda l, j: (l, 0, j)),
                  pl.BlockSpec((1, 1, tn), lambda l, j: (l, 0, j))],
        out_specs=pl.BlockSpec((1, rows, tn), lambda l, j: (l, 0, j)),
        out_shape=jax.ShapeDtypeStruct((depth, rows, n6), F32),
        compiler_params=_cparams(("arbitrary", "arbitrary")),
        name="adaln",
    )(cv, w_ada, b_ada.reshape(depth, 1, n6))


class _Mods:
    def __init__(self, table, rows, batch, ncc):
        self.table, self.rows, self.batch, self.ncc = table, rows, batch, ncc
        self.d = table.shape[-1]

    def spec(self, layer, j, latent_only=False):
        rows, batch, ncc = self.rows, self.batch, self.ncc

        def imap(b, c):
            r = b if latent_only else jnp.where(c < ncc, batch, b)
            return ((layer * rows + r) * 6 + j, 0, 0)

        return pl.BlockSpec((1, 1, self.d), imap)


def _inproj_kernel(ncc, x_ref, c_ref, g_ref, sh_ref, sc_ref, w_ref, wlr_ref, pm_ref, lr_ref):
    c = pl.program_id(1)
    xf = jnp.where(c < ncc, c_ref[0], x_ref[0])
    h = _rms(xf, g_ref[...]) * (1.0 + sc_ref[0]) + sh_ref[0]
    hb = h.astype(BF16)
    pm_ref[0] = jnp.dot(hb, w_ref[...], preferred_element_type=F32).astype(BF16)
    lr_ref[0] = jnp.dot(hb, wlr_ref[...], preferred_element_type=F32).astype(BF16)


def _inproj(x, ctx, gain, mods, w_main, w_lr):
    batch, seq, d = x.shape
    ncc = ctx.shape[1] // TOK
    nch = ncc + seq // TOK
    nm = w_main.shape[1]
    return pl.pallas_call(
        functools.partial(_inproj_kernel, ncc),
        grid=(batch, nch),
        in_specs=[pl.BlockSpec((1, TOK, d), lambda b, c: (b, jnp.maximum(c - ncc, 0), 0)),
                  pl.BlockSpec((1, TOK, d), lambda b, c: (b, jnp.minimum(c, ncc - 1), 0)),
                  pl.BlockSpec((1, d), lambda b, c: (0, 0)),
                  mods.spec(0, 0), mods.spec(0, 1),
                  pl.BlockSpec((d, nm), lambda b, c: (0, 0)),
                  pl.BlockSpec((d, LANES), lambda b, c: (0, 0))],
        out_specs=[pl.BlockSpec((1, TOK, nm), lambda b, c: (b, c, 0)),
                   pl.BlockSpec((1, TOK, LANES), lambda b, c: (b, c, 0))],
        out_shape=[jax.ShapeDtypeStruct((batch, nch * TOK, nm), BF16),
                   jax.ShapeDtypeStruct((batch, nch * TOK, LANES), BF16)],
        compiler_params=_cparams(("arbitrary", "arbitrary")),
        name="inproj",
    )(x, ctx, gain, mods.table, mods.table, w_main, w_lr)


def _lru_scan(a, b, h0, rev):
    sub = lax.broadcasted_iota(I32, a.shape, 0) & (SUBLANES - 1)
    for dist in (1, 2, 4):
        shift = TOK - dist if rev else dist
        a_s = pltpu.roll(a, shift, 0)
        b_s = pltpu.roll(b, shift, 0)
        m = (sub < SUBLANES - dist) if rev else (sub >= dist)
        b = jnp.where(m, a * b_s + b, b)
        a = jnp.where(m, a * a_s, a)
    n_groups = TOK // SUBLANES
    order = range(n_groups - 1, -1, -1) if rev else range(n_groups)
    outs = [None] * n_groups
    h = h0
    for r in order:
        hr = a[r * SUBLANES:(r + 1) * SUBLANES] * h + b[r * SUBLANES:(r + 1) * SUBLANES]
        outs[r] = hr
        h = hr[0:1] if rev else hr[SUBLANES - 1:SUBLANES]
    return jnp.concatenate(outs, axis=0), h


def _seqmix_kernel(rev, ncc, nch, xa_ref, xp_ref, xn_ref, q_ref, k_ref, v_ref, lr_ref,
                   cw_ref, cb_ref, wgt_ref, bgt_ref, lam_ref, wg_ref, bg_ref,
                   lru_ref, gla_ref, h_scr, s_scr):
    s = pl.program_id(1)
    nl = nch - ncc
    if rev:
        ch = jnp.where(s < ncc, ncc - 1 - s, ncc + (nl - 1) - (s - ncc))
    else:
        ch = s

    @pl.when(s == 0)
    def _():
        h_scr[...] = jnp.zeros_like(h_scr)
        s_scr[...] = jnp.zeros_like(s_scr)

    lw = xa_ref.shape[-1]
    xa = xa_ref[0].astype(F32)
    has_prev = jnp.logical_and(ch != 0, ch != ncc)
    has_next = jnp.logical_and(ch != ncc - 1, ch != nch - 1)
    prev = jnp.where(has_prev, xp_ref[0].astype(F32), 0.0)
    nxt = jnp.where(has_next, xn_ref[0].astype(F32), 0.0)
    p2, p1, n0 = prev[14:15], prev[15:16], nxt[0:1]
    row = lax.broadcasted_iota(I32, xa.shape, 0)
    x_m1 = jnp.where(row == 0, p1, pltpu.roll(xa, 1, 0))
    x_m2 = jnp.where(row == 0, p2, jnp.where(row == 1, p1, pltpu.roll(xa, 2, 0)))
    x_p1 = jnp.where(row == TOK - 1, n0, pltpu.roll(xa, TOK - 1, 0))
    cw = cw_ref[...]
    u = cw[0:1] * x_m2 + cw[1:2] * x_m1 + cw[2:3] * xa + cw[3:4] * x_p1 + cb_ref[...]

    gates = jnp.dot(u.astype(BF16), wgt_ref[...], preferred_element_type=F32) + bgt_ref[...]
    r_gate = _sigmoid(gates[:, :lw])
    i_gate = _sigmoid(gates[:, lw:])
    log_a = (-LRU_C) * r_gate * _softplus(-lam_ref[...])
    a = jnp.exp(log_a)
    bb = jnp.sqrt(1.0 - a * a) * (i_gate * u)
    hs, h_last = _lru_scan(a, bb, h_scr[0:1], rev)
    h_scr[0:1] = h_last
    lru_ref[0] = hs.astype(BF16)

    lg = _log_sigmoid(jnp.dot(lr_ref[0], wg_ref[...], preferred_element_type=F32) + bg_ref[...]) * (1.0 / GLA_TAU)
    ri = lax.broadcasted_iota(I32, (TOK, TOK), 0)
    ci = lax.broadcasted_iota(I32, (TOK, TOK), 1)
    same = (ri >> _CHUNK_SHIFT) == (ci >> _CHUNK_SHIFT)
    tri = jnp.logical_and(same, (ci >= ri) if rev else (ci <= ri))
    g = jnp.dot(tri.astype(F32), lg, precision=HIGHEST, preferred_element_type=F32)
    tot = jnp.dot(same.astype(F32), lg, precision=HIGHEST, preferred_element_type=F32)
    q = q_ref[0].astype(F32)
    k = k_ref[0].astype(F32)
    qg = (q * jnp.exp(g) * (GLA_DK ** -0.5)).astype(BF16)
    kg = (k * jnp.exp(-g)).astype(BF16)
    kd = (k * jnp.exp(tot - g)).astype(BF16)
    dec_all = jnp.exp(tot)
    v = v_ref[0]

    lane = lax.broadcasted_iota(I32, (1, GLA_KEY), 1) >> _DK_SHIFT
    head_masks = [(lane == h) for h in range(GLA_HEADS)]
    n_stack = GLA_HEADS * GLA_CHUNK
    ai = lax.broadcasted_iota(I32, (n_stack, GLA_CHUNK), 0) & (GLA_CHUNK - 1)
    aj = lax.broadcasted_iota(I32, (n_stack, GLA_CHUNK), 1)
    causal = (aj >= ai) if rev else (aj <= ai)

    n_sub = TOK // GLA_CHUNK
    order = range(n_sub - 1, -1, -1) if rev else range(n_sub)
    outs = [None] * n_sub
    st = s_scr[...]
    for n in order:
        sl = slice(n * GLA_CHUNK, (n + 1) * GLA_CHUNK)
        qn = qg[sl]
        qs = jnp.concatenate([jnp.where(head_masks[h], qn, jnp.zeros_like(qn)) for h in range(GLA_HEADS)], axis=0)
        att = lax.dot_general(qs, kg[sl], (((1,), (1,)), ((), ())), preferred_element_type=F32)
        att = jnp.where(causal, att, 0.0).astype(BF16)
        o_all = jnp.dot(att, v[sl], preferred_element_type=F32)
        o_int = lax.dot_general(qs, st.astype(BF16), (((1,), (1,)), ((), ())),
                                preferred_element_type=F32)
        outs[n] = jnp.concatenate(
            [o_all[h * GLA_CHUNK:(h + 1) * GLA_CHUNK, h * GLA_DV:(h + 1) * GLA_DV]
             + o_int[h * GLA_CHUNK:(h + 1) * GLA_CHUNK] for h in range(GLA_HEADS)], axis=1)
        ds_t = lax.dot_general(v[sl], kd[sl], (((0,), (0,)), ((), ())), preferred_element_type=F32)
        ds = jnp.zeros_like(st)
        for h in range(GLA_HEADS):
            ds = ds + jnp.where(head_masks[h], ds_t[h * GLA_DV:(h + 1) * GLA_DV], 0.0)
        st = dec_all[n * GLA_CHUNK:n * GLA_CHUNK + 1] * st + ds
    s_scr[...] = st
    gla_ref[0] = jnp.concatenate(outs, axis=0).astype(BF16)


def _seqmix(rev, ncc, pm, lr, conv_w, conv_b, w_gate, b_gate, lam, wg_pad, bg, lw):
    batch, t, _ = pm.shape
    nch = t // TOK
    nl = nch - ncc
    hb = TOK // 16

    def chunk(s):
        if rev:
            return jnp.where(s < ncc, ncc - 1 - s, ncc + (nl - 1) - (s - ncc))
        return s

    qcol = 2 * lw // GLA_KEY
    vcol = (2 * lw + 2 * GLA_KEY) // GLA_VAL
    const = lambda shape: pl.BlockSpec(shape, lambda b, s: (0,) * len(shape))
    return pl.pallas_call(
        functools.partial(_seqmix_kernel, rev, ncc, nch),
        grid=(batch, nch),
        in_specs=[pl.BlockSpec((1, TOK, lw), lambda b, s: (b, chunk(s), 0)),
                  pl.BlockSpec((1, 16, lw), lambda b, s: (b, jnp.maximum(chunk(s) * hb - 1, 0), 0)),
                  pl.BlockSpec((1, 16, lw), lambda b, s: (b, jnp.minimum((chunk(s) + 1) * hb, nch * hb - 1), 0)),
                  pl.BlockSpec((1, TOK, GLA_KEY), lambda b, s: (b, chunk(s), qcol)),
                  pl.BlockSpec((1, TOK, GLA_KEY), lambda b, s: (b, chunk(s), qcol + 1)),
                  pl.BlockSpec((1, TOK, GLA_VAL), lambda b, s: (b, chunk(s), vcol)),
                  pl.BlockSpec((1, TOK, LANES), lambda b, s: (b, chunk(s), 0)),
                  const((4, lw)), const((1, lw)), const((lw, 2 * lw)), const((1, 2 * lw)), const((1, lw)),
                  const((LANES, GLA_KEY)), const((1, GLA_KEY))],
        out_specs=[pl.BlockSpec((1, TOK, lw), lambda b, s: (b, chunk(s), 0)),
                   pl.BlockSpec((1, TOK, GLA_VAL), lambda b, s: (b, chunk(s), 0))],
        out_shape=[jax.ShapeDtypeStruct((batch, t, lw), BF16),
                   jax.ShapeDtypeStruct((batch, t, GLA_VAL), BF16)],
        scratch_shapes=[pltpu.VMEM((SUBLANES, lw), F32), pltpu.VMEM((GLA_DV, GLA_KEY), F32)],
        compiler_params=_cparams(("arbitrary", "arbitrary")),
        name="seqmix_rev" if rev else "seqmix_fwd",
    )(pm, pm, pm, pm, pm, pm, lr, conv_w, conv_b, w_gate, b_gate, lam, wg_pad, bg)


def _route_epilogue(first, xm, n2_ref, sh2_ref, sc2_ref, wr_ref, br_ref,
                    xmid_ref, h2_ref, cr_ref, cnt_ref, carry):
    d = xm.shape[-1]
    xmid_ref[0] = xm
    h2 = _rms(xm, n2_ref[...]) * (1.0 + sc2_ref[0]) + sh2_ref[0]
    logits = jnp.dot(h2, wr_ref[...], precision=HIGHEST, preferred_element_type=F32) + br_ref[...]

    def col(j):
        return logits[:, j:j + 1]

    lgs = [col(g) for g in range(N_GROUPS)]
    gmax = functools.reduce(jnp.maximum, lgs)
    gi = jnp.where(lgs[0] == gmax, 0, jnp.where(lgs[1] == gmax, 1, jnp.where(lgs[2] == gmax, 2, 3)))
    w_group = 1.0 / functools.reduce(lambda p, q: p + q, [jnp.exp(l - gmax) for l in lgs])
    es = []
    for j in range(EXPERTS_PER_GROUP):
        acc = jnp.zeros_like(gmax)
        for g in range(N_GROUPS):
            acc = acc + jnp.where(gi == g, col(N_GROUPS + g * EXPERTS_PER_GROUP + j), 0.0)
        es.append(acc)
    m1 = functools.reduce(jnp.maximum, es)
    i1 = jnp.where(es[0] == m1, 0, jnp.where(es[1] == m1, 1, jnp.where(es[2] == m1, 2, 3)))
    rest = [jnp.where(i1 == j, -jnp.inf, es[j]) for j in range(EXPERTS_PER_GROUP)]
    m2 = functools.reduce(jnp.maximum, rest)
    i2 = jnp.where(rest[0] == m2, 0, jnp.where(rest[1] == m2, 1, jnp.where(rest[2] == m2, 2, 3)))
    e2 = jnp.exp(m2 - m1)
    w1 = w_group / (1.0 + e2)
    w2 = w_group * e2 / (1.0 + e2)
    first_lo = i1 < i2
    lo = jnp.where(first_lo, i1, i2)
    hi = jnp.where(first_lo, i2, i1)
    w_lo = jnp.where(first_lo, w1, w2)
    w_hi = jnp.where(first_lo, w2, w1)
    pidx = jnp.where(lo == 0, hi - 1, jnp.where(lo == 1, hi + 1, 5))
    cls = gi * PAIRS_PER_GROUP + pidx

    lane = lax.broadcasted_iota(I32, (TOK, LANES), 1)
    h2_ref[:, :d] = h2
    h2_ref[:, d:] = jnp.where(lane == 0, w_lo, jnp.where(lane == 1, w_hi, 0.0))

    @pl.when(first)
    def _():
        carry[...] = jnp.zeros_like(carry)

    onehot = lane == cls
    ri = lax.broadcasted_iota(I32, (TOK, TOK), 0)
    ci = lax.broadcasted_iota(I32, (TOK, TOK), 1)
    tril = (ci <= ri).astype(BF16)
    pref = jnp.dot(tril, onehot.astype(BF16), preferred_element_type=F32)
    base = carry[0:1]
    rank = jnp.sum(jnp.where(onehot, pref - 1.0 + base, 0.0), axis=-1, keepdims=True)
    new_carry = base + jnp.sum(onehot.astype(F32), axis=0, keepdims=True)
    carry[0:1] = new_carry
    cnt_ref[...] = jnp.broadcast_to(new_carry, cnt_ref.shape)
    cr_ref[...] = jnp.where(lane == 0, cls, jnp.where(lane == 1, rank.astype(I32), 0))


def _post0_kernel(ncc, l0_ref, l1_ref, g0_ref, g1_ref, ga_ref, r_ref, x_ref, c_ref,
                  gate_ref, sh2_ref, sc2_ref, n2_ref, gn_ref, wout_ref, wr_ref, br_ref,
                  xmid_ref, h2_ref, cr_ref, cnt_ref, carry):
    b, c = pl.program_id(0), pl.program_id(1)
    lru = l0_ref[0].astype(F32) + l1_ref[0].astype(F32)
    ya = lru * _gelu_tanh(ga_ref[0].astype(F32))
    gla = g0_ref[0].astype(F32) + g1_ref[0].astype(F32)
    parts = []
    for h in range(GLA_HEADS):
        parts.append(_rms(gla[:, h * GLA_DV:(h + 1) * GLA_DV], gn_ref[...]))
    yb = jnp.concatenate(parts, axis=1) * _silu(r_ref[0].astype(F32))
    ycat = jnp.concatenate([ya, yb], axis=1).astype(BF16)
    y = jnp.dot(ycat, wout_ref[...], preferred_element_type=F32)
    x0 = jnp.where(c < ncc, c_ref[0], x_ref[0])
    xm = x0 + gate_ref[0] * y
    _route_epilogue(jnp.logical_and(b == 0, c == 0), xm, n2_ref, sh2_ref, sc2_ref, wr_ref, br_ref,
                    xmid_ref, h2_ref, cr_ref, cnt_ref, carry)


def _post1_kernel(o_ref, x_ref, gate_ref, sh2_ref, sc2_ref, n2_ref, wo_ref, wr_ref, br_ref,
                  xmid_ref, h2_ref, cr_ref, cnt_ref, carry):
    b, c = pl.program_id(0), pl.program_id(1)
    y = jnp.dot(o_ref[0], wo_ref[...], preferred_element_type=F32)
    xm = x_ref[0] + gate_ref[0] * y
    _route_epilogue(jnp.logical_and(b == 0, c == 0), xm, n2_ref, sh2_ref, sc2_ref, wr_ref, br_ref,
                    xmid_ref, h2_ref, cr_ref, cnt_ref, carry)


def _route_out(batch, nch, d):
    n = batch * nch * TOK
    specs = [pl.BlockSpec((1, TOK, d), lambda b, c: (b, c, 0)),
             pl.BlockSpec((TOK, d + WEXT), lambda b, c: (b * nch + c, 0)),
             pl.BlockSpec((TOK, LANES), lambda b, c: (b * nch + c, 0)),
             pl.BlockSpec((SUBLANES, LANES), lambda b, c: (0, 0))]
    shapes = [jax.ShapeDtypeStruct((batch, nch * TOK, d), F32),
              jax.ShapeDtypeStruct((n, d + WEXT), F32),
              jax.ShapeDtypeStruct((n, LANES), I32),
              jax.ShapeDtypeStruct((SUBLANES, LANES), F32)]
    return specs, shapes


def _post0(ncc, lru_f, lru_r, gla_f, gla_r, pm, x, ctx, mods, n2, gn, w_out, wr, br, lw):
    batch, t, _ = pm.shape
    d = x.shape[-1]
    nch = t // TOK
    rcol = (2 * lw + 2 * GLA_KEY + GLA_VAL) // GLA_VAL
    tokspec = lambda w: pl.BlockSpec((1, TOK, w), lambda b, c: (b, c, 0))
    const = lambda shape: pl.BlockSpec(shape, lambda b, c: (0,) * len(shape))
    out_specs, out_shape = _route_out(batch, nch, d)
    return pl.pallas_call(
        functools.partial(_post0_kernel, ncc),
        grid=(batch, nch),
        in_specs=[tokspec(lw), tokspec(lw), tokspec(GLA_VAL), tokspec(GLA_VAL),
                  pl.BlockSpec((1, TOK, lw), lambda b, c: (b, c, 1)),
                  pl.BlockSpec((1, TOK, GLA_VAL), lambda b, c: (b, c, rcol)),
                  pl.BlockSpec((1, TOK, d), lambda b, c: (b, jnp.maximum(c - ncc, 0), 0)),
                  pl.BlockSpec((1, TOK, d), lambda b, c: (b, jnp.minimum(c, ncc - 1), 0)),
                  mods.spec(0, 2), mods.spec(0, 3), mods.spec(0, 4),
                  const((1, d)), const((1, GLA_DV)), const((lw + GLA_VAL, d)),
                  const((d, LANES)), const((1, LANES))],
        out_specs=out_specs, out_shape=out_shape,
        scratch_shapes=[pltpu.VMEM((SUBLANES, LANES), F32)],
        compiler_params=_cparams(("arbitrary", "arbitrary")),
        name="post0",
    )(lru_f, lru_r, gla_f, gla_r, pm, pm, x, ctx, mods.table, mods.table, mods.table, n2, gn, w_out, wr, br)


def _post1(ncc, att, x1, mods, n2, w_o, wr, br):
    batch, seq, aw = att.shape
    d = x1.shape[-1]
    nl = seq // TOK
    const = lambda shape: pl.BlockSpec(shape, lambda b, c: (0,) * len(shape))
    out_specs, out_shape = _route_out(batch, nl, d)
    return pl.pallas_call(
        _post1_kernel,
        grid=(batch, nl),
        in_specs=[pl.BlockSpec((1, TOK, aw), lambda b, c: (b, c, 0)),
                  pl.BlockSpec((1, TOK, d), lambda b, c: (b, c + ncc, 0)),
                  mods.spec(1, 2, True), mods.spec(1, 3, True), mods.spec(1, 4, True),
                  const((1, d)), const((aw, d)), const((d, LANES)), const((1, LANES))],
        out_specs=out_specs, out_shape=out_shape,
        scratch_shapes=[pltpu.VMEM((SUBLANES, LANES), F32)],
        compiler_params=_cparams(("arbitrary", "arbitrary")),
        name="post1",
    )(att, x1, mods.table, mods.table, mods.table, n2, w_o, wr, br)


_PAIR_LO = np.array([0, 0, 0, 1, 1, 2], np.int32)
_PAIR_HI = np.array([1, 2, 3, 2, 3, 3], np.int32)


def _plan(cr, cnt, n_tiles):
    cls, rank = cr[:, 0], cr[:, 1]
    counts = cnt[0, :N_CLASSES].astype(I32)
    tiles = (counts + MOE_TILE - 1) // MOE_TILE
    ends = jnp.cumsum(tiles)
    offs = (ends - tiles) * MOE_TILE
    pos = offs[cls] + rank
    tile_cls = jnp.minimum(jnp.searchsorted(ends, jnp.arange(n_tiles, dtype=I32), side="right"), N_CLASSES - 1)
    grp = tile_cls // PAIRS_PER_GROUP
    pair = tile_cls % PAIRS_PER_GROUP
    lo = grp * EXPERTS_PER_GROUP + jnp.asarray(_PAIR_LO)[pair]
    hi = grp * EXPERTS_PER_GROUP + jnp.asarray(_PAIR_HI)[pair]
    return pos.astype(I32), lo.astype(I32), hi.astype(I32)


def _scatter_rows_kernel(rows, pos_ref, src_ref, init_ref, dst_ref, sem):
    del init_ref
    base = pl.program_id(0) * rows

    def issue(j, carry):
        pltpu.make_async_copy(src_ref.at[pl.ds(base + j, 1)], dst_ref.at[pl.ds(pos_ref[0, 0, j], 1)], sem).start()
        return carry

    lax.fori_loop(0, rows, issue, 0)

    def drain(j, carry):
        pltpu.make_async_copy(src_ref.at[pl.ds(0, 1)], dst_ref.at[pl.ds(0, 1)], sem).wait()
        return carry

    lax.fori_loop(0, rows, drain, 0)


def _scatter_rows(src, pos, n_pad):
    n, w = src.shape
    rows = PERM_ROWS if n % PERM_ROWS == 0 else TOK
    steps = n // rows
    return pl.pallas_call(
        functools.partial(_scatter_rows_kernel, rows),
        grid=(steps,),
        in_specs=[pl.BlockSpec((1, 1, rows), lambda i: (i, 0, 0), memory_space=pltpu.SMEM),
                  pl.BlockSpec(memory_space=pl.ANY),
                  pl.BlockSpec(memory_space=pl.ANY)],
        out_specs=pl.BlockSpec(memory_space=pl.ANY),
        out_shape=jax.ShapeDtypeStruct((n_pad, w), src.dtype),
        scratch_shapes=[pltpu.SemaphoreType.DMA(())],
        input_output_aliases={2: 0},
        compiler_params=pltpu.CompilerParams(dimension_semantics=("arbitrary",), has_side_effects=True),
        name="scatter_rows",
    )(pos.reshape(steps, 1, rows), src, jnp.zeros((n_pad, w), src.dtype))


def _moe_kernel(d, lo_ref, hi_ref, x_ref, w1l, w3l, w2l, w1h, w3h, w2h, y_ref):
    del lo_ref, hi_ref
    x = x_ref[:, :d].astype(BF16)
    w_lo = x_ref[:, d:d + 1]
    w_hi = x_ref[:, d + 1:d + 2]

    def expert(w1, w3, w2):
        a = jnp.dot(x, w1[0], preferred_element_type=F32)
        b = jnp.dot(x, w3[0], preferred_element_type=F32)
        return jnp.dot((_silu(a) * b).astype(BF16), w2[0], preferred_element_type=F32)

    y_ref[...] = w_lo * expert(w1l, w3l, w2l) + w_hi * expert(w1h, w3h, w2h)


def _moe(xs, lo, hi, w1, w3, w2):
    n_pad, wd = xs.shape
    d = wd - WEXT
    hdim = w1.shape[-1]
    n_tiles = n_pad // MOE_TILE
    wl = lambda i, lo, hi: (lo[i], 0, 0)
    wh = lambda i, lo, hi: (hi[i], 0, 0)
    grid_spec = pltpu.PrefetchScalarGridSpec(
        num_scalar_prefetch=2,
        grid=(n_tiles,),
        in_specs=[pl.BlockSpec((MOE_TILE, wd), lambda i, lo, hi: (i, 0)),
                  pl.BlockSpec((1, d, hdim), wl), pl.BlockSpec((1, d, hdim), wl), pl.BlockSpec((1, hdim, d), wl),
                  pl.BlockSpec((1, d, hdim), wh), pl.BlockSpec((1, d, hdim), wh), pl.BlockSpec((1, hdim, d), wh)],
        out_specs=pl.BlockSpec((MOE_TILE, d), lambda i, lo, hi: (i, 0)),
    )
    return pl.pallas_call(
        functools.partial(_moe_kernel, d),
        grid_spec=grid_spec,
        out_shape=jax.ShapeDtypeStruct((n_pad, d), F32),
        compiler_params=_cparams(("arbitrary",)),
        name="moe_experts",
    )(lo, hi, xs, w1, w3, w2, w1, w3, w2)


def _gather_residual_kernel(pos_ref, ys_ref, xm_ref, gate_ref, o_ref, buf, sem):
    def issue(j, carry):
        pltpu.make_async_copy(ys_ref.at[pl.ds(pos_ref[0, 0, j], 1)], buf.at[pl.ds(j, 1)], sem).start()
        return carry

    lax.fori_loop(0, TOK, issue, 0)

    def drain(j, carry):
        pltpu.make_async_copy(ys_ref.at[pl.ds(0, 1)], buf.at[pl.ds(0, 1)], sem).wait()
        return carry

    lax.fori_loop(0, TOK, drain, 0)
    o_ref[0] = xm_ref[0] + gate_ref[0] * buf[...]


def _gather_residual(ys, pos, xmid, gate_spec, mod_table):
    batch, t, d = xmid.shape
    nch = t // TOK
    return pl.pallas_call(
        _gather_residual_kernel,
        grid=(batch, nch),
        in_specs=[pl.BlockSpec((1, 1, TOK), lambda b, c: (b * nch + c, 0, 0), memory_space=pltpu.SMEM),
                  pl.BlockSpec(memory_space=pl.ANY),
                  pl.BlockSpec((1, TOK, d), lambda b, c: (b, c, 0)),
                  gate_spec],
        out_specs=pl.BlockSpec((1, TOK, d), lambda b, c: (b, c, 0)),
        out_shape=jax.ShapeDtypeStruct((batch, t, d), F32),
        scratch_shapes=[pltpu.VMEM((TOK, d), F32), pltpu.SemaphoreType.DMA(())],
        compiler_params=_cparams(("arbitrary", "arbitrary")),
        name="gather_residual",
    )(pos.reshape(batch * nch, 1, TOK), ys, xmid, mod_table)


def _moe_block(h2ext, cr, cnt, xmid, gate_spec, mod_table, w1, w3, w2):
    n = h2ext.shape[0]
    n_tiles = n // MOE_TILE + N_CLASSES
    pos, lo, hi = _plan(cr, cnt, n_tiles)
    xs = _scatter_rows(h2ext, pos, n_tiles * MOE_TILE)
    ys = _moe(xs, lo, hi, w1, w3, w2)
    return _gather_residual(ys, pos, xmid, gate_spec, mod_table)


def _qkv_kernel(ncc, x_ref, g_ref, sh_ref, sc_ref, w_ref, qn_ref, kn_ref, cos_ref, sin_ref,
                q_ref, k_ref, v_ref):
    c = pl.program_id(1)
    latent = c >= ncc
    h = _rms(x_ref[0], g_ref[...]) * (1.0 + sc_ref[0]) + sh_ref[0]
    hb = h.astype(BF16)
    nq = ATT_HEADS * HEAD_DIM
    nk = KV_HEADS * HEAD_DIM
    cos = cos_ref[...]
    sin = sin_ref[...]

    def head(z, gain):
        z = _rms(z, gain)
        rot = z * cos + pltpu.roll(z, HEAD_DIM // 2, 1) * sin
        return jnp.where(latent, rot, z)

    kv = jnp.dot(hb, w_ref[:, nq:], preferred_element_type=F32)
    k_ref[0] = jnp.concatenate(
        [head(kv[:, i * HEAD_DIM:(i + 1) * HEAD_DIM], kn_ref[...]) for i in range(KV_HEADS)], axis=1).astype(BF16)
    v_ref[0] = kv[:, nk:].astype(BF16)

    @pl.when(latent)
    def _():
        qq = jnp.dot(hb, w_ref[:, :nq], preferred_element_type=F32)
        q_ref[0] = jnp.concatenate(
            [head(qq[:, i * HEAD_DIM:(i + 1) * HEAD_DIM], qn_ref[...]) * (HEAD_DIM ** -0.5)
             for i in range(ATT_HEADS)], axis=1).astype(BF16)


def _qkv(ncc, x1, gain, mods, w_qkv, qn, kn, cos, sin):
    batch, t, d = x1.shape
    nch = t // TOK
    seq = t - ncc * TOK
    nq = ATT_HEADS * HEAD_DIM
    nk = KV_HEADS * HEAD_DIM
    const = lambda shape: pl.BlockSpec(shape, lambda b, c: (0,) * len(shape))
    lat = lambda b, c: (b, jnp.maximum(c - ncc, 0), 0)
    return pl.pallas_call(
        functools.partial(_qkv_kernel, ncc),
        grid=(batch, nch),
        in_specs=[pl.BlockSpec((1, TOK, d), lambda b, c: (b, c, 0)),
                  const((1, d)), mods.spec(1, 0), mods.spec(1, 1),
                  const((d, nq + 2 * nk)), const((1, HEAD_DIM)), const((1, HEAD_DIM)),
                  pl.BlockSpec((TOK, HEAD_DIM), lambda b, c: (jnp.maximum(c - ncc, 0), 0)),
                  pl.BlockSpec((TOK, HEAD_DIM), lambda b, c: (jnp.maximum(c - ncc, 0), 0))],
        out_specs=[pl.BlockSpec((1, TOK, nq), lat),
                   pl.BlockSpec((1, TOK, nk), lambda b, c: (b, c, 0)),
                   pl.BlockSpec((1, TOK, nk), lambda b, c: (b, c, 0))],
        out_shape=[jax.ShapeDtypeStruct((batch, seq, nq), BF16),
                   jax.ShapeDtypeStruct((batch, t, nk), BF16),
                   jax.ShapeDtypeStruct((batch, t, nk), BF16)],
        compiler_params=_cparams(("arbitrary", "arbitrary")),
        name="qkv",
    )(x1, gain, mods.table, mods.table, w_qkv, qn, kn, cos, sin)


def _attn_kernel(q_ref, k_ref, v_ref, o_ref):
    q = q_ref[0]
    qs = jnp.concatenate([q[:, g * HEAD_DIM:(g + 1) * HEAD_DIM] for g in range(Q_PER_KV)], axis=0)
    s = lax.dot_general(qs, k_ref[0], (((1,), (1,)), ((), ())), preferred_element_type=F32)
    m = jnp.max(s, axis=-1, keepdims=True)
    p = jnp.exp(s - m)
    l = jnp.sum(p, axis=-1, keepdims=True)
    o = jnp.dot(p.astype(BF16), v_ref[0], preferred_element_type=F32) / l
    o_ref[0] = jnp.concatenate([o[g * ATT_Q:(g + 1) * ATT_Q] for g in range(Q_PER_KV)], axis=1).astype(BF16)


def _attention(q, k, v):
    batch, seq, nq = q.shape
    t = k.shape[1]
    gw = Q_PER_KV * HEAD_DIM
    return pl.pallas_call(
        _attn_kernel,
        grid=(batch, KV_HEADS, seq // ATT_Q),
        in_specs=[pl.BlockSpec((1, ATT_Q, gw), lambda b, h, i: (b, i, h)),
                  pl.BlockSpec((1, t, HEAD_DIM), lambda b, h, i: (b, 0, h)),
                  pl.BlockSpec((1, t, HEAD_DIM), lambda b, h, i: (b, 0, h))],
        out_specs=pl.BlockSpec((1, ATT_Q, gw), lambda b, h, i: (b, i, h)),
        out_shape=jax.ShapeDtypeStruct((batch, seq, nq), BF16),
        compiler_params=_cparams(("arbitrary", "arbitrary", "arbitrary")),
        name="attention",
    )(q, k, v)


def _block_diag(w):
    nb, bs, _ = w.shape
    eye = jnp.eye(nb, dtype=w.dtype)
    return (eye[:, None, :, None] * w[:, :, None, :]).reshape(nb * bs, nb * bs)


def _router_weights(wg, bg, we, be):
    d = wg.shape[0]
    n = N_GROUPS + N_EXPERTS
    wr = jnp.zeros((d, LANES), F32).at[:, :N_GROUPS].set(wg).at[:, N_GROUPS:n].set(we)
    br = jnp.zeros((1, LANES), F32).at[0, :N_GROUPS].set(bg).at[0, N_GROUPS:n].set(be)
    return wr, br


def _rope_tables(seq):
    rows = seq // GRID_W
    row = np.repeat(np.arange(rows, dtype=np.float32), GRID_W)
    col = np.tile(np.arange(GRID_W, dtype=np.float32), rows)
    ppa = HEAD_DIM // 4
    freqs = (ROPE_THETA ** (-np.arange(ppa, dtype=np.float32) / ppa)).astype(np.float32)
    ang = np.concatenate([row[:, None] * freqs, col[:, None] * freqs], axis=-1)
    cos, sin = np.cos(ang), np.sin(ang)
    return (jnp.asarray(np.concatenate([cos, cos], axis=-1), F32),
            jnp.asarray(np.concatenate([-sin, sin], axis=-1), F32))


_HALF_SPLIT = np.concatenate([np.arange(0, HEAD_DIM, 2), np.arange(1, HEAD_DIM, 2)])


def kernel(x, c, ctx, c_ctx, norm1, norm2, w_ada, b_ada, ev_w_in, ev_conv_w, ev_conv_b, ev_lru_wa, ev_lru_ba, ev_lru_wi, ev_lru_bi, ev_lru_lam, ev_gla_wg, ev_gla_bg, ev_gla_norm, ev_w_out, od_w_qkv, od_q_norm, od_k_norm, od_w_o, moe_wg, moe_bg, moe_we, moe_be, moe_w1, moe_w3, moe_w2):
    batch, seq, d = x.shape
    ctx_len = ctx.shape[1]
    assert seq % TOK == 0 and ctx_len % TOK == 0 and d % GLA_VAL == 0 and seq % GRID_W == 0
    ncc = ctx_len // TOK
    lw = d // 2

    rows = -(-(batch + 1) // SUBLANES) * SUBLANES
    cv = jnp.zeros((rows, d), F32).at[:batch].set(c).at[batch].set(c_ctx)
    table = _adaln(cv, w_ada, b_ada)
    mods = _Mods(table.reshape(table.shape[0] * rows * 6, 1, d), rows, batch, ncc)

    w_in = ev_w_in[0]
    nm = 2 * lw + 2 * GLA_KEY + 2 * GLA_VAL
    w_main = w_in[:, :nm].astype(BF16)
    w_lr = jnp.zeros((d, LANES), F32).at[:, :2 * GLA_RANK].set(w_in[:, nm:]).astype(BF16)
    pm, lr = _inproj(x, ctx, norm1[0][None], mods, w_main, w_lr)

    mixed = []
    for dr in range(2):
        w_gate = jnp.concatenate([_block_diag(ev_lru_wa[0, dr]), _block_diag(ev_lru_wi[0, dr])], axis=1).astype(BF16)
        b_gate = jnp.concatenate([ev_lru_ba[0, dr], ev_lru_bi[0, dr]])[None]
        wg_pad = jnp.zeros((LANES, GLA_KEY), F32).at[dr * GLA_RANK:(dr + 1) * GLA_RANK].set(ev_gla_wg[0, dr]).astype(BF16)
        mixed.append(_seqmix(dr == 1, ncc, pm, lr, ev_conv_w[0], ev_conv_b[0][None], w_gate, b_gate,
                             ev_lru_lam[0, dr][None], wg_pad, ev_gla_bg[0, dr][None], lw))
    (lru_f, gla_f), (lru_r, gla_r) = mixed

    wr0, br0 = _router_weights(moe_wg[0], moe_bg[0], moe_we[0], moe_be[0])
    xmid0, h2e0, cr0, cnt0 = _post0(ncc, lru_f, lru_r, gla_f, gla_r, pm, x, ctx, mods, norm2[0][None],
                                    ev_gla_norm[0][None], ev_w_out[0].astype(BF16), wr0, br0, lw)
    x1 = _moe_block(h2e0, cr0, cnt0, xmid0, mods.spec(0, 5), mods.table,
                    moe_w1[0].astype(BF16), moe_w3[0].astype(BF16), moe_w2[0].astype(BF16))

    nq = ATT_HEADS * HEAD_DIM
    nk = KV_HEADS * HEAD_DIM
    perm = np.concatenate([h * HEAD_DIM + _HALF_SPLIT for h in range(ATT_HEADS + KV_HEADS)]
                          + [np.arange(nq + nk, nq + 2 * nk)])
    w_qkv = od_w_qkv[0][:, perm].astype(BF16)
    cos, sin = _rope_tables(seq)
    q, k, v = _qkv(ncc, x1, norm1[1][None], mods, w_qkv, od_q_norm[0][_HALF_SPLIT][None],
                   od_k_norm[0][_HALF_SPLIT][None], cos, sin)
    att = _attention(q, k, v)
    wr1, br1 = _router_weights(moe_wg[1], moe_bg[1], moe_we[1], moe_be[1])
    xmid1, h2e1, cr1, cnt1 = _post1(ncc, att, x1, mods, norm2[1][None], od_w_o[0].astype(BF16), wr1, br1)
    return _moe_block(h2e1, cr1, cnt1, xmid1, mods.spec(1, 5, True), mods.table,
                      moe_w1[1].astype(BF16), moe_w3[1].astype(BF16), moe_w2[1].astype(BF16))
```

```python
import functools

import numpy as np
import jax
import jax.numpy as jnp
from jax import lax
from jax.experimental import pallas as pl
from jax.experimental.pallas import tpu as pltpu

F32 = jnp.float32
BF16 = jnp.bfloat16
I32 = jnp.int32
HIGHEST = lax.Precision.HIGHEST

EPS = 1e-6
GRID_W = 64
LRU_BLOCKS = 8
LRU_C = 8.0
GLA_HEADS = 4
GLA_DK = 64
GLA_DV = 128
GLA_KEY = GLA_HEADS * GLA_DK
GLA_VAL = GLA_HEADS * GLA_DV
GLA_RANK = 16
GLA_TAU = 16.0
GLA_CHUNK = 64
_CHUNK_SHIFT = GLA_CHUNK.bit_length() - 1
_DK_SHIFT = GLA_DK.bit_length() - 1
ATT_HEADS = 8
KV_HEADS = 2
Q_PER_KV = ATT_HEADS // KV_HEADS
HEAD_DIM = 128
ROPE_THETA = 10000.0
N_GROUPS = 4
EXPERTS_PER_GROUP = 4
N_EXPERTS = N_GROUPS * EXPERTS_PER_GROUP
PAIRS_PER_GROUP = 6
N_CLASSES = N_GROUPS * PAIRS_PER_GROUP

LANES = 128
SUBLANES = 8
TOK = 256
MOE_TILE = 256
ATT_Q = 256
PERM_ROWS = 512
DMA_UNROLL = 8
WEXT = LANES
ROUTER_ROWS = 32
VMEM_LIMIT = 56 * 1024 * 1024


def _cparams(sem):
    return pltpu.CompilerParams(dimension_semantics=sem, vmem_limit_bytes=VMEM_LIMIT)


def _rms(x, g):
    return x * lax.rsqrt(jnp.mean(x * x, axis=-1, keepdims=True) + EPS) * g


def _sigmoid(x):
    return 1.0 / (1.0 + jnp.exp(-x))


def _silu(x):
    return x * _sigmoid(x)


def _gelu_tanh(x):
    return 0.5 * x * (1.0 + jnp.tanh(np.sqrt(2.0 / np.pi).astype(np.float32) * (x + 0.044715 * (x * x * x))))


def _softplus(x):
    return jnp.maximum(x, 0.0) + jnp.log(1.0 + jnp.exp(-jnp.abs(x)))


def _log_sigmoid(x):
    return -_softplus(-x)


def _adaln_kernel(cv_ref, w_ref, b_ref, o_ref):
    s = _silu(cv_ref[...])
    o_ref[0] = jnp.dot(s, w_ref[0], precision=HIGHEST, preferred_element_type=F32) + b_ref[0]


def _adaln(cv, w_ada, b_ada):
    depth, d, n6 = w_ada.shape
    rows = cv.shape[0]
    tn = 6 * d // 4
    return pl.pallas_call(
        _adaln_kernel,
        grid=(depth, n6 // tn),
        in_specs=[pl.BlockSpec((rows, d), lambda l, j: (0, 0)),
                  pl.BlockSpec((1, d, tn), lambda l, j: (l, 0, j)),
                  pl.BlockSpec((1, 1, tn), lambda l, j: (l, 0, j))],
        out_specs=pl.BlockSpec((1, rows, tn), lambda l, j: (l, 0, j)),
        out_shape=jax.ShapeDtypeStruct((depth, rows, n6), F32),
        compiler_params=_cparams(("arbitrary", "arbitrary")),
        name="adaln",
    )(cv, w_ada, b_ada.reshape(depth, 1, n6))


class _Mods:
    def __init__(self, table, rows, batch, ncc):
        self.table, self.rows, self.batch, self.ncc = table, rows, batch, ncc
        self.d = table.shape[-1]

    def spec(self, layer, j, latent_only=False):
        rows, batch, ncc = self.rows, self.batch, self.ncc

        def imap(b, c):
            r = b if latent_only else jnp.where(c < ncc, batch, b)
            return ((layer * rows + r) * 6 + j, 0, 0)

        return pl.BlockSpec((1, 1, self.d), imap)


def _inproj_kernel(ncc, x_ref, c_ref, g_ref, sh_ref, sc_ref, w_ref, wlr_ref, pm_ref, lr_ref):
    c = pl.program_id(1)
    xf = jnp.where(c < ncc, c_ref[0], x_ref[0])
    h = _rms(xf, g_ref[...]) * (1.0 + sc_ref[0]) + sh_ref[0]
    hb = h.astype(BF16)
    pm_ref[0] = jnp.dot(hb, w_ref[...], preferred_element_type=F32).astype(BF16)
    lr_ref[0] = jnp.dot(hb, wlr_ref[...], preferred_element_type=F32).astype(BF16)


def _inproj(x, ctx, gain, mods, w_main, w_lr):
    batch, seq, d = x.shape
    ncc = ctx.shape[1] // TOK
    nch = ncc + seq // TOK
    nm = w_main.shape[1]
    return pl.pallas_call(
        functools.partial(_inproj_kernel, ncc),
        grid=(batch, nch),
        in_specs=[pl.BlockSpec((1, TOK, d), lambda b, c: (b, jnp.maximum(c - ncc, 0), 0)),
                  pl.BlockSpec((1, TOK, d), lambda b, c: (b, jnp.minimum(c, ncc - 1), 0)),
                  pl.BlockSpec((1, d), lambda b, c: (0, 0)),
                  mods.spec(0, 0), mods.spec(0, 1),
                  pl.BlockSpec((d, nm), lambda b, c: (0, 0)),
                  pl.BlockSpec((d, LANES), lambda b, c: (0, 0))],
        out_specs=[pl.BlockSpec((1, TOK, nm), lambda b, c: (b, c, 0)),
                   pl.BlockSpec((1, TOK, LANES), lambda b, c: (b, c, 0))],
        out_shape=[jax.ShapeDtypeStruct((batch, nch * TOK, nm), BF16),
                   jax.ShapeDtypeStruct((batch, nch * TOK, LANES), BF16)],
        compiler_params=_cparams(("arbitrary", "arbitrary")),
        name="inproj",
    )(x, ctx, gain, mods.table, mods.table, w_main, w_lr)


def _lru_scan(a, b, h0, rev):
    sub = lax.broadcasted_iota(I32, a.shape, 0) & (SUBLANES - 1)
    for dist in (1, 2, 4):
        shift = TOK - dist if rev else dist
        a_s = pltpu.roll(a, shift, 0)
        b_s = pltpu.roll(b, shift, 0)
        m = (sub < SUBLANES - dist) if rev else (sub >= dist)
        b = jnp.where(m, a * b_s + b, b)
        a = jnp.where(m, a * a_s, a)
    n_groups = TOK // SUBLANES
    order = range(n_groups - 1, -1, -1) if rev else range(n_groups)
    outs = [None] * n_groups
    h = h0
    for r in order:
        hr = a[r * SUBLANES:(r + 1) * SUBLANES] * h + b[r * SUBLANES:(r + 1) * SUBLANES]
        outs[r] = hr
        h = hr[0:1] if rev else hr[SUBLANES - 1:SUBLANES]
    return jnp.concatenate(outs, axis=0), h


def _seqmix_kernel(rev, ncc, nch, xa_ref, xp_ref, xn_ref, q_ref, k_ref, v_ref, lr_ref,
                   cw_ref, cb_ref, wgt_ref, bgt_ref, lam_ref, wg_ref, bg_ref,
                   lru_ref, gla_ref, h_scr, s_scr):
    s = pl.program_id(1)
    nl = nch - ncc
    if rev:
        ch = jnp.where(s < ncc, ncc - 1 - s, ncc + (nl - 1) - (s - ncc))
    else:
        ch = s

    @pl.when(s == 0)
    def _():
        h_scr[...] = jnp.zeros_like(h_scr)
        s_scr[...] = jnp.zeros_like(s_scr)

    lw = xa_ref.shape[-1]
    xa = xa_ref[0].astype(F32)
    has_prev = jnp.logical_and(ch != 0, ch != ncc)
    has_next = jnp.logical_and(ch != ncc - 1, ch != nch - 1)
    prev = jnp.where(has_prev, xp_ref[0].astype(F32), 0.0)
    nxt = jnp.where(has_next, xn_ref[0].astype(F32), 0.0)
    p2, p1, n0 = prev[14:15], prev[15:16], nxt[0:1]
    row = lax.broadcasted_iota(I32, xa.shape, 0)
    x_m1 = jnp.where(row == 0, p1, pltpu.roll(xa, 1, 0))
    x_m2 = jnp.where(row == 0, p2, jnp.where(row == 1, p1, pltpu.roll(xa, 2, 0)))
    x_p1 = jnp.where(row == TOK - 1, n0, pltpu.roll(xa, TOK - 1, 0))
    cw = cw_ref[...]
    u = cw[0:1] * x_m2 + cw[1:2] * x_m1 + cw[2:3] * xa + cw[3:4] * x_p1 + cb_ref[...]

    gates = jnp.dot(u.astype(BF16), wgt_ref[...], preferred_element_type=F32) + bgt_ref[...]
    r_gate = _sigmoid(gates[:, :lw])
    i_gate = _sigmoid(gates[:, lw:])
    log_a = (-LRU_C) * r_gate * _softplus(-lam_ref[...])
    a = jnp.exp(log_a)
    bb = jnp.sqrt(1.0 - a * a) * (i_gate * u)
    hs, h_last = _lru_scan(a, bb, h_scr[0:1], rev)
    h_scr[0:1] = h_last
    lru_ref[0] = hs.astype(BF16)

    lg = _log_sigmoid(jnp.dot(lr_ref[0], wg_ref[...], preferred_element_type=F32) + bg_ref[...]) * (1.0 / GLA_TAU)
    ri = lax.broadcasted_iota(I32, (TOK, TOK), 0)
    ci = lax.broadcasted_iota(I32, (TOK, TOK), 1)
    same = (ri >> _CHUNK_SHIFT) == (ci >> _CHUNK_SHIFT)
    tri = jnp.logical_and(same, (ci >= ri) if rev else (ci <= ri))
    g = jnp.dot(tri.astype(F32), lg, precision=HIGHEST, preferred_element_type=F32)
    tot = jnp.dot(same.astype(F32), lg, precision=HIGHEST, preferred_element_type=F32)
    q = q_ref[0].astype(F32)
    k = k_ref[0].astype(F32)
    qg = (q * jnp.exp(g) * (GLA_DK ** -0.5)).astype(BF16)
    kg = (k * jnp.exp(-g)).astype(BF16)
    kd = (k * jnp.exp(tot - g)).astype(BF16)
    dec_all = jnp.exp(tot)
    v = v_ref[0]

    lane = lax.broadcasted_iota(I32, (1, GLA_KEY), 1) >> _DK_SHIFT
    head_masks = [(lane == h) for h in range(GLA_HEADS)]
    n_stack = GLA_HEADS * GLA_CHUNK
    ai = lax.broadcasted_iota(I32, (n_stack, GLA_CHUNK), 0) & (GLA_CHUNK - 1)
    aj = lax.broadcasted_iota(I32, (n_stack, GLA_CHUNK), 1)
    causal = (aj >= ai) if rev else (aj <= ai)

    n_sub = TOK // GLA_CHUNK
    order = range(n_sub - 1, -1, -1) if rev else range(n_sub)
    outs = [None] * n_sub
    st = s_scr[...]
    for n in order:
        sl = slice(n * GLA_CHUNK, (n + 1) * GLA_CHUNK)
        qn = qg[sl]
        qs = jnp.concatenate([jnp.where(head_masks[h], qn, jnp.zeros_like(qn)) for h in range(GLA_HEADS)], axis=0)
        att = lax.dot_general(qs, kg[sl], (((1,), (1,)), ((), ())), preferred_element_type=F32)
        att = jnp.where(causal, att, 0.0).astype(BF16)
        o_all = jnp.dot(att, v[sl], preferred_element_type=F32)
        o_int = lax.dot_general(qs, st.astype(BF16), (((1,), (1,)), ((), ())),
                                preferred_element_type=F32)
        outs[n] = jnp.concatenate(
            [o_all[h * GLA_CHUNK:(h + 1) * GLA_CHUNK, h * GLA_DV:(h + 1) * GLA_DV]
             + o_int[h * GLA_CHUNK:(h + 1) * GLA_CHUNK] for h in range(GLA_HEADS)], axis=1)
        ds_t = lax.dot_general(v[sl], kd[sl], (((0,), (0,)), ((), ())), preferred_element_type=F32)
        ds = jnp.zeros_like(st)
        for h in range(GLA_HEADS):
            ds = ds + jnp.where(head_masks[h], ds_t[h * GLA_DV:(h + 1) * GLA_DV], 0.0)
        st = dec_all[n * GLA_CHUNK:n * GLA_CHUNK + 1] * st + ds
    s_scr[...] = st
    gla_ref[0] = jnp.concatenate(outs, axis=0).astype(BF16)


def _seqmix(rev, ncc, pm, lr, conv_w, conv_b, w_gate, b_gate, lam, wg_pad, bg, lw):
    batch, t, _ = pm.shape
    nch = t // TOK
    nl = nch - ncc
    hb = TOK // 16

    def chunk(s):
        if rev:
            return jnp.where(s < ncc, ncc - 1 - s, ncc + (nl - 1) - (s - ncc))
        return s

    qcol = 2 * lw // GLA_KEY
    vcol = (2 * lw + 2 * GLA_KEY) // GLA_VAL
    const = lambda shape: pl.BlockSpec(shape, lambda b, s: (0,) * len(shape))
    return pl.pallas_call(
        functools.partial(_seqmix_kernel, rev, ncc, nch),
        grid=(batch, nch),
        in_specs=[pl.BlockSpec((1, TOK, lw), lambda b, s: (b, chunk(s), 0)),
                  pl.BlockSpec((1, 16, lw), lambda b, s: (b, jnp.maximum(chunk(s) * hb - 1, 0), 0)),
                  pl.BlockSpec((1, 16, lw), lambda b, s: (b, jnp.minimum((chunk(s) + 1) * hb, nch * hb - 1), 0)),
                  pl.BlockSpec((1, TOK, GLA_KEY), lambda b, s: (b, chunk(s), qcol)),
                  pl.BlockSpec((1, TOK, GLA_KEY), lambda b, s: (b, chunk(s), qcol + 1)),
                  pl.BlockSpec((1, TOK, GLA_VAL), lambda b, s: (b, chunk(s), vcol)),
                  pl.BlockSpec((1, TOK, LANES), lambda b, s: (b, chunk(s), 0)),
                  const((4, lw)), const((1, lw)), const((lw, 2 * lw)), const((1, 2 * lw)), const((1, lw)),
                  const((LANES, GLA_KEY)), const((1, GLA_KEY))],
        out_specs=[pl.BlockSpec((1, TOK, lw), lambda b, s: (b, chunk(s), 0)),
                   pl.BlockSpec((1, TOK, GLA_VAL), lambda b, s: (b, chunk(s), 0))],
        out_shape=[jax.ShapeDtypeStruct((batch, t, lw), BF16),
                   jax.ShapeDtypeStruct((batch, t, GLA_VAL), BF16)],
        scratch_shapes=[pltpu.VMEM((SUBLANES, lw), F32), pltpu.VMEM((GLA_DV, GLA_KEY), F32)],
        compiler_params=_cparams(("arbitrary", "arbitrary")),
        name="seqmix_rev" if rev else "seqmix_fwd",
    )(pm, pm, pm, pm, pm, pm, lr, conv_w, conv_b, w_gate, b_gate, lam, wg_pad, bg)


def _route_epilogue(first, xm, n2_ref, sh2_ref, sc2_ref, wr_ref, br_ref,
                    xmid_ref, h2_ref, cr_ref, cnt_ref, carry):
    d = xm.shape[-1]
    xmid_ref[0] = xm
    h2 = _rms(xm, n2_ref[...]) * (1.0 + sc2_ref[0]) + sh2_ref[0]
    h_hi = h2.astype(BF16)
    h_lo = (h2 - h_hi.astype(F32)).astype(BF16)
    nt = (((1,), (1,)), ((), ()))
    o1 = lax.dot_general(wr_ref[...], h_hi, nt, preferred_element_type=F32)
    o2 = lax.dot_general(wr_ref[:ROUTER_ROWS], h_lo, nt, preferred_element_type=F32)
    logits = o1[:ROUTER_ROWS] + o1[ROUTER_ROWS:] + o2 + br_ref[...]

    def col(j):
        return logits[j:j + 1]

    lgs = [col(g) for g in range(N_GROUPS)]
    gmax = functools.reduce(jnp.maximum, lgs)
    gi = jnp.where(lgs[0] == gmax, 0, jnp.where(lgs[1] == gmax, 1, jnp.where(lgs[2] == gmax, 2, 3)))
    w_group = 1.0 / functools.reduce(lambda p, q: p + q, [jnp.exp(l - gmax) for l in lgs])
    es = []
    for j in range(EXPERTS_PER_GROUP):
        acc = jnp.zeros_like(gmax)
        for g in range(N_GROUPS):
            acc = acc + jnp.where(gi == g, col(N_GROUPS + g * EXPERTS_PER_GROUP + j), 0.0)
        es.append(acc)
    m1 = functools.reduce(jnp.maximum, es)
    i1 = jnp.where(es[0] == m1, 0, jnp.where(es[1] == m1, 1, jnp.where(es[2] == m1, 2, 3)))
    rest = [jnp.where(i1 == j, -jnp.inf, es[j]) for j in range(EXPERTS_PER_GROUP)]
    m2 = functools.reduce(jnp.maximum, rest)
    i2 = jnp.where(rest[0] == m2, 0, jnp.where(rest[1] == m2, 1, jnp.where(rest[2] == m2, 2, 3)))
    e2 = jnp.exp(m2 - m1)
    w1 = w_group / (1.0 + e2)
    w2 = w_group * e2 / (1.0 + e2)
    first_lo = i1 < i2
    lo = jnp.where(first_lo, i1, i2)
    hi = jnp.where(first_lo, i2, i1)
    w_lo = jnp.where(first_lo, w1, w2)
    w_hi = jnp.where(first_lo, w2, w1)
    pidx = jnp.where(lo == 0, hi - 1, jnp.where(lo == 1, hi + 1, 5))
    cls = gi * PAIRS_PER_GROUP + pidx

    def terms(w):
        t1 = w.astype(BF16).astype(F32)
        t2 = (w - t1).astype(BF16).astype(F32)
        t3 = (w - t1 - t2).astype(BF16).astype(F32)
        return [t1, t2, t3]

    sub = lax.broadcasted_iota(I32, (2 * SUBLANES, TOK), 0)
    stacked = jnp.zeros((2 * SUBLANES, TOK), F32)
    for r, term in enumerate(terms(w_lo) + terms(w_hi)):
        stacked = jnp.where(sub == r, term, stacked)
    er = lax.broadcasted_iota(I32, (2 * SUBLANES, LANES), 0)
    el = lax.broadcasted_iota(I32, (2 * SUBLANES, LANES), 1)
    place = jnp.logical_or(jnp.logical_and(er < 3, el == 0),
                           jnp.logical_and(jnp.logical_and(er >= 3, er < 6), el == 1))
    w_cols = lax.dot_general(stacked.astype(BF16), place.astype(BF16), (((0,), (0,)), ((), ())),
                             preferred_element_type=F32)
    h2_ref[:, :d] = h2
    h2_ref[:, d:] = w_cols

    @pl.when(first)
    def _():
        carry[...] = jnp.zeros_like(carry)

    krow = lax.broadcasted_iota(I32, (ROUTER_ROWS, TOK), 0)
    onehot = krow == cls
    ri = lax.broadcasted_iota(I32, (TOK, TOK), 0)
    ci = lax.broadcasted_iota(I32, (TOK, TOK), 1)
    triu = (ri <= ci).astype(BF16)
    pref = jnp.dot(onehot.astype(BF16), triu, preferred_element_type=F32)
    base = carry[:, 0:1]
    rank = jnp.sum(jnp.where(onehot, pref - 1.0 + base, 0.0), axis=0, keepdims=True)
    new_carry = jnp.broadcast_to(base + jnp.sum(onehot.astype(F32), axis=1, keepdims=True), carry.shape)
    carry[...] = new_carry
    cnt_ref[...] = new_carry
    sub8 = lax.broadcasted_iota(I32, (SUBLANES, TOK), 0)
    cr_ref[0] = jnp.where(sub8 == 0, cls, jnp.where(sub8 == 1, rank.astype(I32), 0))


def _post0_kernel(ncc, l0_ref, l1_ref, g0_ref, g1_ref, ga_ref, r_ref, x_ref, c_ref,
                  gate_ref, sh2_ref, sc2_ref, n2_ref, gn_ref, wout_ref, wr_ref, br_ref,
                  xmid_ref, h2_ref, cr_ref, cnt_ref, carry):
    b, c = pl.program_id(0), pl.program_id(1)
    lru = l0_ref[0].astype(F32) + l1_ref[0].astype(F32)
    ya = lru * _gelu_tanh(ga_ref[0].astype(F32))
    gla = g0_ref[0].astype(F32) + g1_ref[0].astype(F32)
    parts = []
    for h in range(GLA_HEADS):
        parts.append(_rms(gla[:, h * GLA_DV:(h + 1) * GLA_DV], gn_ref[...]))
    yb = jnp.concatenate(parts, axis=1) * _silu(r_ref[0].astype(F32))
    ycat = jnp.concatenate([ya, yb], axis=1).astype(BF16)
    y = jnp.dot(ycat, wout_ref[...], preferred_element_type=F32)
    x0 = jnp.where(c < ncc, c_ref[0], x_ref[0])
    xm = x0 + gate_ref[0] * y
    _route_epilogue(jnp.logical_and(b == 0, c == 0), xm, n2_ref, sh2_ref, sc2_ref, wr_ref, br_ref,
                    xmid_ref, h2_ref, cr_ref, cnt_ref, carry)


def _post1_kernel(o_ref, x_ref, gate_ref, sh2_ref, sc2_ref, n2_ref, wo_ref, wr_ref, br_ref,
                  xmid_ref, h2_ref, cr_ref, cnt_ref, carry):
    b, c = pl.program_id(0), pl.program_id(1)
    y = jnp.dot(o_ref[0], wo_ref[...], preferred_element_type=F32)
    xm = x_ref[0] + gate_ref[0] * y
    _route_epilogue(jnp.logical_and(b == 0, c == 0), xm, n2_ref, sh2_ref, sc2_ref, wr_ref, br_ref,
                    xmid_ref, h2_ref, cr_ref, cnt_ref, carry)


def _route_out(batch, nch, d):
    n = batch * nch * TOK
    specs = [pl.BlockSpec((1, TOK, d), lambda b, c: (b, c, 0)),
             pl.BlockSpec((TOK, d + WEXT), lambda b, c: (b * nch + c, 0)),
             pl.BlockSpec((1, SUBLANES, TOK), lambda b, c: (b * nch + c, 0, 0)),
             pl.BlockSpec((ROUTER_ROWS, LANES), lambda b, c: (0, 0))]
    shapes = [jax.ShapeDtypeStruct((batch, nch * TOK, d), F32),
              jax.ShapeDtypeStruct((n, d + WEXT), F32),
              jax.ShapeDtypeStruct((batch * nch, SUBLANES, TOK), I32),
              jax.ShapeDtypeStruct((ROUTER_ROWS, LANES), F32)]
    return specs, shapes


def _post0(ncc, lru_f, lru_r, gla_f, gla_r, pm, x, ctx, mods, n2, gn, w_out, wr, br, lw):
    batch, t, _ = pm.shape
    d = x.shape[-1]
    nch = t // TOK
    rcol = (2 * lw + 2 * GLA_KEY + GLA_VAL) // GLA_VAL
    tokspec = lambda w: pl.BlockSpec((1, TOK, w), lambda b, c: (b, c, 0))
    const = lambda shape: pl.BlockSpec(shape, lambda b, c: (0,) * len(shape))
    out_specs, out_shape = _route_out(batch, nch, d)
    return pl.pallas_call(
        functools.partial(_post0_kernel, ncc),
        grid=(batch, nch),
        in_specs=[tokspec(lw), tokspec(lw), tokspec(GLA_VAL), tokspec(GLA_VAL),
                  pl.BlockSpec((1, TOK, lw), lambda b, c: (b, c, 1)),
                  pl.BlockSpec((1, TOK, GLA_VAL), lambda b, c: (b, c, rcol)),
                  pl.BlockSpec((1, TOK, d), lambda b, c: (b, jnp.maximum(c - ncc, 0), 0)),
                  pl.BlockSpec((1, TOK, d), lambda b, c: (b, jnp.minimum(c, ncc - 1), 0)),
                  mods.spec(0, 2), mods.spec(0, 3), mods.spec(0, 4),
                  const((1, d)), const((1, GLA_DV)), const((lw + GLA_VAL, d)),
                  const((2 * ROUTER_ROWS, d)), const((ROUTER_ROWS, 1))],
        out_specs=out_specs, out_shape=out_shape,
        scratch_shapes=[pltpu.VMEM((ROUTER_ROWS, LANES), F32)],
        compiler_params=_cparams(("arbitrary", "arbitrary")),
        name="post0",
    )(lru_f, lru_r, gla_f, gla_r, pm, pm, x, ctx, mods.table, mods.table, mods.table, n2, gn, w_out, wr, br)


def _post1(ncc, att, x1, mods, n2, w_o, wr, br):
    batch, seq, aw = att.shape
    d = x1.shape[-1]
    nl = seq // TOK
    const = lambda shape: pl.BlockSpec(shape, lambda b, c: (0,) * len(shape))
    out_specs, out_shape = _route_out(batch, nl, d)
    return pl.pallas_call(
        _post1_kernel,
        grid=(batch, nl),
        in_specs=[pl.BlockSpec((1, TOK, aw), lambda b, c: (b, c, 0)),
                  pl.BlockSpec((1, TOK, d), lambda b, c: (b, c + ncc, 0)),
                  mods.spec(1, 2, True), mods.spec(1, 3, True), mods.spec(1, 4, True),
                  const((1, d)), const((aw, d)), const((2 * ROUTER_ROWS, d)), const((ROUTER_ROWS, 1))],
        out_specs=out_specs, out_shape=out_shape,
        scratch_shapes=[pltpu.VMEM((ROUTER_ROWS, LANES), F32)],
        compiler_params=_cparams(("arbitrary", "arbitrary")),
        name="post1",
    )(att, x1, mods.table, mods.table, mods.table, n2, w_o, wr, br)


_PAIR_LO = np.array([0, 0, 0, 1, 1, 2], np.int32)
_PAIR_HI = np.array([1, 2, 3, 2, 3, 3], np.int32)


def _plan(cr, cnt, n_tiles):
    cls, rank = cr[:, 0, :].reshape(-1), cr[:, 1, :].reshape(-1)
    counts = cnt[:N_CLASSES, 0].astype(I32)
    tiles = (counts + MOE_TILE - 1) // MOE_TILE
    ends = jnp.cumsum(tiles)
    offs = (ends - tiles) * MOE_TILE
    classes = jnp.arange(N_CLASSES, dtype=I32)
    pos = jnp.sum(jnp.where(cls[:, None] == classes[None, :], offs[None, :], 0), axis=1) + rank
    tile_ids = jnp.arange(n_tiles, dtype=I32)
    tile_cls = jnp.minimum(jnp.sum((tile_ids[:, None] >= ends[None, :]).astype(I32), axis=1), N_CLASSES - 1)
    grp = tile_cls // PAIRS_PER_GROUP
    pair = tile_cls % PAIRS_PER_GROUP
    lo = grp * EXPERTS_PER_GROUP + jnp.asarray(_PAIR_LO)[pair]
    hi = grp * EXPERTS_PER_GROUP + jnp.asarray(_PAIR_HI)[pair]
    return pos.astype(I32), lo.astype(I32), hi.astype(I32)


def _scatter_rows_kernel(rows, pos_ref, src_ref, init_ref, dst_ref, sem):
    del init_ref

    def row_copy(j, p):
        return pltpu.make_async_copy(src_ref.at[pl.ds(j, 1)], dst_ref.at[pl.ds(p, 1)], sem)

    def issue(j, carry):
        row_copy(j, pos_ref[0, 0, j]).start()
        return carry

    lax.fori_loop(0, rows, issue, 0, unroll=DMA_UNROLL)

    def drain(j, carry):
        row_copy(0, 0).wait()
        return carry

    lax.fori_loop(0, rows, drain, 0, unroll=DMA_UNROLL)


def _scatter_rows(src, pos, n_pad):
    n, w = src.shape
    rows = PERM_ROWS if n % PERM_ROWS == 0 else TOK
    steps = n // rows
    return pl.pallas_call(
        functools.partial(_scatter_rows_kernel, rows),
        grid=(steps,),
        in_specs=[pl.BlockSpec((1, 1, rows), lambda i: (i, 0, 0), memory_space=pltpu.SMEM),
                  pl.BlockSpec((rows, w), lambda i: (i, 0)),
                  pl.BlockSpec(memory_space=pl.ANY)],
        out_specs=pl.BlockSpec(memory_space=pl.ANY),
        out_shape=jax.ShapeDtypeStruct((n_pad, w), src.dtype),
        scratch_shapes=[pltpu.SemaphoreType.DMA(())],
        input_output_aliases={2: 0},
        compiler_params=pltpu.CompilerParams(dimension_semantics=("arbitrary",), has_side_effects=True),
        name="scatter_rows",
    )(pos.reshape(steps, 1, rows), src, jnp.zeros((n_pad, w), src.dtype))


def _moe_kernel(d, lo_ref, hi_ref, x_ref, w1l, w3l, w2l, w1h, w3h, w2h, y_ref):
    del lo_ref, hi_ref
    x = x_ref[:, :d].astype(BF16)
    w_lo = x_ref[:, d:d + 1]
    w_hi = x_ref[:, d + 1:d + 2]

    def expert(w1, w3, w2):
        a = jnp.dot(x, w1[0], preferred_element_type=F32)
        b = jnp.dot(x, w3[0], preferred_element_type=F32)
        return jnp.dot((_silu(a) * b).astype(BF16), w2[0], preferred_element_type=F32)

    y_ref[...] = w_lo * expert(w1l, w3l, w2l) + w_hi * expert(w1h, w3h, w2h)


def _moe(xs, lo, hi, w1, w3, w2):
    n_pad, wd = xs.shape
    d = wd - WEXT
    hdim = w1.shape[-1]
    n_tiles = n_pad // MOE_TILE
    wl = lambda i, lo, hi: (lo[i], 0, 0)
    wh = lambda i, lo, hi: (hi[i], 0, 0)
    grid_spec = pltpu.PrefetchScalarGridSpec(
        num_scalar_prefetch=2,
        grid=(n_tiles,),
        in_specs=[pl.BlockSpec((MOE_TILE, wd), lambda i, lo, hi: (i, 0)),
                  pl.BlockSpec((1, d, hdim), wl), pl.BlockSpec((1, d, hdim), wl), pl.BlockSpec((1, hdim, d), wl),
                  pl.BlockSpec((1, d, hdim), wh), pl.BlockSpec((1, d, hdim), wh), pl.BlockSpec((1, hdim, d), wh)],
        out_specs=pl.BlockSpec((MOE_TILE, d), lambda i, lo, hi: (i, 0)),
    )
    return pl.pallas_call(
        functools.partial(_moe_kernel, d),
        grid_spec=grid_spec,
        out_shape=jax.ShapeDtypeStruct((n_pad, d), F32),
        compiler_params=_cparams(("arbitrary",)),
        name="moe_experts",
    )(lo, hi, xs, w1, w3, w2, w1, w3, w2)


def _gather_residual_kernel(n_steps, pos_ref, posn_ref, ys_ref, xm_ref, gate_ref, o_ref, buf, sem):
    step = pl.program_id(0) * pl.num_programs(1) + pl.program_id(1)
    slot = step % 2

    def row_copy(p, j, s):
        return pltpu.make_async_copy(ys_ref.at[pl.ds(p, 1)], buf.at[s, pl.ds(j, 1)], sem.at[s])

    def issue_all(idx_ref, s):
        def issue(j, carry):
            row_copy(idx_ref[0, 0, j], j, s).start()
            return carry

        lax.fori_loop(0, TOK, issue, 0, unroll=DMA_UNROLL)

    @pl.when(step == 0)
    def _():
        issue_all(pos_ref, slot)

    @pl.when(step + 1 < n_steps)
    def _():
        issue_all(posn_ref, 1 - slot)

    def drain(j, carry):
        row_copy(0, 0, slot).wait()
        return carry

    lax.fori_loop(0, TOK, drain, 0, unroll=DMA_UNROLL)
    o_ref[0] = xm_ref[0] + gate_ref[0] * buf[slot]


def _gather_residual(ys, pos, xmid, gate_spec, mod_table):
    batch, t, d = xmid.shape
    nch = t // TOK
    n_steps = batch * nch
    return pl.pallas_call(
        functools.partial(_gather_residual_kernel, n_steps),
        grid=(batch, nch),
        in_specs=[pl.BlockSpec((1, 1, TOK), lambda b, c: (b * nch + c, 0, 0), memory_space=pltpu.SMEM),
                  pl.BlockSpec((1, 1, TOK), lambda b, c: (jnp.minimum(b * nch + c + 1, n_steps - 1), 0, 0),
                               memory_space=pltpu.SMEM),
                  pl.BlockSpec(memory_space=pl.ANY),
                  pl.BlockSpec((1, TOK, d), lambda b, c: (b, c, 0)),
                  gate_spec],
        out_specs=pl.BlockSpec((1, TOK, d), lambda b, c: (b, c, 0)),
        out_shape=jax.ShapeDtypeStruct((batch, t, d), F32),
        scratch_shapes=[pltpu.VMEM((2, TOK, d), F32), pltpu.SemaphoreType.DMA((2,))],
        compiler_params=_cparams(("arbitrary", "arbitrary")),
        name="gather_residual",
    )(pos.reshape(n_steps, 1, TOK), pos.reshape(n_steps, 1, TOK), ys, xmid, mod_table)


def _moe_block(h2ext, cr, cnt, xmid, gate_spec, mod_table, w1, w3, w2):
    n = h2ext.shape[0]
    n_tiles = n // MOE_TILE + N_CLASSES
    pos, lo, hi = _plan(cr, cnt, n_tiles)
    xs = _scatter_rows(h2ext, pos, n_tiles * MOE_TILE)
    ys = _moe(xs, lo, hi, w1, w3, w2)
    return _gather_residual(ys, pos, xmid, gate_spec, mod_table)


def _qkv_kernel(ncc, x_ref, g_ref, sh_ref, sc_ref, w_ref, qn_ref, kn_ref, cos_ref, sin_ref,
                q_ref, k_ref, v_ref):
    c = pl.program_id(1)
    latent = c >= ncc
    h = _rms(x_ref[0], g_ref[...]) * (1.0 + sc_ref[0]) + sh_ref[0]
    hb = h.astype(BF16)
    nq = ATT_HEADS * HEAD_DIM
    nk = KV_HEADS * HEAD_DIM
    cos = cos_ref[...]
    sin = sin_ref[...]

    def head(z, gain):
        z = _rms(z, gain)
        rot = z * cos + pltpu.roll(z, HEAD_DIM // 2, 1) * sin
        return jnp.where(latent, rot, z)

    kv = jnp.dot(hb, w_ref[:, nq:], preferred_element_type=F32)
    k_ref[0] = jnp.concatenate(
        [head(kv[:, i * HEAD_DIM:(i + 1) * HEAD_DIM], kn_ref[...]) for i in range(KV_HEADS)], axis=1).astype(BF16)
    ones = jnp.ones((TOK, HEAD_DIM), BF16)
    v_ref[0] = jnp.concatenate(
        [blk for i in range(KV_HEADS)
         for blk in (kv[:, nk + i * HEAD_DIM:nk + (i + 1) * HEAD_DIM].astype(BF16), ones)], axis=1)

    @pl.when(latent)
    def _():
        qq = jnp.dot(hb, w_ref[:, :nq], preferred_element_type=F32)
        q_ref[0] = jnp.concatenate(
            [head(qq[:, i * HEAD_DIM:(i + 1) * HEAD_DIM], qn_ref[...]) * (HEAD_DIM ** -0.5)
             for i in range(ATT_HEADS)], axis=1).astype(BF16)


def _qkv(ncc, x1, gain, mods, w_qkv, qn, kn, cos, sin):
    batch, t, d = x1.shape
    nch = t // TOK
    seq = t - ncc * TOK
    nq = ATT_HEADS * HEAD_DIM
    nk = KV_HEADS * HEAD_DIM
    const = lambda shape: pl.BlockSpec(shape, lambda b, c: (0,) * len(shape))
    lat = lambda b, c: (b, jnp.maximum(c - ncc, 0), 0)
    return pl.pallas_call(
        functools.partial(_qkv_kernel, ncc),
        grid=(batch, nch),
        in_specs=[pl.BlockSpec((1, TOK, d), lambda b, c: (b, c, 0)),
                  const((1, d)), mods.spec(1, 0), mods.spec(1, 1),
                  const((d, nq + 2 * nk)), const((1, HEAD_DIM)), const((1, HEAD_DIM)),
                  pl.BlockSpec((TOK, HEAD_DIM), lambda b, c: (jnp.maximum(c - ncc, 0), 0)),
                  pl.BlockSpec((TOK, HEAD_DIM), lambda b, c: (jnp.maximum(c - ncc, 0), 0))],
        out_specs=[pl.BlockSpec((1, TOK, nq), lat),
                   pl.BlockSpec((1, TOK, nk), lambda b, c: (b, c, 0)),
                   pl.BlockSpec((1, TOK, 2 * nk), lambda b, c: (b, c, 0))],
        out_shape=[jax.ShapeDtypeStruct((batch, seq, nq), BF16),
                   jax.ShapeDtypeStruct((batch, t, nk), BF16),
                   jax.ShapeDtypeStruct((batch, t, 2 * nk), BF16)],
        compiler_params=_cparams(("arbitrary", "arbitrary")),
        name="qkv",
    )(x1, gain, mods.table, mods.table, w_qkv, qn, kn, cos, sin)


def _attn_kernel(n_kb, q_ref, k_ref, v_ref, o_ref):
    q = q_ref[0]
    qs = jnp.concatenate([q[:, g * HEAD_DIM:(g + 1) * HEAD_DIM] for g in range(Q_PER_KV)], axis=0)
    kb = k_ref.shape[1] // n_kb
    m = acc = None
    for j in range(n_kb):
        s = lax.dot_general(qs, k_ref[0, j * kb:(j + 1) * kb, :], (((1,), (1,)), ((), ())),
                            preferred_element_type=F32)
        m_blk = jnp.max(s, axis=-1, keepdims=True)
        m_new = m_blk if m is None else jnp.maximum(m, m_blk)
        p = jnp.exp((s - m_new).astype(BF16))
        pv = jnp.dot(p, v_ref[0, j * kb:(j + 1) * kb, :], preferred_element_type=F32)
        acc = pv if acc is None else jnp.exp(m - m_new) * acc + pv
        m = m_new
    o = acc[:, :HEAD_DIM] / acc[:, HEAD_DIM:HEAD_DIM + 1]
    o_ref[0] = jnp.concatenate([o[g * ATT_Q:(g + 1) * ATT_Q] for g in range(Q_PER_KV)], axis=1).astype(BF16)


def _attention(q, k, v):
    batch, seq, nq = q.shape
    t = k.shape[1]
    gw = Q_PER_KV * HEAD_DIM
    n_kb = 2 if t % (2 * LANES) == 0 else 1
    return pl.pallas_call(
        functools.partial(_attn_kernel, n_kb),
        grid=(batch, KV_HEADS, seq // ATT_Q),
        in_specs=[pl.BlockSpec((1, ATT_Q, gw), lambda b, h, i: (b, i, h)),
                  pl.BlockSpec((1, t, HEAD_DIM), lambda b, h, i: (b, 0, h)),
                  pl.BlockSpec((1, t, 2 * HEAD_DIM), lambda b, h, i: (b, 0, h))],
        out_specs=pl.BlockSpec((1, ATT_Q, gw), lambda b, h, i: (b, i, h)),
        out_shape=jax.ShapeDtypeStruct((batch, seq, nq), BF16),
        compiler_params=_cparams(("arbitrary", "arbitrary", "arbitrary")),
        name="attention",
    )(q, k, v)


def _block_diag(w):
    nb, bs, _ = w.shape
    eye = jnp.eye(nb, dtype=w.dtype)
    return (eye[:, None, :, None] * w[:, :, None, :]).reshape(nb * bs, nb * bs)


def _router_weights(wg, bg, we, be):
    d = wg.shape[0]
    n = N_GROUPS + N_EXPERTS
    wr = jnp.zeros((ROUTER_ROWS, d), F32).at[:N_GROUPS].set(wg.T).at[N_GROUPS:n].set(we.T)
    br = jnp.zeros((ROUTER_ROWS, 1), F32).at[:N_GROUPS, 0].set(bg).at[N_GROUPS:n, 0].set(be)
    w_hi = wr.astype(BF16)
    w_lo = (wr - w_hi.astype(F32)).astype(BF16)
    return jnp.concatenate([w_hi, w_lo], axis=0), br


def _rope_tables(seq):
    rows = seq // GRID_W
    row = np.repeat(np.arange(rows, dtype=np.float32), GRID_W)
    col = np.tile(np.arange(GRID_W, dtype=np.float32), rows)
    ppa = HEAD_DIM // 4
    freqs = (ROPE_THETA ** (-np.arange(ppa, dtype=np.float32) / ppa)).astype(np.float32)
    ang = np.concatenate([row[:, None] * freqs, col[:, None] * freqs], axis=-1)
    cos, sin = np.cos(ang), np.sin(ang)
    return (jnp.asarray(np.concatenate([cos, cos], axis=-1), F32),
            jnp.asarray(np.concatenate([-sin, sin], axis=-1), F32))


_HALF_SPLIT = np.concatenate([np.arange(0, HEAD_DIM, 2), np.arange(1, HEAD_DIM, 2)])


def kernel(x, c, ctx, c_ctx, norm1, norm2, w_ada, b_ada, ev_w_in, ev_conv_w, ev_conv_b, ev_lru_wa, ev_lru_ba, ev_lru_wi, ev_lru_bi, ev_lru_lam, ev_gla_wg, ev_gla_bg, ev_gla_norm, ev_w_out, od_w_qkv, od_q_norm, od_k_norm, od_w_o, moe_wg, moe_bg, moe_we, moe_be, moe_w1, moe_w3, moe_w2):
    batch, seq, d = x.shape
    ctx_len = ctx.shape[1]
    assert seq % TOK == 0 and ctx_len % TOK == 0 and d % GLA_VAL == 0 and seq % GRID_W == 0
    ncc = ctx_len // TOK
    lw = d // 2

    rows = -(-(batch + 1) // SUBLANES) * SUBLANES
    cv = jnp.zeros((rows, d), F32).at[:batch].set(c).at[batch].set(c_ctx)
    table = _adaln(cv, w_ada, b_ada)
    mods = _Mods(table.reshape(table.shape[0] * rows * 6, 1, d), rows, batch, ncc)

    w_in = ev_w_in[0]
    nm = 2 * lw + 2 * GLA_KEY + 2 * GLA_VAL
    w_main = w_in[:, :nm].astype(BF16)
    w_lr = jnp.zeros((d, LANES), F32).at[:, :2 * GLA_RANK].set(w_in[:, nm:]).astype(BF16)
    pm, lr = _inproj(x, ctx, norm1[0][None], mods, w_main, w_lr)

    mixed = []
    for dr in range(2):
        w_gate = jnp.concatenate([_block_diag(ev_lru_wa[0, dr]), _block_diag(ev_lru_wi[0, dr])], axis=1).astype(BF16)
        b_gate = jnp.concatenate([ev_lru_ba[0, dr], ev_lru_bi[0, dr]])[None]
        wg_pad = jnp.zeros((LANES, GLA_KEY), F32).at[dr * GLA_RANK:(dr + 1) * GLA_RANK].set(ev_gla_wg[0, dr]).astype(BF16)
        mixed.append(_seqmix(dr == 1, ncc, pm, lr, ev_conv_w[0], ev_conv_b[0][None], w_gate, b_gate,
                             ev_lru_lam[0, dr][None], wg_pad, ev_gla_bg[0, dr][None], lw))
    (lru_f, gla_f), (lru_r, gla_r) = mixed

    wr0, br0 = _router_weights(moe_wg[0], moe_bg[0], moe_we[0], moe_be[0])
    xmid0, h2e0, cr0, cnt0 = _post0(ncc, lru_f, lru_r, gla_f, gla_r, pm, x, ctx, mods, norm2[0][None],
                                    ev_gla_norm[0][None], ev_w_out[0].astype(BF16), wr0, br0, lw)
    x1 = _moe_block(h2e0, cr0, cnt0, xmid0, mods.spec(0, 5), mods.table,
                    moe_w1[0].astype(BF16), moe_w3[0].astype(BF16), moe_w2[0].astype(BF16))

    nq = ATT_HEADS * HEAD_DIM
    nk = KV_HEADS * HEAD_DIM
    perm = np.concatenate([h * HEAD_DIM + _HALF_SPLIT for h in range(ATT_HEADS + KV_HEADS)]
                          + [np.arange(nq + nk, nq + 2 * nk)])
    w_qkv = od_w_qkv[0][:, perm].astype(BF16)
    cos, sin = _rope_tables(seq)
    q, k, v = _qkv(ncc, x1, norm1[1][None], mods, w_qkv, od_q_norm[0][_HALF_SPLIT][None],
                   od_k_norm[0][_HALF_SPLIT][None], cos, sin)
    att = _attention(q, k, v)
    wr1, br1 = _router_weights(moe_wg[1], moe_bg[1], moe_we[1], moe_be[1])
    xmid1, h2e1, cr1, cnt1 = _post1(ncc, att, x1, mods, norm2[1][None], od_w_o[0].astype(BF16), wr1, br1)
    return _moe_block(h2e1, cr1, cnt1, xmid1, mods.spec(1, 5, True), mods.table,
                      moe_w1[1].astype(BF16), moe_w3[1].astype(BF16), moe_w2[1].astype(BF16))
```

```python
import functools

import numpy as np
import jax
import jax.numpy as jnp
from jax import lax
from jax.experimental import pallas as pl
from jax.experimental.pallas import tpu as pltpu

F32 = jnp.float32
BF16 = jnp.bfloat16
I32 = jnp.int32
HIGHEST = lax.Precision.HIGHEST

EPS = 1e-6
GRID_W = 64
LRU_BLOCKS = 8
LRU_C = 8.0
GLA_HEADS = 4
GLA_DK = 64
GLA_DV = 128
GLA_KEY = GLA_HEADS * GLA_DK
GLA_VAL = GLA_HEADS * GLA_DV
GLA_RANK = 16
GLA_TAU = 16.0
GLA_CHUNK = 64
_CHUNK_SHIFT = GLA_CHUNK.bit_length() - 1
_DK_SHIFT = GLA_DK.bit_length() - 1
ATT_HEADS = 8
KV_HEADS = 2
Q_PER_KV = ATT_HEADS // KV_HEADS
HEAD_DIM = 128
ROPE_THETA = 10000.0
N_GROUPS = 4
EXPERTS_PER_GROUP = 4
N_EXPERTS = N_GROUPS * EXPERTS_PER_GROUP
PAIRS_PER_GROUP = 6
N_CLASSES = N_GROUPS * PAIRS_PER_GROUP

LANES = 128
SUBLANES = 8
TOK = 256
MOE_TILE = 256
ATT_Q = 256
PERM_ROWS = 512
DMA_UNROLL = 8
WEXT = LANES
ROUTER_ROWS = 32
VMEM_LIMIT = 56 * 1024 * 1024


def _cparams(sem):
    return pltpu.CompilerParams(dimension_semantics=sem, vmem_limit_bytes=VMEM_LIMIT)


def _rms(x, g):
    return x * lax.rsqrt(jnp.mean(x * x, axis=-1, keepdims=True) + EPS) * g


def _sigmoid(x):
    return 1.0 / (1.0 + jnp.exp(-x))


def _silu(x):
    return x * _sigmoid(x)


def _gelu_tanh(x):
    return 0.5 * x * (1.0 + jnp.tanh(np.sqrt(2.0 / np.pi).astype(np.float32) * (x + 0.044715 * (x * x * x))))


def _softplus(x):
    return jnp.maximum(x, 0.0) + jnp.log(1.0 + jnp.exp(-jnp.abs(x)))


def _log_sigmoid(x):
    return -_softplus(-x)


def _adaln_kernel(cv_ref, w_ref, b_ref, o_ref):
    s = _silu(cv_ref[...])
    o_ref[0] = jnp.dot(s, w_ref[0], precision=HIGHEST, preferred_element_type=F32) + b_ref[0]


def _adaln(cv, w_ada, b_ada):
    depth, d, n6 = w_ada.shape
    rows = cv.shape[0]
    tn = 6 * d // 4
    return pl.pallas_call(
        _adaln_kernel,
        grid=(depth, n6 // tn),
        in_specs=[pl.BlockSpec((rows, d), lambda l, j: (0, 0)),
                  pl.BlockSpec((1, d, tn), lambda l, j: (l, 0, j)),
                  pl.BlockSpec((1, 1, tn), lambda l, j: (l, 0, j))],
        out_specs=pl.BlockSpec((1, rows, tn), lambda l, j: (l, 0, j)),
        out_shape=jax.ShapeDtypeStruct((depth, rows, n6), F32),
        compiler_params=_cparams(("arbitrary", "arbitrary")),
        name="adaln",
    )(cv, w_ada, b_ada.reshape(depth, 1, n6))


class _Mods:
    def __init__(self, table, rows, batch, ncc):
        self.table, self.rows, self.batch, self.ncc = table, rows, batch, ncc
        self.d = table.shape[-1]

    def spec(self, layer, j, latent_only=False):
        rows, batch, ncc = self.rows, self.batch, self.ncc

        def imap(b, c):
            r = b if latent_only else jnp.where(c < ncc, batch, b)
            return ((layer * rows + r) * 6 + j, 0, 0)

        return pl.BlockSpec((1, 1, self.d), imap)


def _inproj_kernel(ncc, x_ref, c_ref, g_ref, sh_ref, sc_ref, w_ref, wlr_ref, pm_ref, lr_ref):
    c = pl.program_id(1)
    xf = jnp.where(c < ncc, c_ref[0], x_ref[0])
    h = _rms(xf, g_ref[...]) * (1.0 + sc_ref[0]) + sh_ref[0]
    hb = h.astype(BF16)
    pm_ref[0] = jnp.dot(hb, w_ref[...], preferred_element_type=F32).astype(BF16)
    lr_ref[0] = jnp.dot(hb, wlr_ref[...], preferred_element_type=F32).astype(BF16)


def _inproj(x, ctx, gain, mods, w_main, w_lr):
    batch, seq, d = x.shape
    ncc = ctx.shape[1] // TOK
    nch = ncc + seq // TOK
    nm = w_main.shape[1]
    return pl.pallas_call(
        functools.partial(_inproj_kernel, ncc),
        grid=(batch, nch),
        in_specs=[pl.BlockSpec((1, TOK, d), lambda b, c: (b, jnp.maximum(c - ncc, 0), 0)),
                  pl.BlockSpec((1, TOK, d), lambda b, c: (b, jnp.minimum(c, ncc - 1), 0)),
                  pl.BlockSpec((1, d), lambda b, c: (0, 0)),
                  mods.spec(0, 0), mods.spec(0, 1),
                  pl.BlockSpec((d, nm), lambda b, c: (0, 0)),
                  pl.BlockSpec((d, LANES), lambda b, c: (0, 0))],
        out_specs=[pl.BlockSpec((1, TOK, nm), lambda b, c: (b, c, 0)),
                   pl.BlockSpec((1, TOK, LANES), lambda b, c: (b, c, 0))],
        out_shape=[jax.ShapeDtypeStruct((batch, nch * TOK, nm), BF16),
                   jax.ShapeDtypeStruct((batch, nch * TOK, LANES), BF16)],
        compiler_params=_cparams(("arbitrary", "arbitrary")),
        name="inproj",
    )(x, ctx, gain, mods.table, mods.table, w_main, w_lr)


def _lru_scan(a, b, h0, rev):
    sub = lax.broadcasted_iota(I32, a.shape, 0) & (SUBLANES - 1)
    for dist in (1, 2, 4):
        shift = TOK - dist if rev else dist
        a_s = pltpu.roll(a, shift, 0)
        b_s = pltpu.roll(b, shift, 0)
        m = (sub < SUBLANES - dist) if rev else (sub >= dist)
        b = jnp.where(m, a * b_s + b, b)
        a = jnp.where(m, a * a_s, a)
    n_groups = TOK // SUBLANES
    order = range(n_groups - 1, -1, -1) if rev else range(n_groups)
    outs = [None] * n_groups
    h = h0
    for r in order:
        hr = a[r * SUBLANES:(r + 1) * SUBLANES] * h + b[r * SUBLANES:(r + 1) * SUBLANES]
        outs[r] = hr
        h = hr[0:1] if rev else hr[SUBLANES - 1:SUBLANES]
    return jnp.concatenate(outs, axis=0), h


def _seqmix_kernel(rev, ncc, nch, xa_ref, xp_ref, xn_ref, q_ref, k_ref, v_ref, lr_ref,
                   cw_ref, cb_ref, wgt_ref, bgt_ref, lam_ref, wg_ref, bg_ref, sel_ref,
                   lru_ref, gla_ref, h_scr, s_scr):
    s = pl.program_id(1)
    nl = nch - ncc
    if rev:
        ch = jnp.where(s < ncc, ncc - 1 - s, ncc + (nl - 1) - (s - ncc))
    else:
        ch = s

    @pl.when(s == 0)
    def _():
        h_scr[...] = jnp.zeros_like(h_scr)
        s_scr[...] = jnp.zeros_like(s_scr)

    lw = xa_ref.shape[-1]
    xa = xa_ref[0].astype(F32)
    has_prev = jnp.logical_and(ch != 0, ch != ncc)
    has_next = jnp.logical_and(ch != ncc - 1, ch != nch - 1)
    prev = jnp.where(has_prev, xp_ref[0].astype(F32), 0.0)
    nxt = jnp.where(has_next, xn_ref[0].astype(F32), 0.0)
    p2, p1, n0 = prev[14:15], prev[15:16], nxt[0:1]
    row = lax.broadcasted_iota(I32, xa.shape, 0)
    x_m1 = jnp.where(row == 0, p1, pltpu.roll(xa, 1, 0))
    x_m2 = jnp.where(row == 0, p2, jnp.where(row == 1, p1, pltpu.roll(xa, 2, 0)))
    x_p1 = jnp.where(row == TOK - 1, n0, pltpu.roll(xa, TOK - 1, 0))
    cw = cw_ref[...]
    u = cw[0:1] * x_m2 + cw[1:2] * x_m1 + cw[2:3] * xa + cw[3:4] * x_p1 + cb_ref[...]

    gates = jnp.dot(u.astype(BF16), wgt_ref[...], preferred_element_type=F32) + bgt_ref[...]
    r_gate = _sigmoid(gates[:, :lw])
    i_gate = _sigmoid(gates[:, lw:])
    log_a = (-LRU_C) * r_gate * _softplus(-lam_ref[...])
    a = jnp.exp(log_a)
    bb = jnp.sqrt(1.0 - a * a) * (i_gate * u)
    hs, h_last = _lru_scan(a, bb, h_scr[0:1], rev)
    h_scr[0:1] = h_last
    lru_ref[0] = hs.astype(BF16)

    lg = _log_sigmoid(jnp.dot(lr_ref[0], wg_ref[...], preferred_element_type=F32) + bg_ref[...]) * (1.0 / GLA_TAU)
    lg_hi = lg.astype(BF16)
    lg_lo = (lg - lg_hi.astype(F32)).astype(BF16)
    cum = jnp.dot(sel_ref[...], jnp.concatenate([lg_hi, lg_lo], axis=1), preferred_element_type=F32)
    g = cum[:TOK, :GLA_KEY] + cum[:TOK, GLA_KEY:]
    tot = cum[TOK:, :GLA_KEY] + cum[TOK:, GLA_KEY:]
    q = q_ref[0].astype(F32)
    k = k_ref[0].astype(F32)
    dec_all = jnp.exp(tot)
    k_neg = k * jnp.exp(-g)
    qg = (q * jnp.exp(g) * (GLA_DK ** -0.5)).astype(BF16)
    kg = k_neg.astype(BF16)
    kd = (k_neg * dec_all).astype(BF16)
    v = v_ref[0]

    lane = lax.broadcasted_iota(I32, (1, GLA_KEY), 1) >> _DK_SHIFT
    head_masks = [(lane == h) for h in range(GLA_HEADS)]
    n_stack = GLA_HEADS * GLA_CHUNK
    ai = lax.broadcasted_iota(I32, (n_stack, GLA_CHUNK), 0) & (GLA_CHUNK - 1)
    aj = lax.broadcasted_iota(I32, (n_stack, GLA_CHUNK), 1)
    causal = (aj >= ai) if rev else (aj <= ai)

    n_sub = TOK // GLA_CHUNK
    order = range(n_sub - 1, -1, -1) if rev else range(n_sub)
    outs = [None] * n_sub
    st = s_scr[...]
    for n in order:
        sl = slice(n * GLA_CHUNK, (n + 1) * GLA_CHUNK)
        qn = qg[sl]
        qs = jnp.concatenate([jnp.where(head_masks[h], qn, jnp.zeros_like(qn)) for h in range(GLA_HEADS)], axis=0)
        att = lax.dot_general(qs, kg[sl], (((1,), (1,)), ((), ())), preferred_element_type=F32)
        att = jnp.where(causal, att, 0.0).astype(BF16)
        o_all = jnp.dot(att, v[sl], preferred_element_type=F32)
        o_int = lax.dot_general(qs, st.astype(BF16), (((1,), (1,)), ((), ())),
                                preferred_element_type=F32)
        outs[n] = jnp.concatenate(
            [o_all[h * GLA_CHUNK:(h + 1) * GLA_CHUNK, h * GLA_DV:(h + 1) * GLA_DV]
             + o_int[h * GLA_CHUNK:(h + 1) * GLA_CHUNK] for h in range(GLA_HEADS)], axis=1)
        ds_t = lax.dot_general(v[sl], kd[sl], (((0,), (0,)), ((), ())), preferred_element_type=F32)
        ds = jnp.zeros_like(st)
        for h in range(GLA_HEADS):
            ds = ds + jnp.where(head_masks[h], ds_t[h * GLA_DV:(h + 1) * GLA_DV], 0.0)
        st = dec_all[n * GLA_CHUNK:n * GLA_CHUNK + 1] * st + ds
    s_scr[...] = st
    gla_ref[0] = jnp.concatenate(outs, axis=0).astype(BF16)


def _cumsum_selector(rev):
    r = np.arange(TOK)[:, None]
    c = np.arange(TOK)[None, :]
    same = (r // GLA_CHUNK) == (c // GLA_CHUNK)
    tri = same & ((c >= r) if rev else (c <= r))
    return jnp.asarray(np.concatenate([tri, same], axis=0), BF16)


def _seqmix(rev, ncc, pm, lr, conv_w, conv_b, w_gate, b_gate, lam, wg_pad, bg, lw):
    batch, t, _ = pm.shape
    nch = t // TOK
    nl = nch - ncc
    hb = TOK // 16

    def chunk(s):
        if rev:
            return jnp.where(s < ncc, ncc - 1 - s, ncc + (nl - 1) - (s - ncc))
        return s

    qcol = 2 * lw // GLA_KEY
    vcol = (2 * lw + 2 * GLA_KEY) // GLA_VAL
    const = lambda shape: pl.BlockSpec(shape, lambda b, s: (0,) * len(shape))
    return pl.pallas_call(
        functools.partial(_seqmix_kernel, rev, ncc, nch),
        grid=(batch, nch),
        in_specs=[pl.BlockSpec((1, TOK, lw), lambda b, s: (b, chunk(s), 0)),
                  pl.BlockSpec((1, 16, lw), lambda b, s: (b, jnp.maximum(chunk(s) * hb - 1, 0), 0)),
                  pl.BlockSpec((1, 16, lw), lambda b, s: (b, jnp.minimum((chunk(s) + 1) * hb, nch * hb - 1), 0)),
                  pl.BlockSpec((1, TOK, GLA_KEY), lambda b, s: (b, chunk(s), qcol)),
                  pl.BlockSpec((1, TOK, GLA_KEY), lambda b, s: (b, chunk(s), qcol + 1)),
                  pl.BlockSpec((1, TOK, GLA_VAL), lambda b, s: (b, chunk(s), vcol)),
                  pl.BlockSpec((1, TOK, LANES), lambda b, s: (b, chunk(s), 0)),
                  const((4, lw)), const((1, lw)), const((lw, 2 * lw)), const((1, 2 * lw)), const((1, lw)),
                  const((LANES, GLA_KEY)), const((1, GLA_KEY)), const((2 * TOK, TOK))],
        out_specs=[pl.BlockSpec((1, TOK, lw), lambda b, s: (b, chunk(s), 0)),
                   pl.BlockSpec((1, TOK, GLA_VAL), lambda b, s: (b, chunk(s), 0))],
        out_shape=[jax.ShapeDtypeStruct((batch, t, lw), BF16),
                   jax.ShapeDtypeStruct((batch, t, GLA_VAL), BF16)],
        scratch_shapes=[pltpu.VMEM((SUBLANES, lw), F32), pltpu.VMEM((GLA_DV, GLA_KEY), F32)],
        compiler_params=_cparams(("arbitrary", "arbitrary")),
        name="seqmix_rev" if rev else "seqmix_fwd",
    )(pm, pm, pm, pm, pm, pm, lr, conv_w, conv_b, w_gate, b_gate, lam, wg_pad, bg, _cumsum_selector(rev))


def _route_epilogue(first, xm, n2_ref, sh2_ref, sc2_ref, wr_ref, br_ref,
                    xmid_ref, h2_ref, cr_ref, cnt_ref, carry):
    d = xm.shape[-1]
    xmid_ref[0] = xm
    h2 = _rms(xm, n2_ref[...]) * (1.0 + sc2_ref[0]) + sh2_ref[0]
    h_hi = h2.astype(BF16)
    h_lo = (h2 - h_hi.astype(F32)).astype(BF16)
    nt = (((1,), (1,)), ((), ()))
    o1 = lax.dot_general(wr_ref[...], h_hi, nt, preferred_element_type=F32)
    o2 = lax.dot_general(wr_ref[:ROUTER_ROWS], h_lo, nt, preferred_element_type=F32)
    logits = o1[:ROUTER_ROWS] + o1[ROUTER_ROWS:] + o2 + br_ref[...]

    def col(j):
        return logits[j:j + 1]

    lgs = [col(g) for g in range(N_GROUPS)]
    gmax = functools.reduce(jnp.maximum, lgs)
    gi = jnp.where(lgs[0] == gmax, 0, jnp.where(lgs[1] == gmax, 1, jnp.where(lgs[2] == gmax, 2, 3)))
    w_group = 1.0 / functools.reduce(lambda p, q: p + q, [jnp.exp(l - gmax) for l in lgs])
    es = []
    for j in range(EXPERTS_PER_GROUP):
        acc = jnp.zeros_like(gmax)
        for g in range(N_GROUPS):
            acc = acc + jnp.where(gi == g, col(N_GROUPS + g * EXPERTS_PER_GROUP + j), 0.0)
        es.append(acc)
    m1 = functools.reduce(jnp.maximum, es)
    i1 = jnp.where(es[0] == m1, 0, jnp.where(es[1] == m1, 1, jnp.where(es[2] == m1, 2, 3)))
    rest = [jnp.where(i1 == j, -jnp.inf, es[j]) for j in range(EXPERTS_PER_GROUP)]
    m2 = functools.reduce(jnp.maximum, rest)
    i2 = jnp.where(rest[0] == m2, 0, jnp.where(rest[1] == m2, 1, jnp.where(rest[2] == m2, 2, 3)))
    e2 = jnp.exp(m2 - m1)
    w1 = w_group / (1.0 + e2)
    w2 = w_group * e2 / (1.0 + e2)
    first_lo = i1 < i2
    lo = jnp.where(first_lo, i1, i2)
    hi = jnp.where(first_lo, i2, i1)
    w_lo = jnp.where(first_lo, w1, w2)
    w_hi = jnp.where(first_lo, w2, w1)
    pidx = jnp.where(lo == 0, hi - 1, jnp.where(lo == 1, hi + 1, 5))
    cls = gi * PAIRS_PER_GROUP + pidx

    def terms(w):
        t1 = w.astype(BF16).astype(F32)
        t2 = (w - t1).astype(BF16).astype(F32)
        t3 = (w - t1 - t2).astype(BF16).astype(F32)
        return [t1, t2, t3]

    sub = lax.broadcasted_iota(I32, (2 * SUBLANES, TOK), 0)
    stacked = jnp.zeros((2 * SUBLANES, TOK), F32)
    for r, term in enumerate(terms(w_lo) + terms(w_hi)):
        stacked = jnp.where(sub == r, term, stacked)
    er = lax.broadcasted_iota(I32, (2 * SUBLANES, LANES), 0)
    el = lax.broadcasted_iota(I32, (2 * SUBLANES, LANES), 1)
    place = jnp.logical_or(jnp.logical_and(er < 3, el == 0),
                           jnp.logical_and(jnp.logical_and(er >= 3, er < 6), el == 1))
    w_cols = lax.dot_general(stacked.astype(BF16), place.astype(BF16), (((0,), (0,)), ((), ())),
                             preferred_element_type=F32)
    h2_ref[:, :d] = h2
    h2_ref[:, d:] = w_cols

    @pl.when(first)
    def _():
        carry[...] = jnp.zeros_like(carry)

    krow = lax.broadcasted_iota(I32, (ROUTER_ROWS, TOK), 0)
    onehot = krow == cls
    ri = lax.broadcasted_iota(I32, (TOK, TOK), 0)
    ci = lax.broadcasted_iota(I32, (TOK, TOK), 1)
    triu = (ri <= ci).astype(BF16)
    pref = jnp.dot(onehot.astype(BF16), triu, preferred_element_type=F32)
    base = carry[:, 0:1]
    rank = jnp.sum(jnp.where(onehot, pref - 1.0 + base, 0.0), axis=0, keepdims=True)
    new_carry = jnp.broadcast_to(base + jnp.sum(onehot.astype(F32), axis=1, keepdims=True), carry.shape)
    carry[...] = new_carry
    cnt_ref[...] = new_carry
    sub8 = lax.broadcasted_iota(I32, (SUBLANES, TOK), 0)
    cr_ref[0] = jnp.where(sub8 == 0, cls, jnp.where(sub8 == 1, rank.astype(I32), 0))


def _post0_kernel(ncc, l0_ref, l1_ref, g0_ref, g1_ref, ga_ref, r_ref, x_ref, c_ref,
                  gate_ref, sh2_ref, sc2_ref, n2_ref, gn_ref, wout_ref, wr_ref, br_ref,
                  xmid_ref, h2_ref, cr_ref, cnt_ref, carry):
    b, c = pl.program_id(0), pl.program_id(1)
    lru = l0_ref[0].astype(F32) + l1_ref[0].astype(F32)
    ya = lru * _gelu_tanh(ga_ref[0].astype(F32))
    gla = g0_ref[0].astype(F32) + g1_ref[0].astype(F32)
    parts = []
    for h in range(GLA_HEADS):
        parts.append(_rms(gla[:, h * GLA_DV:(h + 1) * GLA_DV], gn_ref[...]))
    yb = jnp.concatenate(parts, axis=1) * _silu(r_ref[0].astype(F32))
    ycat = jnp.concatenate([ya, yb], axis=1).astype(BF16)
    y = jnp.dot(ycat, wout_ref[...], preferred_element_type=F32)
    x0 = jnp.where(c < ncc, c_ref[0], x_ref[0])
    xm = x0 + gate_ref[0] * y
    _route_epilogue(jnp.logical_and(b == 0, c == 0), xm, n2_ref, sh2_ref, sc2_ref, wr_ref, br_ref,
                    xmid_ref, h2_ref, cr_ref, cnt_ref, carry)


def _post1_kernel(o_ref, xm0_ref, y0_ref, g2_ref, gate_ref, sh2_ref, sc2_ref, n2_ref, wo_ref, wr_ref, br_ref,
                  xmid_ref, h2_ref, cr_ref, cnt_ref, carry):
    b, c = pl.program_id(0), pl.program_id(1)
    y = jnp.dot(o_ref[0], wo_ref[...], preferred_element_type=F32)
    x1 = xm0_ref[0] + g2_ref[0] * y0_ref[...]
    xm = x1 + gate_ref[0] * y
    _route_epilogue(jnp.logical_and(b == 0, c == 0), xm, n2_ref, sh2_ref, sc2_ref, wr_ref, br_ref,
                    xmid_ref, h2_ref, cr_ref, cnt_ref, carry)


def _route_out(batch, nch, d):
    n = batch * nch * TOK
    specs = [pl.BlockSpec((1, TOK, d), lambda b, c: (b, c, 0)),
             pl.BlockSpec((TOK, d + WEXT), lambda b, c: (b * nch + c, 0)),
             pl.BlockSpec((1, SUBLANES, TOK), lambda b, c: (b * nch + c, 0, 0)),
             pl.BlockSpec((ROUTER_ROWS, LANES), lambda b, c: (0, 0))]
    shapes = [jax.ShapeDtypeStruct((batch, nch * TOK, d), F32),
              jax.ShapeDtypeStruct((n, d + WEXT), F32),
              jax.ShapeDtypeStruct((batch * nch, SUBLANES, TOK), I32),
              jax.ShapeDtypeStruct((ROUTER_ROWS, LANES), F32)]
    return specs, shapes


def _post0(ncc, lru_f, lru_r, gla_f, gla_r, pm, x, ctx, mods, n2, gn, w_out, wr, br, lw):
    batch, t, _ = pm.shape
    d = x.shape[-1]
    nch = t // TOK
    rcol = (2 * lw + 2 * GLA_KEY + GLA_VAL) // GLA_VAL
    tokspec = lambda w: pl.BlockSpec((1, TOK, w), lambda b, c: (b, c, 0))
    const = lambda shape: pl.BlockSpec(shape, lambda b, c: (0,) * len(shape))
    out_specs, out_shape = _route_out(batch, nch, d)
    return pl.pallas_call(
        functools.partial(_post0_kernel, ncc),
        grid=(batch, nch),
        in_specs=[tokspec(lw), tokspec(lw), tokspec(GLA_VAL), tokspec(GLA_VAL),
                  pl.BlockSpec((1, TOK, lw), lambda b, c: (b, c, 1)),
                  pl.BlockSpec((1, TOK, GLA_VAL), lambda b, c: (b, c, rcol)),
                  pl.BlockSpec((1, TOK, d), lambda b, c: (b, jnp.maximum(c - ncc, 0), 0)),
                  pl.BlockSpec((1, TOK, d), lambda b, c: (b, jnp.minimum(c, ncc - 1), 0)),
                  mods.spec(0, 2), mods.spec(0, 3), mods.spec(0, 4),
                  const((1, d)), const((1, GLA_DV)), const((lw + GLA_VAL, d)),
                  const((2 * ROUTER_ROWS, d)), const((ROUTER_ROWS, 1))],
        out_specs=out_specs, out_shape=out_shape,
        scratch_shapes=[pltpu.VMEM((ROUTER_ROWS, LANES), F32)],
        compiler_params=_cparams(("arbitrary", "arbitrary")),
        name="post0",
    )(lru_f, lru_r, gla_f, gla_r, pm, pm, x, ctx, mods.table, mods.table, mods.table, n2, gn, w_out, wr, br)


def _post1(ncc, att, xmid0, y0, mods, n2, w_o, wr, br):
    batch, seq, aw = att.shape
    d = xmid0.shape[-1]
    nl = seq // TOK
    nch = ncc + nl
    const = lambda shape: pl.BlockSpec(shape, lambda b, c: (0,) * len(shape))
    out_specs, out_shape = _route_out(batch, nl, d)
    return pl.pallas_call(
        _post1_kernel,
        grid=(batch, nl),
        in_specs=[pl.BlockSpec((1, TOK, aw), lambda b, c: (b, c, 0)),
                  pl.BlockSpec((1, TOK, d), lambda b, c: (b, c + ncc, 0)),
                  pl.BlockSpec((TOK, d), lambda b, c: (b * nch + c + ncc, 0)),
                  mods.spec(0, 5, True),
                  mods.spec(1, 2, True), mods.spec(1, 3, True), mods.spec(1, 4, True),
                  const((1, d)), const((aw, d)), const((2 * ROUTER_ROWS, d)), const((ROUTER_ROWS, 1))],
        out_specs=out_specs, out_shape=out_shape,
        scratch_shapes=[pltpu.VMEM((ROUTER_ROWS, LANES), F32)],
        compiler_params=_cparams(("arbitrary", "arbitrary")),
        name="post1",
    )(att, xmid0, y0, mods.table, mods.table, mods.table, mods.table, n2, w_o, wr, br)


_PAIR_LO = np.array([0, 0, 0, 1, 1, 2], np.int32)
_PAIR_HI = np.array([1, 2, 3, 2, 3, 3], np.int32)


def _plan(cr, cnt, n_tiles):
    cls, rank = cr[:, 0, :].reshape(-1), cr[:, 1, :].reshape(-1)
    counts = cnt[:N_CLASSES, 0].astype(I32)
    tiles = (counts + MOE_TILE - 1) // MOE_TILE
    ends = jnp.cumsum(tiles)
    offs = (ends - tiles) * MOE_TILE
    classes = jnp.arange(N_CLASSES, dtype=I32)
    pos = jnp.sum(jnp.where(cls[:, None] == classes[None, :], offs[None, :], 0), axis=1) + rank
    tile_ids = jnp.arange(n_tiles, dtype=I32)
    tile_cls = jnp.minimum(jnp.sum((tile_ids[:, None] >= ends[None, :]).astype(I32), axis=1), N_CLASSES - 1)
    grp = tile_cls // PAIRS_PER_GROUP
    pair = tile_cls % PAIRS_PER_GROUP
    lo = grp * EXPERTS_PER_GROUP + jnp.asarray(_PAIR_LO)[pair]
    hi = grp * EXPERTS_PER_GROUP + jnp.asarray(_PAIR_HI)[pair]
    n = cls.shape[0]
    n_pad = n_tiles * MOE_TILE
    token = jnp.full((n_pad,), -1, I32).at[pos].set(jnp.arange(n, dtype=I32))
    is_pad = token < 0
    spare = n + jnp.cumsum(is_pad.astype(I32)) - 1
    src = jnp.where(is_pad, 0, token)
    dst = jnp.where(is_pad, spare, token)
    return src.astype(I32), dst.astype(I32), lo.astype(I32), hi.astype(I32)


def _moe_kernel(d, n_tiles, lo_ref, hi_ref, src_ref, srcn_ref, dst_ref, x_hbm,
                w1l, w3l, w2l, w1h, w3h, w2h, y_hbm, xbuf, ybuf, wbuf, wbuf2, gsem, ssem):
    i = pl.program_id(0)
    slot = i % 2

    def gather_copy(p, grp, sub, s):
        return pltpu.make_async_copy(x_hbm.at[pl.ds(p, 1)], xbuf.at[s, grp, pl.ds(sub, 1)], gsem.at[s])

    def scatter_copy(grp, sub, p, s):
        return pltpu.make_async_copy(ybuf.at[s, grp, pl.ds(sub, 1)], y_hbm.at[pl.ds(p, 1)], ssem.at[s])

    def gather_all(idx_ref, s):
        def issue(grp, carry):
            for sub in range(SUBLANES):
                gather_copy(idx_ref[0, 0, grp * SUBLANES + sub], grp, sub, s).start(priority=sub % 2)
            return carry

        lax.fori_loop(0, MOE_TILE // SUBLANES, issue, 0)

    def wait_all(make_copy):
        def drain(grp, carry):
            for _ in range(SUBLANES):
                make_copy().wait()
            return carry

        lax.fori_loop(0, MOE_TILE // SUBLANES, drain, 0)

    @pl.when(i == 0)
    def _():
        gather_all(src_ref, slot)

    @pl.when(i + 1 < n_tiles)
    def _():
        gather_all(srcn_ref, 1 - slot)

    changed = jnp.logical_or(i == 0, jnp.logical_or(lo_ref[i] != lo_ref[jnp.maximum(i - 1, 0)],
                                                    hi_ref[i] != hi_ref[jnp.maximum(i - 1, 0)]))

    @pl.when(changed)
    def _():
        for k, w in enumerate((w1l, w3l, w1h, w3h)):
            wbuf[k] = w[0].astype(BF16)
        wbuf2[0] = w2l[0].astype(BF16)
        wbuf2[1] = w2h[0].astype(BF16)

    wait_all(lambda: gather_copy(0, 0, 0, slot))
    xt = xbuf[slot].reshape(MOE_TILE, d + WEXT)
    x = xt[:, :d].astype(BF16)

    def expert(k1, k3, k2):
        a = jnp.dot(x, wbuf[k1], preferred_element_type=F32)
        b = jnp.dot(x, wbuf[k3], preferred_element_type=F32)
        return jnp.dot((_silu(a) * b).astype(BF16), wbuf2[k2], preferred_element_type=F32)

    y = xt[:, d:d + 1] * expert(0, 1, 0) + xt[:, d + 1:d + 2] * expert(2, 3, 1)
    ybuf[slot] = y.reshape(MOE_TILE // SUBLANES, SUBLANES, d)

    @pl.when(i >= 1)
    def _():
        wait_all(lambda: scatter_copy(0, 0, 0, 1 - slot))

    def issue_out(grp, carry):
        for sub in range(SUBLANES):
            scatter_copy(grp, sub, dst_ref[0, 0, grp * SUBLANES + sub], slot).start(priority=sub % 2)
        return carry

    lax.fori_loop(0, MOE_TILE // SUBLANES, issue_out, 0)

    @pl.when(i == n_tiles - 1)
    def _():
        wait_all(lambda: scatter_copy(0, 0, 0, slot))


def _moe(h2ext, src, dst, lo, hi, w1, w3, w2):
    n, wd = h2ext.shape
    d = wd - WEXT
    hdim = w1.shape[-1]
    n_pad = src.shape[0]
    n_tiles = n_pad // MOE_TILE
    wl = lambda i, lo, hi: (lo[i], 0, 0)
    wh = lambda i, lo, hi: (hi[i], 0, 0)
    idx = lambda off: pl.BlockSpec((1, 1, MOE_TILE), lambda i, lo, hi: (jnp.minimum(i + off, n_tiles - 1), 0, 0),
                                   memory_space=pltpu.SMEM)
    grid_spec = pltpu.PrefetchScalarGridSpec(
        num_scalar_prefetch=2,
        grid=(n_tiles,),
        in_specs=[idx(0), idx(1), idx(0),
                  pl.BlockSpec(memory_space=pl.ANY),
                  pl.BlockSpec((1, d, hdim), wl), pl.BlockSpec((1, d, hdim), wl), pl.BlockSpec((1, hdim, d), wl),
                  pl.BlockSpec((1, d, hdim), wh), pl.BlockSpec((1, d, hdim), wh), pl.BlockSpec((1, hdim, d), wh)],
        out_specs=pl.BlockSpec(memory_space=pl.ANY),
        scratch_shapes=[pltpu.VMEM((2, MOE_TILE // SUBLANES, SUBLANES, wd), F32),
                        pltpu.VMEM((2, MOE_TILE // SUBLANES, SUBLANES, d), F32),
                        pltpu.VMEM((4, d, hdim), BF16), pltpu.VMEM((2, hdim, d), BF16),
                        pltpu.SemaphoreType.DMA((2,)), pltpu.SemaphoreType.DMA((2,))],
    )
    src3 = src.reshape(n_tiles, 1, MOE_TILE)
    return pl.pallas_call(
        functools.partial(_moe_kernel, d, n_tiles),
        grid_spec=grid_spec,
        out_shape=jax.ShapeDtypeStruct((n_pad, d), F32),
        compiler_params=pltpu.CompilerParams(dimension_semantics=("arbitrary",), vmem_limit_bytes=VMEM_LIMIT,
                                             has_side_effects=True),
        name="moe_experts",
    )(lo, hi, src3, src3, dst.reshape(n_tiles, 1, MOE_TILE), h2ext, w1, w3, w2, w1, w3, w2)


def _moe_block(h2ext, cr, cnt, w1, w3, w2):
    n = h2ext.shape[0]
    n_tiles = n // MOE_TILE + N_CLASSES
    src, dst, lo, hi = _plan(cr, cnt, n_tiles)
    return _moe(h2ext, src, dst, lo, hi, w1, w3, w2)


def _residual_kernel(xm_ref, y_ref, gate_ref, o_ref):
    o_ref[0] = xm_ref[0] + gate_ref[0] * y_ref[...]


def _residual_out(xmid, y, gate_spec, mod_table):
    batch, t, d = xmid.shape
    nch = t // TOK
    return pl.pallas_call(
        _residual_kernel,
        grid=(batch, nch),
        in_specs=[pl.BlockSpec((1, TOK, d), lambda b, c: (b, c, 0)),
                  pl.BlockSpec((TOK, d), lambda b, c: (b * nch + c, 0)),
                  gate_spec],
        out_specs=pl.BlockSpec((1, TOK, d), lambda b, c: (b, c, 0)),
        out_shape=jax.ShapeDtypeStruct((batch, t, d), F32),
        compiler_params=_cparams(("arbitrary", "arbitrary")),
        name="residual_out",
    )(xmid, y, mod_table)


def _qkv_kernel(ncc, xm_ref, y_ref, g2_ref, g_ref, sh_ref, sc_ref, w_ref, qn_ref, kn_ref, cos_ref, sin_ref,
                q_ref, k_ref, v_ref):
    c = pl.program_id(1)
    latent = c >= ncc
    x1 = xm_ref[0] + g2_ref[0] * y_ref[...]
    h = _rms(x1, g_ref[...]) * (1.0 + sc_ref[0]) + sh_ref[0]
    hb = h.astype(BF16)
    nq = ATT_HEADS * HEAD_DIM
    nk = KV_HEADS * HEAD_DIM
    cos = cos_ref[...]
    sin = sin_ref[...]

    def head(z, gain):
        z = _rms(z, gain)
        rot = z * cos + pltpu.roll(z, HEAD_DIM // 2, 1) * sin
        return jnp.where(latent, rot, z)

    kv = jnp.dot(hb, w_ref[:, nq:], preferred_element_type=F32)
    k_ref[0] = jnp.concatenate(
        [head(kv[:, i * HEAD_DIM:(i + 1) * HEAD_DIM], kn_ref[...]) for i in range(KV_HEADS)], axis=1).astype(BF16)
    ones = jnp.ones((TOK, HEAD_DIM), BF16)
    v_ref[0] = jnp.concatenate(
        [blk for i in range(KV_HEADS)
         for blk in (kv[:, nk + i * HEAD_DIM:nk + (i + 1) * HEAD_DIM].astype(BF16), ones)], axis=1)

    @pl.when(latent)
    def _():
        qq = jnp.dot(hb, w_ref[:, :nq], preferred_element_type=F32)
        q_ref[0] = jnp.concatenate(
            [head(qq[:, i * HEAD_DIM:(i + 1) * HEAD_DIM], qn_ref[...]) * (HEAD_DIM ** -0.5)
             for i in range(ATT_HEADS)], axis=1).astype(BF16)


def _qkv(ncc, xmid, y, gain, mods, w_qkv, qn, kn, cos, sin):
    batch, t, d = xmid.shape
    nch = t // TOK
    seq = t - ncc * TOK
    nq = ATT_HEADS * HEAD_DIM
    nk = KV_HEADS * HEAD_DIM
    const = lambda shape: pl.BlockSpec(shape, lambda b, c: (0,) * len(shape))
    lat = lambda b, c: (b, jnp.maximum(c - ncc, 0), 0)
    return pl.pallas_call(
        functools.partial(_qkv_kernel, ncc),
        grid=(batch, nch),
        in_specs=[pl.BlockSpec((1, TOK, d), lambda b, c: (b, c, 0)),
                  pl.BlockSpec((TOK, d), lambda b, c: (b * nch + c, 0)),
                  mods.spec(0, 5),
                  const((1, d)), mods.spec(1, 0), mods.spec(1, 1),
                  const((d, nq + 2 * nk)), const((1, HEAD_DIM)), const((1, HEAD_DIM)),
                  pl.BlockSpec((TOK, HEAD_DIM), lambda b, c: (jnp.maximum(c - ncc, 0), 0)),
                  pl.BlockSpec((TOK, HEAD_DIM), lambda b, c: (jnp.maximum(c - ncc, 0), 0))],
        out_specs=[pl.BlockSpec((1, TOK, nq), lat),
                   pl.BlockSpec((1, TOK, nk), lambda b, c: (b, c, 0)),
                   pl.BlockSpec((1, TOK, 2 * nk), lambda b, c: (b, c, 0))],
        out_shape=[jax.ShapeDtypeStruct((batch, seq, nq), BF16),
                   jax.ShapeDtypeStruct((batch, t, nk), BF16),
                   jax.ShapeDtypeStruct((batch, t, 2 * nk), BF16)],
        compiler_params=_cparams(("arbitrary", "arbitrary")),
        name="qkv",
    )(xmid, y, mods.table, gain, mods.table, mods.table, w_qkv, qn, kn, cos, sin)


def _attn_kernel(n_kb, q_ref, k_ref, v_ref, o_ref):
    q = q_ref[0]
    qs = jnp.concatenate([q[:, g * HEAD_DIM:(g + 1) * HEAD_DIM] for g in range(Q_PER_KV)], axis=0)
    kb = k_ref.shape[1] // n_kb
    m = acc = None
    for j in range(n_kb):
        s = lax.dot_general(qs, k_ref[0, j * kb:(j + 1) * kb, :], (((1,), (1,)), ((), ())),
                            preferred_element_type=F32)
        m_blk = jnp.max(s, axis=-1, keepdims=True)
        m_new = m_blk if m is None else jnp.maximum(m, m_blk)
        p = jnp.exp((s - m_new).astype(BF16))
        pv = jnp.dot(p, v_ref[0, j * kb:(j + 1) * kb, :], preferred_element_type=F32)
        acc = pv if acc is None else jnp.exp(m - m_new) * acc + pv
        m = m_new
    o = acc[:, :HEAD_DIM] / acc[:, HEAD_DIM:HEAD_DIM + 1]
    o_ref[0] = jnp.concatenate([o[g * ATT_Q:(g + 1) * ATT_Q] for g in range(Q_PER_KV)], axis=1).astype(BF16)


def _attention(q, k, v):
    batch, seq, nq = q.shape
    t = k.shape[1]
    gw = Q_PER_KV * HEAD_DIM
    n_kb = 2 if t % (2 * LANES) == 0 else 1
    return pl.pallas_call(
        functools.partial(_attn_kernel, n_kb),
        grid=(batch, KV_HEADS, seq // ATT_Q),
        in_specs=[pl.BlockSpec((1, ATT_Q, gw), lambda b, h, i: (b, i, h)),
                  pl.BlockSpec((1, t, HEAD_DIM), lambda b, h, i: (b, 0, h)),
                  pl.BlockSpec((1, t, 2 * HEAD_DIM), lambda b, h, i: (b, 0, h))],
        out_specs=pl.BlockSpec((1, ATT_Q, gw), lambda b, h, i: (b, i, h)),
        out_shape=jax.ShapeDtypeStruct((batch, seq, nq), BF16),
        compiler_params=_cparams(("arbitrary", "arbitrary", "arbitrary")),
        name="attention",
    )(q, k, v)


def _block_diag(w):
    nb, bs, _ = w.shape
    eye = jnp.eye(nb, dtype=w.dtype)
    return (eye[:, None, :, None] * w[:, :, None, :]).reshape(nb * bs, nb * bs)


def _router_weights(wg, bg, we, be):
    d = wg.shape[0]
    n = N_GROUPS + N_EXPERTS
    wr = jnp.zeros((ROUTER_ROWS, d), F32).at[:N_GROUPS].set(wg.T).at[N_GROUPS:n].set(we.T)
    br = jnp.zeros((ROUTER_ROWS, 1), F32).at[:N_GROUPS, 0].set(bg).at[N_GROUPS:n, 0].set(be)
    w_hi = wr.astype(BF16)
    w_lo = (wr - w_hi.astype(F32)).astype(BF16)
    return jnp.concatenate([w_hi, w_lo], axis=0), br


def _rope_tables(seq):
    rows = seq // GRID_W
    row = np.repeat(np.arange(rows, dtype=np.float32), GRID_W)
    col = np.tile(np.arange(GRID_W, dtype=np.float32), rows)
    ppa = HEAD_DIM // 4
    freqs = (ROPE_THETA ** (-np.arange(ppa, dtype=np.float32) / ppa)).astype(np.float32)
    ang = np.concatenate([row[:, None] * freqs, col[:, None] * freqs], axis=-1)
    cos, sin = np.cos(ang), np.sin(ang)
    return (jnp.asarray(np.concatenate([cos, cos], axis=-1), F32),
            jnp.asarray(np.concatenate([-sin, sin], axis=-1), F32))


_HALF_SPLIT = np.concatenate([np.arange(0, HEAD_DIM, 2), np.arange(1, HEAD_DIM, 2)])


def kernel(x, c, ctx, c_ctx, norm1, norm2, w_ada, b_ada, ev_w_in, ev_conv_w, ev_conv_b, ev_lru_wa, ev_lru_ba, ev_lru_wi, ev_lru_bi, ev_lru_lam, ev_gla_wg, ev_gla_bg, ev_gla_norm, ev_w_out, od_w_qkv, od_q_norm, od_k_norm, od_w_o, moe_wg, moe_bg, moe_we, moe_be, moe_w1, moe_w3, moe_w2):
    batch, seq, d = x.shape
    ctx_len = ctx.shape[1]
    assert seq % TOK == 0 and ctx_len % TOK == 0 and d % GLA_VAL == 0 and seq % GRID_W == 0
    ncc = ctx_len // TOK
    lw = d // 2

    rows = -(-(batch + 1) // SUBLANES) * SUBLANES
    cv = jnp.zeros((rows, d), F32).at[:batch].set(c).at[batch].set(c_ctx)
    table = _adaln(cv, w_ada, b_ada)
    mods = _Mods(table.reshape(table.shape[0] * rows * 6, 1, d), rows, batch, ncc)

    w_in = ev_w_in[0]
    nm = 2 * lw + 2 * GLA_KEY + 2 * GLA_VAL
    w_main = w_in[:, :nm].astype(BF16)
    w_lr = jnp.zeros((d, LANES), F32).at[:, :2 * GLA_RANK].set(w_in[:, nm:]).astype(BF16)
    pm, lr = _inproj(x, ctx, norm1[0][None], mods, w_main, w_lr)

    mixed = []
    for dr in range(2):
        w_gate = jnp.concatenate([_block_diag(ev_lru_wa[0, dr]), _block_diag(ev_lru_wi[0, dr])], axis=1).astype(BF16)
        b_gate = jnp.concatenate([ev_lru_ba[0, dr], ev_lru_bi[0, dr]])[None]
        wg_pad = jnp.zeros((LANES, GLA_KEY), F32).at[dr * GLA_RANK:(dr + 1) * GLA_RANK].set(ev_gla_wg[0, dr]).astype(BF16)
        mixed.append(_seqmix(dr == 1, ncc, pm, lr, ev_conv_w[0], ev_conv_b[0][None], w_gate, b_gate,
                             ev_lru_lam[0, dr][None], wg_pad, ev_gla_bg[0, dr][None], lw))
    (lru_f, gla_f), (lru_r, gla_r) = mixed

    wr0, br0 = _router_weights(moe_wg[0], moe_bg[0], moe_we[0], moe_be[0])
    xmid0, h2e0, cr0, cnt0 = _post0(ncc, lru_f, lru_r, gla_f, gla_r, pm, x, ctx, mods, norm2[0][None],
                                    ev_gla_norm[0][None], ev_w_out[0].astype(BF16), wr0, br0, lw)
    y0 = _moe_block(h2e0, cr0, cnt0, moe_w1[0], moe_w3[0], moe_w2[0])

    nq = ATT_HEADS * HEAD_DIM
    nk = KV_HEADS * HEAD_DIM
    perm = np.concatenate([h * HEAD_DIM + _HALF_SPLIT for h in range(ATT_HEADS + KV_HEADS)]
                          + [np.arange(nq + nk, nq + 2 * nk)])
    w_qkv = od_w_qkv[0][:, perm].astype(BF16)
    cos, sin = _rope_tables(seq)
    q, k, v = _qkv(ncc, xmid0, y0, norm1[1][None], mods, w_qkv, od_q_norm[0][_HALF_SPLIT][None],
                   od_k_norm[0][_HALF_SPLIT][None], cos, sin)
    att = _attention(q, k, v)
    wr1, br1 = _router_weights(moe_wg[1], moe_bg[1], moe_we[1], moe_be[1])
    xmid1, h2e1, cr1, cnt1 = _post1(ncc, att, xmid0, y0, mods, norm2[1][None], od_w_o[0].astype(BF16), wr1, br1)
    y1 = _moe_block(h2e1, cr1, cnt1, moe_w1[1], moe_w3[1], moe_w2[1])
    return _residual_out(xmid1, y1, mods.spec(1, 5, True), mods.table)
```

```python
import functools

import numpy as np
import jax
import jax.numpy as jnp
from jax import lax
from jax.experimental import pallas as pl
from jax.experimental.pallas import tpu as pltpu

F32 = jnp.float32
BF16 = jnp.bfloat16
I32 = jnp.int32
U32 = jnp.uint32
HIGHEST = lax.Precision.HIGHEST

EPS = 1e-6
GRID_W = 64
LRU_BLOCKS = 8
LRU_C = 8.0
GLA_HEADS = 4
GLA_DK = 64
GLA_DV = 128
GLA_KEY = GLA_HEADS * GLA_DK
GLA_VAL = GLA_HEADS * GLA_DV
GLA_RANK = 16
GLA_TAU = 16.0
GLA_CHUNK = 64
_CHUNK_SHIFT = GLA_CHUNK.bit_length() - 1
_DK_SHIFT = GLA_DK.bit_length() - 1
ATT_HEADS = 8
KV_HEADS = 2
Q_PER_KV = ATT_HEADS // KV_HEADS
HEAD_DIM = 128
ROPE_THETA = 10000.0
N_GROUPS = 4
EXPERTS_PER_GROUP = 4
N_EXPERTS = N_GROUPS * EXPERTS_PER_GROUP
PAIRS_PER_GROUP = 6
N_CLASSES = N_GROUPS * PAIRS_PER_GROUP

LANES = 128
SUBLANES = 8
TOK = 256
MOE_TILE = 512
ATT_Q = 256
PERM_ROWS = 512
DMA_UNROLL = 8
WEXT = LANES
ROUTER_ROWS = 32
VMEM_LIMIT = 56 * 1024 * 1024


def _cparams(sem):
    return pltpu.CompilerParams(dimension_semantics=sem, vmem_limit_bytes=VMEM_LIMIT)


def _rms(x, g):
    return x * lax.rsqrt(jnp.mean(x * x, axis=-1, keepdims=True) + EPS) * g


def _sigmoid(x):
    return 1.0 / (1.0 + jnp.exp(-x))


def _silu(x):
    return x * _sigmoid(x)


def _gelu_tanh(x):
    return 0.5 * x * (1.0 + jnp.tanh(np.sqrt(2.0 / np.pi).astype(np.float32) * (x + 0.044715 * (x * x * x))))


def _pack_bf16_pairs(xb):
    k = xb.shape[-1] // 2
    lo = lax.shift_right_logical(pltpu.bitcast(xb[:, :k].astype(F32), U32), jnp.uint32(16))
    hi = pltpu.bitcast(xb[:, k:].astype(F32), U32) & jnp.uint32(0xFFFF0000)
    return hi | lo


def _unpack_bf16_pairs(words):
    lo = pltpu.bitcast(lax.shift_left(words, jnp.uint32(16)), F32).astype(BF16)
    hi = pltpu.bitcast(words & jnp.uint32(0xFFFF0000), F32).astype(BF16)
    return jnp.concatenate([lo, hi], axis=1)


def _softplus(x):
    return jnp.maximum(x, 0.0) + jnp.log(1.0 + jnp.exp(-jnp.abs(x)))


def _log_sigmoid(x):
    return -_softplus(-x)


def _adaln_kernel(cv_ref, w_ref, b_ref, o_ref):
    s = _silu(cv_ref[...])
    o_ref[0] = jnp.dot(s, w_ref[0], precision=HIGHEST, preferred_element_type=F32) + b_ref[0]


def _adaln(cv, w_ada, b_ada):
    depth, d, n6 = w_ada.shape
    rows = cv.shape[0]
    tn = 6 * d // 4
    return pl.pallas_call(
        _adaln_kernel,
        grid=(depth, n6 // tn),
        in_specs=[pl.BlockSpec((rows, d), lambda l, j: (0, 0)),
                  pl.BlockSpec((1, d, tn), lambda l, j: (l, 0, j)),
                  pl.BlockSpec((1, 1, tn), lambda l, j: (l, 0, j))],
        out_specs=pl.BlockSpec((1, rows, tn), lambda l, j: (l, 0, j)),
        out_shape=jax.ShapeDtypeStruct((depth, rows, n6), F32),
        compiler_params=_cparams(("arbitrary", "arbitrary")),
        name="adaln",
    )(cv, w_ada, b_ada.reshape(depth, 1, n6))


class _Mods:
    def __init__(self, table, rows, batch, ncc):
        self.table, self.rows, self.batch, self.ncc = table, rows, batch, ncc
        self.d = table.shape[-1]

    def spec(self, layer, j, latent_only=False):
        rows, batch, ncc = self.rows, self.batch, self.ncc

        def imap(b, c):
            r = b if latent_only else jnp.where(c < ncc, batch, b)
            return ((layer * rows + r) * 6 + j, 0, 0)

        return pl.BlockSpec((1, 1, self.d), imap)


def _inproj_kernel(ncc, x_ref, c_ref, g_ref, sh_ref, sc_ref, w_ref, wlr_ref, pm_ref, lr_ref):
    c = pl.program_id(1)
    xf = jnp.where(c < ncc, c_ref[0], x_ref[0])
    h = _rms(xf, g_ref[...]) * (1.0 + sc_ref[0]) + sh_ref[0]
    hb = h.astype(BF16)
    pm_ref[0] = jnp.dot(hb, w_ref[...], preferred_element_type=F32).astype(BF16)
    lr_ref[0] = jnp.dot(hb, wlr_ref[...], preferred_element_type=F32).astype(BF16)


def _inproj(x, ctx, gain, mods, w_main, w_lr):
    batch, seq, d = x.shape
    ncc = ctx.shape[1] // TOK
    nch = ncc + seq // TOK
    nm = w_main.shape[1]
    return pl.pallas_call(
        functools.partial(_inproj_kernel, ncc),
        grid=(batch, nch),
        in_specs=[pl.BlockSpec((1, TOK, d), lambda b, c: (b, jnp.maximum(c - ncc, 0), 0)),
                  pl.BlockSpec((1, TOK, d), lambda b, c: (b, jnp.minimum(c, ncc - 1), 0)),
                  pl.BlockSpec((1, d), lambda b, c: (0, 0)),
                  mods.spec(0, 0), mods.spec(0, 1),
                  pl.BlockSpec((d, nm), lambda b, c: (0, 0)),
                  pl.BlockSpec((d, LANES), lambda b, c: (0, 0))],
        out_specs=[pl.BlockSpec((1, TOK, nm), lambda b, c: (b, c, 0)),
                   pl.BlockSpec((1, TOK, LANES), lambda b, c: (b, c, 0))],
        out_shape=[jax.ShapeDtypeStruct((batch, nch * TOK, nm), BF16),
                   jax.ShapeDtypeStruct((batch, nch * TOK, LANES), BF16)],
        compiler_params=_cparams(("arbitrary", "arbitrary")),
        name="inproj",
    )(x, ctx, gain, mods.table, mods.table, w_main, w_lr)


def _lru_scan(a, b, h0, rev):
    sub = lax.broadcasted_iota(I32, a.shape, 0) & (SUBLANES - 1)
    for dist in (1, 2, 4):
        shift = TOK - dist if rev else dist
        a_s = pltpu.roll(a, shift, 0)
        b_s = pltpu.roll(b, shift, 0)
        m = (sub < SUBLANES - dist) if rev else (sub >= dist)
        b = jnp.where(m, a * b_s + b, b)
        a = jnp.where(m, a * a_s, a)
    n_groups = TOK // SUBLANES
    order = range(n_groups - 1, -1, -1) if rev else range(n_groups)
    outs = [None] * n_groups
    h = h0
    for r in order:
        hr = a[r * SUBLANES:(r + 1) * SUBLANES] * h + b[r * SUBLANES:(r + 1) * SUBLANES]
        outs[r] = hr
        h = hr[0:1] if rev else hr[SUBLANES - 1:SUBLANES]
    return jnp.concatenate(outs, axis=0), h


def _seqmix_kernel(rev, ncc, nch, xa_ref, xp_ref, xn_ref, q_ref, k_ref, v_ref, lr_ref,
                   cw_ref, cb_ref, wgt_ref, bgt_ref, lam_ref, wg_ref, bg_ref, sel_ref,
                   lru_ref, gla_ref, h_scr, s_scr):
    s = pl.program_id(1)
    nl = nch - ncc
    if rev:
        ch = jnp.where(s < ncc, ncc - 1 - s, ncc + (nl - 1) - (s - ncc))
    else:
        ch = s

    @pl.when(s == 0)
    def _():
        h_scr[...] = jnp.zeros_like(h_scr)
        s_scr[...] = jnp.zeros_like(s_scr)

    lw = xa_ref.shape[-1]
    xa = xa_ref[0].astype(F32)
    has_prev = jnp.logical_and(ch != 0, ch != ncc)
    has_next = jnp.logical_and(ch != ncc - 1, ch != nch - 1)
    prev = jnp.where(has_prev, xp_ref[0].astype(F32), 0.0)
    nxt = jnp.where(has_next, xn_ref[0].astype(F32), 0.0)
    p2, p1, n0 = prev[14:15], prev[15:16], nxt[0:1]
    row = lax.broadcasted_iota(I32, xa.shape, 0)
    x_m1 = jnp.where(row == 0, p1, pltpu.roll(xa, 1, 0))
    x_m2 = jnp.where(row == 0, p2, jnp.where(row == 1, p1, pltpu.roll(xa, 2, 0)))
    x_p1 = jnp.where(row == TOK - 1, n0, pltpu.roll(xa, TOK - 1, 0))
    cw = cw_ref[...]
    u = cw[0:1] * x_m2 + cw[1:2] * x_m1 + cw[2:3] * xa + cw[3:4] * x_p1 + cb_ref[...]

    gates = jnp.dot(u.astype(BF16), wgt_ref[...], preferred_element_type=F32) + bgt_ref[...]
    r_gate = _sigmoid(gates[:, :lw])
    i_gate = _sigmoid(gates[:, lw:])
    log_a = (-LRU_C) * r_gate * _softplus(-lam_ref[...])
    a = jnp.exp(log_a)
    bb = jnp.sqrt(1.0 - a * a) * (i_gate * u)
    hs, h_last = _lru_scan(a, bb, h_scr[0:1], rev)
    h_scr[0:1] = h_last
    lru_ref[0] = hs.astype(BF16)

    lg = _log_sigmoid(jnp.dot(lr_ref[0], wg_ref[...], preferred_element_type=F32) + bg_ref[...]) * (1.0 / GLA_TAU)
    lg_hi = lg.astype(BF16)
    lg_lo = (lg - lg_hi.astype(F32)).astype(BF16)
    cum = jnp.dot(sel_ref[...], jnp.concatenate([lg_hi, lg_lo], axis=1), preferred_element_type=F32)
    g = cum[:TOK, :GLA_KEY] + cum[:TOK, GLA_KEY:]
    tot = cum[TOK:, :GLA_KEY] + cum[TOK:, GLA_KEY:]
    q = q_ref[0].astype(F32)
    k = k_ref[0].astype(F32)
    dec_all = jnp.exp(tot)
    k_neg = k * jnp.exp(-g)
    qg = (q * jnp.exp(g) * (GLA_DK ** -0.5)).astype(BF16)
    kg = k_neg.astype(BF16)
    kd = (k_neg * dec_all).astype(BF16)
    v = v_ref[0]

    lane = lax.broadcasted_iota(I32, (1, GLA_KEY), 1) >> _DK_SHIFT
    head_masks = [(lane == h) for h in range(GLA_HEADS)]
    n_stack = GLA_HEADS * GLA_CHUNK
    ai = lax.broadcasted_iota(I32, (n_stack, GLA_CHUNK), 0) & (GLA_CHUNK - 1)
    aj = lax.broadcasted_iota(I32, (n_stack, GLA_CHUNK), 1)
    causal = (aj >= ai) if rev else (aj <= ai)

    n_sub = TOK // GLA_CHUNK
    order = range(n_sub - 1, -1, -1) if rev else range(n_sub)
    outs = [None] * n_sub
    st = s_scr[...]
    for n in order:
        sl = slice(n * GLA_CHUNK, (n + 1) * GLA_CHUNK)
        qn = qg[sl]
        qs = jnp.concatenate([jnp.where(head_masks[h], qn, jnp.zeros_like(qn)) for h in range(GLA_HEADS)], axis=0)
        att = lax.dot_general(qs, kg[sl], (((1,), (1,)), ((), ())), preferred_element_type=F32)
        att = jnp.where(causal, att, 0.0).astype(BF16)
        o_all = jnp.dot(att, v[sl], preferred_element_type=F32)
        o_int = lax.dot_general(qs, st.astype(BF16), (((1,), (1,)), ((), ())),
                                preferred_element_type=F32)
        outs[n] = jnp.concatenate(
            [o_all[h * GLA_CHUNK:(h + 1) * GLA_CHUNK, h * GLA_DV:(h + 1) * GLA_DV]
             + o_int[h * GLA_CHUNK:(h + 1) * GLA_CHUNK] for h in range(GLA_HEADS)], axis=1)
        ds_t = lax.dot_general(v[sl], kd[sl], (((0,), (0,)), ((), ())), preferred_element_type=F32)
        ds = jnp.zeros_like(st)
        for h in range(GLA_HEADS):
            ds = ds + jnp.where(head_masks[h], ds_t[h * GLA_DV:(h + 1) * GLA_DV], 0.0)
        st = dec_all[n * GLA_CHUNK:n * GLA_CHUNK + 1] * st + ds
    s_scr[...] = st
    gla_ref[0] = jnp.concatenate(outs, axis=0).astype(BF16)


def _cumsum_selector(rev):
    r = np.arange(TOK)[:, None]
    c = np.arange(TOK)[None, :]
    same = (r // GLA_CHUNK) == (c // GLA_CHUNK)
    tri = same & ((c >= r) if rev else (c <= r))
    return jnp.asarray(np.concatenate([tri, same], axis=0), BF16)


def _seqmix(rev, ncc, pm, lr, conv_w, conv_b, w_gate, b_gate, lam, wg_pad, bg, lw):
    batch, t, _ = pm.shape
    nch = t // TOK
    nl = nch - ncc
    hb = TOK // 16

    def chunk(s):
        if rev:
            return jnp.where(s < ncc, ncc - 1 - s, ncc + (nl - 1) - (s - ncc))
        return s

    qcol = 2 * lw // GLA_KEY
    vcol = (2 * lw + 2 * GLA_KEY) // GLA_VAL
    const = lambda shape: pl.BlockSpec(shape, lambda b, s: (0,) * len(shape))
    return pl.pallas_call(
        functools.partial(_seqmix_kernel, rev, ncc, nch),
        grid=(batch, nch),
        in_specs=[pl.BlockSpec((1, TOK, lw), lambda b, s: (b, chunk(s), 0)),
                  pl.BlockSpec((1, 16, lw), lambda b, s: (b, jnp.maximum(chunk(s) * hb - 1, 0), 0)),
                  pl.BlockSpec((1, 16, lw), lambda b, s: (b, jnp.minimum((chunk(s) + 1) * hb, nch * hb - 1), 0)),
                  pl.BlockSpec((1, TOK, GLA_KEY), lambda b, s: (b, chunk(s), qcol)),
                  pl.BlockSpec((1, TOK, GLA_KEY), lambda b, s: (b, chunk(s), qcol + 1)),
                  pl.BlockSpec((1, TOK, GLA_VAL), lambda b, s: (b, chunk(s), vcol)),
                  pl.BlockSpec((1, TOK, LANES), lambda b, s: (b, chunk(s), 0)),
                  const((4, lw)), const((1, lw)), const((lw, 2 * lw)), const((1, 2 * lw)), const((1, lw)),
                  const((LANES, GLA_KEY)), const((1, GLA_KEY)), const((2 * TOK, TOK))],
        out_specs=[pl.BlockSpec((1, TOK, lw), lambda b, s: (b, chunk(s), 0)),
                   pl.BlockSpec((1, TOK, GLA_VAL), lambda b, s: (b, chunk(s), 0))],
        out_shape=[jax.ShapeDtypeStruct((batch, t, lw), BF16),
                   jax.ShapeDtypeStruct((batch, t, GLA_VAL), BF16)],
        scratch_shapes=[pltpu.VMEM((SUBLANES, lw), F32), pltpu.VMEM((GLA_DV, GLA_KEY), F32)],
        compiler_params=_cparams(("arbitrary", "arbitrary")),
        name="seqmix_rev" if rev else "seqmix_fwd",
    )(pm, pm, pm, pm, pm, pm, lr, conv_w, conv_b, w_gate, b_gate, lam, wg_pad, bg, _cumsum_selector(rev))


def _route_epilogue(first, xm, n2_ref, sh2_ref, sc2_ref, wr_ref, br_ref,
                    xmid_ref, h2_ref, cr_ref, cnt_ref, carry):
    d = xm.shape[-1]
    xmid_ref[0] = xm
    h2 = _rms(xm, n2_ref[...]) * (1.0 + sc2_ref[0]) + sh2_ref[0]
    h_hi = h2.astype(BF16)
    h_lo = (h2 - h_hi.astype(F32)).astype(BF16)
    nt = (((1,), (1,)), ((), ()))
    o1 = lax.dot_general(wr_ref[...], h_hi, nt, preferred_element_type=F32)
    o2 = lax.dot_general(wr_ref[:ROUTER_ROWS], h_lo, nt, preferred_element_type=F32)
    logits = o1[:ROUTER_ROWS] + o1[ROUTER_ROWS:] + o2 + br_ref[...]

    def col(j):
        return logits[j:j + 1]

    lgs = [col(g) for g in range(N_GROUPS)]
    gmax = functools.reduce(jnp.maximum, lgs)
    gi = jnp.where(lgs[0] == gmax, 0, jnp.where(lgs[1] == gmax, 1, jnp.where(lgs[2] == gmax, 2, 3)))
    w_group = 1.0 / functools.reduce(lambda p, q: p + q, [jnp.exp(l - gmax) for l in lgs])
    es = []
    for j in range(EXPERTS_PER_GROUP):
        acc = jnp.zeros_like(gmax)
        for g in range(N_GROUPS):
            acc = acc + jnp.where(gi == g, col(N_GROUPS + g * EXPERTS_PER_GROUP + j), 0.0)
        es.append(acc)
    m1 = functools.reduce(jnp.maximum, es)
    i1 = jnp.where(es[0] == m1, 0, jnp.where(es[1] == m1, 1, jnp.where(es[2] == m1, 2, 3)))
    rest = [jnp.where(i1 == j, -jnp.inf, es[j]) for j in range(EXPERTS_PER_GROUP)]
    m2 = functools.reduce(jnp.maximum, rest)
    i2 = jnp.where(rest[0] == m2, 0, jnp.where(rest[1] == m2, 1, jnp.where(rest[2] == m2, 2, 3)))
    e2 = jnp.exp(m2 - m1)
    w1 = w_group / (1.0 + e2)
    w2 = w_group * e2 / (1.0 + e2)
    first_lo = i1 < i2
    lo = jnp.where(first_lo, i1, i2)
    hi = jnp.where(first_lo, i2, i1)
    w_lo = jnp.where(first_lo, w1, w2)
    w_hi = jnp.where(first_lo, w2, w1)
    pidx = jnp.where(lo == 0, hi - 1, jnp.where(lo == 1, hi + 1, 5))
    cls = gi * PAIRS_PER_GROUP + pidx

    def terms(w):
        t1 = w.astype(BF16).astype(F32)
        t2 = (w - t1).astype(BF16).astype(F32)
        t3 = (w - t1 - t2).astype(BF16).astype(F32)
        return [t1, t2, t3]

    sub = lax.broadcasted_iota(I32, (2 * SUBLANES, TOK), 0)
    stacked = jnp.zeros((2 * SUBLANES, TOK), F32)
    for r, term in enumerate(terms(w_lo) + terms(w_hi)):
        stacked = jnp.where(sub == r, term, stacked)
    er = lax.broadcasted_iota(I32, (2 * SUBLANES, LANES), 0)
    el = lax.broadcasted_iota(I32, (2 * SUBLANES, LANES), 1)
    place = jnp.logical_or(jnp.logical_and(er < 3, el == 0),
                           jnp.logical_and(jnp.logical_and(er >= 3, er < 6), el == 1))
    w_cols = lax.dot_general(stacked.astype(BF16), place.astype(BF16), (((0,), (0,)), ((), ())),
                             preferred_element_type=F32)
    h2_ref[:, :d // 2] = _pack_bf16_pairs(h_hi)
    h2_ref[:, d // 2:] = pltpu.bitcast(w_cols, U32)

    @pl.when(first)
    def _():
        carry[...] = jnp.zeros_like(carry)

    krow = lax.broadcasted_iota(I32, (ROUTER_ROWS, TOK), 0)
    onehot = krow == cls
    ri = lax.broadcasted_iota(I32, (TOK, TOK), 0)
    ci = lax.broadcasted_iota(I32, (TOK, TOK), 1)
    triu = (ri <= ci).astype(BF16)
    pref = jnp.dot(onehot.astype(BF16), triu, preferred_element_type=F32)
    base = carry[:, 0:1]
    rank = jnp.sum(jnp.where(onehot, pref - 1.0 + base, 0.0), axis=0, keepdims=True)
    new_carry = jnp.broadcast_to(base + jnp.sum(onehot.astype(F32), axis=1, keepdims=True), carry.shape)
    carry[...] = new_carry
    cnt_ref[...] = new_carry
    sub8 = lax.broadcasted_iota(I32, (SUBLANES, TOK), 0)
    cr_ref[0] = jnp.where(sub8 == 0, cls, jnp.where(sub8 == 1, rank.astype(I32), 0))


def _post0_kernel(ncc, l0_ref, l1_ref, g0_ref, g1_ref, ga_ref, r_ref, x_ref, c_ref,
                  gate_ref, sh2_ref, sc2_ref, n2_ref, gn_ref, wout_ref, wr_ref, br_ref,
                  xmid_ref, h2_ref, cr_ref, cnt_ref, carry):
    b, c = pl.program_id(0), pl.program_id(1)
    lru = l0_ref[0].astype(F32) + l1_ref[0].astype(F32)
    ya = lru * _gelu_tanh(ga_ref[0].astype(F32))
    gla = g0_ref[0].astype(F32) + g1_ref[0].astype(F32)
    parts = []
    for h in range(GLA_HEADS):
        parts.append(_rms(gla[:, h * GLA_DV:(h + 1) * GLA_DV], gn_ref[...]))
    yb = jnp.concatenate(parts, axis=1) * _silu(r_ref[0].astype(F32))
    ycat = jnp.concatenate([ya, yb], axis=1).astype(BF16)
    y = jnp.dot(ycat, wout_ref[...], preferred_element_type=F32)
    x0 = jnp.where(c < ncc, c_ref[0], x_ref[0])
    xm = x0 + gate_ref[0] * y
    _route_epilogue(jnp.logical_and(b == 0, c == 0), xm, n2_ref, sh2_ref, sc2_ref, wr_ref, br_ref,
                    xmid_ref, h2_ref, cr_ref, cnt_ref, carry)


def _post1_kernel(o_ref, x_ref, gate_ref, sh2_ref, sc2_ref, n2_ref, wo_ref, wr_ref, br_ref,
                  xmid_ref, h2_ref, cr_ref, cnt_ref, carry):
    b, c = pl.program_id(0), pl.program_id(1)
    y = jnp.dot(o_ref[0], wo_ref[...], preferred_element_type=F32)
    xm = x_ref[0] + gate_ref[0] * y
    _route_epilogue(jnp.logical_and(b == 0, c == 0), xm, n2_ref, sh2_ref, sc2_ref, wr_ref, br_ref,
                    xmid_ref, h2_ref, cr_ref, cnt_ref, carry)


def _route_out(batch, nch, d):
    n = batch * nch * TOK
    specs = [pl.BlockSpec((1, TOK, d), lambda b, c: (b, c, 0)),
             pl.BlockSpec((TOK, d // 2 + WEXT), lambda b, c: (b * nch + c, 0)),
             pl.BlockSpec((1, SUBLANES, TOK), lambda b, c: (b * nch + c, 0, 0)),
             pl.BlockSpec((ROUTER_ROWS, LANES), lambda b, c: (0, 0))]
    shapes = [jax.ShapeDtypeStruct((batch, nch * TOK, d), F32),
              jax.ShapeDtypeStruct((n, d // 2 + WEXT), U32),
              jax.ShapeDtypeStruct((batch * nch, SUBLANES, TOK), I32),
              jax.ShapeDtypeStruct((ROUTER_ROWS, LANES), F32)]
    return specs, shapes


def _post0(ncc, lru_f, lru_r, gla_f, gla_r, pm, x, ctx, mods, n2, gn, w_out, wr, br, lw):
    batch, t, _ = pm.shape
    d = x.shape[-1]
    nch = t // TOK
    rcol = (2 * lw + 2 * GLA_KEY + GLA_VAL) // GLA_VAL
    tokspec = lambda w: pl.BlockSpec((1, TOK, w), lambda b, c: (b, c, 0))
    const = lambda shape: pl.BlockSpec(shape, lambda b, c: (0,) * len(shape))
    out_specs, out_shape = _route_out(batch, nch, d)
    return pl.pallas_call(
        functools.partial(_post0_kernel, ncc),
        grid=(batch, nch),
        in_specs=[tokspec(lw), tokspec(lw), tokspec(GLA_VAL), tokspec(GLA_VAL),
                  pl.BlockSpec((1, TOK, lw), lambda b, c: (b, c, 1)),
                  pl.BlockSpec((1, TOK, GLA_VAL), lambda b, c: (b, c, rcol)),
                  pl.BlockSpec((1, TOK, d), lambda b, c: (b, jnp.maximum(c - ncc, 0), 0)),
                  pl.BlockSpec((1, TOK, d), lambda b, c: (b, jnp.minimum(c, ncc - 1), 0)),
                  mods.spec(0, 2), mods.spec(0, 3), mods.spec(0, 4),
                  const((1, d)), const((1, GLA_DV)), const((lw + GLA_VAL, d)),
                  const((2 * ROUTER_ROWS, d)), const((ROUTER_ROWS, 1))],
        out_specs=out_specs, out_shape=out_shape,
        scratch_shapes=[pltpu.VMEM((ROUTER_ROWS, LANES), F32)],
        compiler_params=_cparams(("arbitrary", "arbitrary")),
        name="post0",
    )(lru_f, lru_r, gla_f, gla_r, pm, pm, x, ctx, mods.table, mods.table, mods.table, n2, gn, w_out, wr, br)


def _post1(ncc, att, x1, mods, n2, w_o, wr, br):
    batch, seq, aw = att.shape
    d = x1.shape[-1]
    nl = seq // TOK
    const = lambda shape: pl.BlockSpec(shape, lambda b, c: (0,) * len(shape))
    out_specs, out_shape = _route_out(batch, nl, d)
    return pl.pallas_call(
        _post1_kernel,
        grid=(batch, nl),
        in_specs=[pl.BlockSpec((1, TOK, aw), lambda b, c: (b, c, 0)),
                  pl.BlockSpec((1, TOK, d), lambda b, c: (b, c + ncc, 0)),
                  mods.spec(1, 2, True), mods.spec(1, 3, True), mods.spec(1, 4, True),
                  const((1, d)), const((aw, d)), const((2 * ROUTER_ROWS, d)), const((ROUTER_ROWS, 1))],
        out_specs=out_specs, out_shape=out_shape,
        scratch_shapes=[pltpu.VMEM((ROUTER_ROWS, LANES), F32)],
        compiler_params=_cparams(("arbitrary", "arbitrary")),
        name="post1",
    )(att, x1, mods.table, mods.table, mods.table, n2, w_o, wr, br)


_PAIR_LO = np.array([0, 0, 0, 1, 1, 2], np.int32)
_PAIR_HI = np.array([1, 2, 3, 2, 3, 3], np.int32)


def _plan(cr, cnt, n_tiles):
    cls, rank = cr[:, 0, :].reshape(-1), cr[:, 1, :].reshape(-1)
    counts = cnt[:N_CLASSES, 0].astype(I32)
    tiles = (counts + MOE_TILE - 1) // MOE_TILE
    ends = jnp.cumsum(tiles)
    offs = (ends - tiles) * MOE_TILE
    classes = jnp.arange(N_CLASSES, dtype=I32)
    pos = jnp.sum(jnp.where(cls[:, None] == classes[None, :], offs[None, :], 0), axis=1) + rank
    tile_ids = jnp.arange(n_tiles, dtype=I32)
    tile_cls = jnp.minimum(jnp.sum((tile_ids[:, None] >= ends[None, :]).astype(I32), axis=1), N_CLASSES - 1)
    grp = tile_cls // PAIRS_PER_GROUP
    pair = tile_cls % PAIRS_PER_GROUP
    lo = grp * EXPERTS_PER_GROUP + jnp.asarray(_PAIR_LO)[pair]
    hi = grp * EXPERTS_PER_GROUP + jnp.asarray(_PAIR_HI)[pair]
    return pos.astype(I32), lo.astype(I32), hi.astype(I32)


def _moe_kernel(d, lo_ref, hi_ref, x_ref, w1l, w3l, w2l, w1h, w3h, w2h, y_ref, wbuf, wbuf2):
    i = pl.program_id(0)

    changed = jnp.logical_or(i == 0, jnp.logical_or(lo_ref[i] != lo_ref[jnp.maximum(i - 1, 0)],
                                                    hi_ref[i] != hi_ref[jnp.maximum(i - 1, 0)]))

    @pl.when(changed)
    def _():
        for k, w in enumerate((w1l, w3l, w1h, w3h)):
            wbuf[k] = w[0].astype(BF16)
        wbuf2[0] = w2l[0].astype(BF16)
        wbuf2[1] = w2h[0].astype(BF16)

    x = _unpack_bf16_pairs(x_ref[:, :d // 2])
    wts = pltpu.bitcast(x_ref[:, d // 2:], F32)

    def expert(k1, k3, k2):
        a = jnp.dot(x, wbuf[k1], preferred_element_type=F32)
        b = jnp.dot(x, wbuf[k3], preferred_element_type=F32)
        return jnp.dot((_silu(a) * b).astype(BF16), wbuf2[k2], preferred_element_type=F32)

    y = wts[:, 0:1] * expert(0, 1, 0) + wts[:, 1:2] * expert(2, 3, 1)
    y_ref[...] = _pack_bf16_pairs(y.astype(BF16))


def _moe(xs, lo, hi, w1, w3, w2):
    n_pad, wd = xs.shape
    d = 2 * (wd - WEXT)
    hdim = w1.shape[-1]
    n_tiles = n_pad // MOE_TILE
    wl = lambda i, lo, hi: (lo[i], 0, 0)
    wh = lambda i, lo, hi: (hi[i], 0, 0)
    grid_spec = pltpu.PrefetchScalarGridSpec(
        num_scalar_prefetch=2,
        grid=(n_tiles,),
        in_specs=[pl.BlockSpec((MOE_TILE, wd), lambda i, lo, hi: (i, 0)),
                  pl.BlockSpec((1, d, hdim), wl), pl.BlockSpec((1, d, hdim), wl), pl.BlockSpec((1, hdim, d), wl),
                  pl.BlockSpec((1, d, hdim), wh), pl.BlockSpec((1, d, hdim), wh), pl.BlockSpec((1, hdim, d), wh)],
        out_specs=pl.BlockSpec((MOE_TILE, d // 2), lambda i, lo, hi: (i, 0)),
        scratch_shapes=[pltpu.VMEM((4, d, hdim), BF16), pltpu.VMEM((2, hdim, d), BF16)],
    )
    return pl.pallas_call(
        functools.partial(_moe_kernel, d),
        grid_spec=grid_spec,
        out_shape=jax.ShapeDtypeStruct((n_pad, d // 2), U32),
        compiler_params=_cparams(("arbitrary",)),
        name="moe_experts",
    )(lo, hi, xs, w1, w3, w2, w1, w3, w2)


def _row_dma_loops(tile_rows, make_copy, index):
    def issue(grp, carry):
        for sub in range(SUBLANES):
            make_copy(grp, sub, index(grp * SUBLANES + sub)).start()
        return carry

    def drain(grp, carry):
        for _ in range(SUBLANES):
            make_copy(0, 0, 0).wait()
        return carry

    n_groups = tile_rows // SUBLANES
    return (lambda: lax.fori_loop(0, n_groups, issue, 0)), (lambda: lax.fori_loop(0, n_groups, drain, 0))


def _scatter_rows_kernel(rows, pos_ref, src_ref, init_ref, dst_ref, sem):
    del init_ref
    issue, drain = _row_dma_loops(
        rows,
        lambda grp, sub, p: pltpu.make_async_copy(src_ref.at[grp, pl.ds(sub, 1)], dst_ref.at[pl.ds(p, 1)], sem),
        lambda j: pos_ref[0, 0, j])
    issue()
    drain()


def _scatter_rows(src, pos, n_pad):
    n, w = src.shape
    rows = PERM_ROWS if n % PERM_ROWS == 0 else TOK
    steps = n // rows
    return pl.pallas_call(
        functools.partial(_scatter_rows_kernel, rows),
        grid=(steps,),
        in_specs=[pl.BlockSpec((1, 1, rows), lambda i: (i, 0, 0), memory_space=pltpu.SMEM),
                  pl.BlockSpec((rows // SUBLANES, SUBLANES, w), lambda i: (i, 0, 0)),
                  pl.BlockSpec(memory_space=pl.ANY)],
        out_specs=pl.BlockSpec(memory_space=pl.ANY),
        out_shape=jax.ShapeDtypeStruct((n_pad, w), src.dtype),
        scratch_shapes=[pltpu.SemaphoreType.DMA(())],
        input_output_aliases={2: 0},
        compiler_params=pltpu.CompilerParams(dimension_semantics=("arbitrary",), has_side_effects=True),
        name="scatter_rows",
    )(pos.reshape(steps, 1, rows), src.reshape(n // SUBLANES, SUBLANES, w), jnp.zeros((n_pad, w), src.dtype))


def _gather_residual_kernel(n_steps, pos_ref, posn_ref, ys_ref, xm_ref, gate_ref, o_ref, buf, sem):
    step = pl.program_id(0) * pl.num_programs(1) + pl.program_id(1)
    slot = step % 2

    def loops(idx_ref, s):
        return _row_dma_loops(
            TOK,
            lambda grp, sub, p: pltpu.make_async_copy(ys_ref.at[pl.ds(p, 1)], buf.at[s, grp, pl.ds(sub, 1)],
                                                      sem.at[s]),
            lambda j: idx_ref[0, 0, j])

    @pl.when(step == 0)
    def _():
        loops(pos_ref, slot)[0]()

    @pl.when(step + 1 < n_steps)
    def _():
        loops(posn_ref, 1 - slot)[0]()

    loops(pos_ref, slot)[1]()
    y = _unpack_bf16_pairs(buf[slot].reshape(TOK, buf.shape[-1])).astype(F32)
    o_ref[0] = xm_ref[0] + gate_ref[0] * y


def _gather_residual(ys, pos, xmid, gate_spec, mod_table):
    batch, t, d = xmid.shape
    nch = t // TOK
    n_steps = batch * nch
    return pl.pallas_call(
        functools.partial(_gather_residual_kernel, n_steps),
        grid=(batch, nch),
        in_specs=[pl.BlockSpec((1, 1, TOK), lambda b, c: (b * nch + c, 0, 0), memory_space=pltpu.SMEM),
                  pl.BlockSpec((1, 1, TOK), lambda b, c: (jnp.minimum(b * nch + c + 1, n_steps - 1), 0, 0),
                               memory_space=pltpu.SMEM),
                  pl.BlockSpec(memory_space=pl.ANY),
                  pl.BlockSpec((1, TOK, d), lambda b, c: (b, c, 0)),
                  gate_spec],
        out_specs=pl.BlockSpec((1, TOK, d), lambda b, c: (b, c, 0)),
        out_shape=jax.ShapeDtypeStruct((batch, t, d), F32),
        scratch_shapes=[pltpu.VMEM((2, TOK // SUBLANES, SUBLANES, d // 2), U32), pltpu.SemaphoreType.DMA((2,))],
        compiler_params=_cparams(("arbitrary", "arbitrary")),
        name="gather_residual",
    )(pos.reshape(n_steps, 1, TOK), pos.reshape(n_steps, 1, TOK), ys, xmid, mod_table)


def _moe_block(h2ext, cr, cnt, xmid, gate_spec, mod_table, w1, w3, w2):
    n = h2ext.shape[0]
    n_tiles = n // MOE_TILE + N_CLASSES
    pos, lo, hi = _plan(cr, cnt, n_tiles)
    xs = _scatter_rows(h2ext, pos, n_tiles * MOE_TILE)
    ys = _moe(xs, lo, hi, w1, w3, w2)
    return _gather_residual(ys, pos, xmid, gate_spec, mod_table)


def _qkv_kernel(ncc, x_ref, g_ref, sh_ref, sc_ref, w_ref, qn_ref, kn_ref, cos_ref, sin_ref,
                q_ref, k_ref, v_ref):
    c = pl.program_id(1)
    latent = c >= ncc
    h = _rms(x_ref[0], g_ref[...]) * (1.0 + sc_ref[0]) + sh_ref[0]
    hb = h.astype(BF16)
    nq = ATT_HEADS * HEAD_DIM
    nk = KV_HEADS * HEAD_DIM
    cos = cos_ref[...]
    sin = sin_ref[...]

    def head(z, gain):
        z = _rms(z, gain)
        rot = z * cos + pltpu.roll(z, HEAD_DIM // 2, 1) * sin
        return jnp.where(latent, rot, z)

    kv = jnp.dot(hb, w_ref[:, nq:], preferred_element_type=F32)
    k_ref[0] = jnp.concatenate(
        [head(kv[:, i * HEAD_DIM:(i + 1) * HEAD_DIM], kn_ref[...]) for i in range(KV_HEADS)], axis=1).astype(BF16)
    ones = jnp.ones((TOK, HEAD_DIM), BF16)
    v_ref[0] = jnp.concatenate(
        [blk for i in range(KV_HEADS)
         for blk in (kv[:, nk + i * HEAD_DIM:nk + (i + 1) * HEAD_DIM].astype(BF16), ones)], axis=1)

    @pl.when(latent)
    def _():
        qq = jnp.dot(hb, w_ref[:, :nq], preferred_element_type=F32)
        q_ref[0] = jnp.concatenate(
            [head(qq[:, i * HEAD_DIM:(i + 1) * HEAD_DIM], qn_ref[...]) * (HEAD_DIM ** -0.5)
             for i in range(ATT_HEADS)], axis=1).astype(BF16)


def _qkv(ncc, x1, gain, mods, w_qkv, qn, kn, cos, sin):
    batch, t, d = x1.shape
    nch = t // TOK
    seq = t - ncc * TOK
    nq = ATT_HEADS * HEAD_DIM
    nk = KV_HEADS * HEAD_DIM
    const = lambda shape: pl.BlockSpec(shape, lambda b, c: (0,) * len(shape))
    lat = lambda b, c: (b, jnp.maximum(c - ncc, 0), 0)
    return pl.pallas_call(
        functools.partial(_qkv_kernel, ncc),
        grid=(batch, nch),
        in_specs=[pl.BlockSpec((1, TOK, d), lambda b, c: (b, c, 0)),
                  const((1, d)), mods.spec(1, 0), mods.spec(1, 1),
                  const((d, nq + 2 * nk)), const((1, HEAD_DIM)), const((1, HEAD_DIM)),
                  pl.BlockSpec((TOK, HEAD_DIM), lambda b, c: (jnp.maximum(c - ncc, 0), 0)),
                  pl.BlockSpec((TOK, HEAD_DIM), lambda b, c: (jnp.maximum(c - ncc, 0), 0))],
        out_specs=[pl.BlockSpec((1, TOK, nq), lat),
                   pl.BlockSpec((1, TOK, nk), lambda b, c: (b, c, 0)),
                   pl.BlockSpec((1, TOK, 2 * nk), lambda b, c: (b, c, 0))],
        out_shape=[jax.ShapeDtypeStruct((batch, seq, nq), BF16),
                   jax.ShapeDtypeStruct((batch, t, nk), BF16),
                   jax.ShapeDtypeStruct((batch, t, 2 * nk), BF16)],
        compiler_params=_cparams(("arbitrary", "arbitrary")),
        name="qkv",
    )(x1, gain, mods.table, mods.table, w_qkv, qn, kn, cos, sin)


def _attn_kernel(n_kb, q_ref, k_ref, v_ref, o_ref):
    q = q_ref[0]
    qs = jnp.concatenate([q[:, g * HEAD_DIM:(g + 1) * HEAD_DIM] for g in range(Q_PER_KV)], axis=0)
    kb = k_ref.shape[1] // n_kb
    m = acc = None
    for j in range(n_kb):
        s = lax.dot_general(qs, k_ref[0, j * kb:(j + 1) * kb, :], (((1,), (1,)), ((), ())),
                            preferred_element_type=F32)
        m_blk = jnp.max(s, axis=-1, keepdims=True)
        m_new = m_blk if m is None else jnp.maximum(m, m_blk)
        p = jnp.exp((s - m_new).astype(BF16))
        pv = jnp.dot(p, v_ref[0, j * kb:(j + 1) * kb, :], preferred_element_type=F32)
        acc = pv if acc is None else jnp.exp(m - m_new) * acc + pv
        m = m_new
    o = acc[:, :HEAD_DIM] / acc[:, HEAD_DIM:HEAD_DIM + 1]
    o_ref[0] = jnp.concatenate([o[g * ATT_Q:(g + 1) * ATT_Q] for g in range(Q_PER_KV)], axis=1).astype(BF16)


def _attention(q, k, v):
    batch, seq, nq = q.shape
    t = k.shape[1]
    gw = Q_PER_KV * HEAD_DIM
    n_kb = 2 if t % (2 * LANES) == 0 else 1
    return pl.pallas_call(
        functools.partial(_attn_kernel, n_kb),
        grid=(batch, KV_HEADS, seq // ATT_Q),
        in_specs=[pl.BlockSpec((1, ATT_Q, gw), lambda b, h, i: (b, i, h)),
                  pl.BlockSpec((1, t, HEAD_DIM), lambda b, h, i: (b, 0, h)),
                  pl.BlockSpec((1, t, 2 * HEAD_DIM), lambda b, h, i: (b, 0, h))],
        out_specs=pl.BlockSpec((1, ATT_Q, gw), lambda b, h, i: (b, i, h)),
        out_shape=jax.ShapeDtypeStruct((batch, seq, nq), BF16),
        compiler_params=_cparams(("arbitrary", "arbitrary", "arbitrary")),
        name="attention",
    )(q, k, v)


def _block_diag(w):
    nb, bs, _ = w.shape
    eye = jnp.eye(nb, dtype=w.dtype)
    return (eye[:, None, :, None] * w[:, :, None, :]).reshape(nb * bs, nb * bs)


def _router_weights(wg, bg, we, be):
    d = wg.shape[0]
    n = N_GROUPS + N_EXPERTS
    wr = jnp.zeros((ROUTER_ROWS, d), F32).at[:N_GROUPS].set(wg.T).at[N_GROUPS:n].set(we.T)
    br = jnp.zeros((ROUTER_ROWS, 1), F32).at[:N_GROUPS, 0].set(bg).at[N_GROUPS:n, 0].set(be)
    w_hi = wr.astype(BF16)
    w_lo = (wr - w_hi.astype(F32)).astype(BF16)
    return jnp.concatenate([w_hi, w_lo], axis=0), br


def _rope_tables(seq):
    rows = seq // GRID_W
    row = np.repeat(np.arange(rows, dtype=np.float32), GRID_W)
    col = np.tile(np.arange(GRID_W, dtype=np.float32), rows)
    ppa = HEAD_DIM // 4
    freqs = (ROPE_THETA ** (-np.arange(ppa, dtype=np.float32) / ppa)).astype(np.float32)
    ang = np.concatenate([row[:, None] * freqs, col[:, None] * freqs], axis=-1)
    cos, sin = np.cos(ang), np.sin(ang)
    return (jnp.asarray(np.concatenate([cos, cos], axis=-1), F32),
            jnp.asarray(np.concatenate([-sin, sin], axis=-1), F32))


_HALF_SPLIT = np.concatenate([np.arange(0, HEAD_DIM, 2), np.arange(1, HEAD_DIM, 2)])


def kernel(x, c, ctx, c_ctx, norm1, norm2, w_ada, b_ada, ev_w_in, ev_conv_w, ev_conv_b, ev_lru_wa, ev_lru_ba, ev_lru_wi, ev_lru_bi, ev_lru_lam, ev_gla_wg, ev_gla_bg, ev_gla_norm, ev_w_out, od_w_qkv, od_q_norm, od_k_norm, od_w_o, moe_wg, moe_bg, moe_we, moe_be, moe_w1, moe_w3, moe_w2):
    batch, seq, d = x.shape
    ctx_len = ctx.shape[1]
    assert seq % TOK == 0 and ctx_len % TOK == 0 and d % GLA_VAL == 0 and seq % GRID_W == 0
    ncc = ctx_len // TOK
    lw = d // 2

    rows = -(-(batch + 1) // SUBLANES) * SUBLANES
    cv = jnp.zeros((rows, d), F32).at[:batch].set(c).at[batch].set(c_ctx)
    table = _adaln(cv, w_ada, b_ada)
    mods = _Mods(table.reshape(table.shape[0] * rows * 6, 1, d), rows, batch, ncc)

    w_in = ev_w_in[0]
    nm = 2 * lw + 2 * GLA_KEY + 2 * GLA_VAL
    w_main = w_in[:, :nm].astype(BF16)
    w_lr = jnp.zeros((d, LANES), F32).at[:, :2 * GLA_RANK].set(w_in[:, nm:]).astype(BF16)
    pm, lr = _inproj(x, ctx, norm1[0][None], mods, w_main, w_lr)

    mixed = []
    for dr in range(2):
        w_gate = jnp.concatenate([_block_diag(ev_lru_wa[0, dr]), _block_diag(ev_lru_wi[0, dr])], axis=1).astype(BF16)
        b_gate = jnp.concatenate([ev_lru_ba[0, dr], ev_lru_bi[0, dr]])[None]
        wg_pad = jnp.zeros((LANES, GLA_KEY), F32).at[dr * GLA_RANK:(dr + 1) * GLA_RANK].set(ev_gla_wg[0, dr]).astype(BF16)
        mixed.append(_seqmix(dr == 1, ncc, pm, lr, ev_conv_w[0], ev_conv_b[0][None], w_gate, b_gate,
                             ev_lru_lam[0, dr][None], wg_pad, ev_gla_bg[0, dr][None], lw))
    (lru_f, gla_f), (lru_r, gla_r) = mixed

    wr0, br0 = _router_weights(moe_wg[0], moe_bg[0], moe_we[0], moe_be[0])
    xmid0, h2e0, cr0, cnt0 = _post0(ncc, lru_f, lru_r, gla_f, gla_r, pm, x, ctx, mods, norm2[0][None],
                                    ev_gla_norm[0][None], ev_w_out[0].astype(BF16), wr0, br0, lw)
    x1 = _moe_block(h2e0, cr0, cnt0, xmid0, mods.spec(0, 5), mods.table, moe_w1[0], moe_w3[0], moe_w2[0])

    nq = ATT_HEADS * HEAD_DIM
    nk = KV_HEADS * HEAD_DIM
    perm = np.concatenate([h * HEAD_DIM + _HALF_SPLIT for h in range(ATT_HEADS + KV_HEADS)]
                          + [np.arange(nq + nk, nq + 2 * nk)])
    w_qkv = od_w_qkv[0][:, perm].astype(BF16)
    cos, sin = _rope_tables(seq)
    q, k, v = _qkv(ncc, x1, norm1[1][None], mods, w_qkv, od_q_norm[0][_HALF_SPLIT][None],
                   od_k_norm[0][_HALF_SPLIT][None], cos, sin)
    att = _attention(q, k, v)
    wr1, br1 = _router_weights(moe_wg[1], moe_bg[1], moe_we[1], moe_be[1])
    xmid1, h2e1, cr1, cnt1 = _post1(ncc, att, x1, mods, norm2[1][None], od_w_o[0].astype(BF16), wr1, br1)
    return _moe_block(h2e1, cr1, cnt1, xmid1, mods.spec(1, 5, True), mods.table,
                      moe_w1[1], moe_w3[1], moe_w2[1])
```

```python
import functools
import itertools

import numpy as np
import jax
import jax.numpy as jnp
from jax import lax
from jax.experimental import pallas as pl
from jax.experimental.pallas import tpu as pltpu

F32 = jnp.float32
BF16 = jnp.bfloat16
I32 = jnp.int32
U32 = jnp.uint32
HIGHEST = lax.Precision.HIGHEST

EPS = 1e-6
GRID_W = 64
LRU_BLOCKS = 8
LRU_C = 8.0
GLA_HEADS = 4
GLA_DK = 64
GLA_DV = 128
GLA_KEY = GLA_HEADS * GLA_DK
GLA_VAL = GLA_HEADS * GLA_DV
GLA_RANK = 16
GLA_TAU = 16.0
GLA_CHUNK = 64
_CHUNK_SHIFT = GLA_CHUNK.bit_length() - 1
_DK_SHIFT = GLA_DK.bit_length() - 1
ATT_HEADS = 8
KV_HEADS = 2
Q_PER_KV = ATT_HEADS // KV_HEADS
HEAD_DIM = 128
ROPE_THETA = 10000.0
N_GROUPS = 4
EXPERTS_PER_GROUP = 4
N_EXPERTS = N_GROUPS * EXPERTS_PER_GROUP
PAIRS_PER_GROUP = 6
N_CLASSES = N_GROUPS * PAIRS_PER_GROUP

LANES = 128
SUBLANES = 8
TOK = 256
MOE_TILE = 512
ATT_Q = 256
PERM_ROWS = 512
SEQ_ROWS = 4
WEXT = LANES
ROUTER_ROWS = 32
VMEM_LIMIT = 56 * 1024 * 1024


def _cparams(sem):
    return pltpu.CompilerParams(dimension_semantics=sem, vmem_limit_bytes=VMEM_LIMIT)


def _rms(x, g):
    return x * lax.rsqrt(jnp.mean(x * x, axis=-1, keepdims=True) + EPS) * g


def _sigmoid(x):
    return 1.0 / (1.0 + jnp.exp(-x))


def _silu(x):
    return x * _sigmoid(x)


def _gelu_tanh(x):
    return 0.5 * x * (1.0 + jnp.tanh(np.sqrt(2.0 / np.pi).astype(np.float32) * (x + 0.044715 * (x * x * x))))


def _pack_bf16_pairs(xb):
    k = xb.shape[-1] // 2
    lo = lax.shift_right_logical(pltpu.bitcast(xb[:, :k].astype(F32), U32), jnp.uint32(16))
    hi = pltpu.bitcast(xb[:, k:].astype(F32), U32) & jnp.uint32(0xFFFF0000)
    return hi | lo


def _unpack_bf16_pairs(words):
    lo = pltpu.bitcast(lax.shift_left(words, jnp.uint32(16)), F32).astype(BF16)
    hi = pltpu.bitcast(words & jnp.uint32(0xFFFF0000), F32).astype(BF16)
    return jnp.concatenate([lo, hi], axis=1)


def _softplus(x):
    return jnp.maximum(x, 0.0) + jnp.log(1.0 + jnp.exp(-jnp.abs(x)))


def _log_sigmoid(x):
    return -_softplus(-x)


def _adaln_kernel(cv_ref, w_ref, b_ref, o_ref):
    s = _silu(cv_ref[...])
    o_ref[0] = jnp.dot(s, w_ref[0], precision=HIGHEST, preferred_element_type=F32) + b_ref[0]


def _adaln(cv, w_ada, b_ada):
    depth, d, n6 = w_ada.shape
    rows = cv.shape[0]
    tn = 6 * d // 4
    return pl.pallas_call(
        _adaln_kernel,
        grid=(depth, n6 // tn),
        in_specs=[pl.BlockSpec((rows, d), lambda l, j: (0, 0)),
                  pl.BlockSpec((1, d, tn), lambda l, j: (l, 0, j)),
                  pl.BlockSpec((1, 1, tn), lambda l, j: (l, 0, j))],
        out_specs=pl.BlockSpec((1, rows, tn), lambda l, j: (l, 0, j)),
        out_shape=jax.ShapeDtypeStruct((depth, rows, n6), F32),
        compiler_params=_cparams(("arbitrary", "arbitrary")),
        name="adaln",
    )(cv, w_ada, b_ada.reshape(depth, 1, n6))


class _Mods:
    def __init__(self, table, rows, batch, ncc):
        self.table, self.rows, self.batch, self.ncc = table, rows, batch, ncc
        self.d = table.shape[-1]

    def spec(self, layer, j, latent_only=False):
        rows, batch, ncc = self.rows, self.batch, self.ncc

        def imap(b, c):
            r = b if latent_only else jnp.where(c < ncc, batch, b)
            return ((layer * rows + r) * 6 + j, 0, 0)

        return pl.BlockSpec((1, 1, self.d), imap)


def _inproj_kernel(ncc, nch, lw, x_ref, c_ref, xp_ref, xn_ref, cp_ref, cn_ref, g_ref, sh_ref, sc_ref,
                   w_ref, wlr_ref, cw_ref, cb_ref, pm_ref, lr_ref):
    c = pl.program_id(1)
    is_ctx = c < ncc

    def modulated(rows):
        return (_rms(rows, g_ref[...]) * (1.0 + sc_ref[0]) + sh_ref[0]).astype(BF16)

    hb = modulated(jnp.where(is_ctx, c_ref[0], x_ref[0]))
    pm = jnp.dot(hb, w_ref[...], preferred_element_type=F32)
    lr_ref[0] = jnp.dot(hb, wlr_ref[...], preferred_element_type=F32).astype(BF16)

    halo = jnp.concatenate([jnp.where(is_ctx, cp_ref[0], xp_ref[0]), jnp.where(is_ctx, cn_ref[0], xn_ref[0])], axis=0)
    xa_halo = jnp.dot(modulated(halo), w_ref[:, :lw], preferred_element_type=F32)
    has_prev = jnp.logical_and(c != 0, c != ncc)
    has_next = jnp.logical_and(c != ncc - 1, c != nch - 1)
    p2 = jnp.where(has_prev, xa_halo[SUBLANES - 2:SUBLANES - 1], 0.0)
    p1 = jnp.where(has_prev, xa_halo[SUBLANES - 1:SUBLANES], 0.0)
    n0 = jnp.where(has_next, xa_halo[SUBLANES:SUBLANES + 1], 0.0)
    xa = pm[:, :lw]
    row = lax.broadcasted_iota(I32, xa.shape, 0)
    x_m1 = jnp.where(row == 0, p1, pltpu.roll(xa, 1, 0))
    x_m2 = jnp.where(row == 0, p2, jnp.where(row == 1, p1, pltpu.roll(xa, 2, 0)))
    x_p1 = jnp.where(row == TOK - 1, n0, pltpu.roll(xa, TOK - 1, 0))
    cw = cw_ref[...]
    u = cw[0:1] * x_m2 + cw[1:2] * x_m1 + cw[2:3] * xa + cw[3:4] * x_p1 + cb_ref[...]
    pm_ref[0, :, :lw] = u.astype(BF16)
    pm_ref[0, :, lw:] = pm[:, lw:].astype(BF16)


def _inproj(x, ctx, gain, mods, w_main, w_lr, conv_w, conv_b, lw):
    batch, seq, d = x.shape
    ncc = ctx.shape[1] // TOK
    nl = seq // TOK
    nch = ncc + nl
    nm = w_main.shape[1]
    hb = TOK // SUBLANES
    return pl.pallas_call(
        functools.partial(_inproj_kernel, ncc, nch, lw),
        grid=(batch, nch),
        in_specs=[pl.BlockSpec((1, TOK, d), lambda b, c: (b, jnp.maximum(c - ncc, 0), 0)),
                  pl.BlockSpec((1, TOK, d), lambda b, c: (b, jnp.minimum(c, ncc - 1), 0)),
                  pl.BlockSpec((1, SUBLANES, d), lambda b, c: (b, jnp.maximum((c - ncc) * hb - 1, 0), 0)),
                  pl.BlockSpec((1, SUBLANES, d),
                               lambda b, c: (b, jnp.clip((c - ncc + 1) * hb, 0, nl * hb - 1), 0)),
                  pl.BlockSpec((1, SUBLANES, d), lambda b, c: (b, jnp.clip(c * hb - 1, 0, ncc * hb - 1), 0)),
                  pl.BlockSpec((1, SUBLANES, d), lambda b, c: (b, jnp.clip((c + 1) * hb, 0, ncc * hb - 1), 0)),
                  pl.BlockSpec((1, d), lambda b, c: (0, 0)),
                  mods.spec(0, 0), mods.spec(0, 1),
                  pl.BlockSpec((d, nm), lambda b, c: (0, 0)),
                  pl.BlockSpec((d, LANES), lambda b, c: (0, 0)),
                  pl.BlockSpec((4, lw), lambda b, c: (0, 0)),
                  pl.BlockSpec((1, lw), lambda b, c: (0, 0))],
        out_specs=[pl.BlockSpec((1, TOK, nm), lambda b, c: (b, c, 0)),
                   pl.BlockSpec((1, TOK, LANES), lambda b, c: (b, c, 0))],
        out_shape=[jax.ShapeDtypeStruct((batch, nch * TOK, nm), BF16),
                   jax.ShapeDtypeStruct((batch, nch * TOK, LANES), BF16)],
        compiler_params=_cparams(("arbitrary", "arbitrary")),
        name="inproj",
    )(x, ctx, x, x, ctx, ctx, gain, mods.table, mods.table, w_main, w_lr, conv_w, conv_b)


def _lru_scan(a, b, h0, rev):
    n_groups = TOK // SUBLANES
    a = a.reshape(n_groups, SUBLANES, a.shape[-1])
    b = b.reshape(a.shape)
    sub = lax.broadcasted_iota(I32, a.shape, 1)
    for dist in (1, 2, 4):
        shift = SUBLANES - dist if rev else dist
        a_s = pltpu.roll(a, shift, 1)
        b_s = pltpu.roll(b, shift, 1)
        m = (sub < SUBLANES - dist) if rev else (sub >= dist)
        b = jnp.where(m, a * b_s + b, b)
        a = jnp.where(m, a * a_s, a)
    order = range(n_groups - 1, -1, -1) if rev else range(n_groups)
    outs = [None] * n_groups
    h = h0
    for r in order:
        hr = a[r] * h + b[r]
        outs[r] = hr
        h = hr[0:1] if rev else hr[SUBLANES - 1:SUBLANES]
    return jnp.concatenate(outs, axis=0), h


def _seqmix_kernel(rev, *refs):
    h_scr, s_scr = refs[-2:]

    @pl.when(pl.program_id(1) == 0)
    def _():
        h_scr[...] = jnp.zeros_like(h_scr)
        s_scr[...] = jnp.zeros_like(s_scr)

    rows = [_seqmix_row(rev, i, *refs) for i in range(refs[0].shape[0])]
    for _ in itertools.zip_longest(*rows):
        pass


def _seqmix_row(rev, i, u_ref, q_ref, k_ref, v_ref, lr_ref,
                wgt_ref, bgt_ref, lam_ref, wg_ref, bg_ref, sel_ref,
                lru_ref, gla_ref, h_scr, s_scr):
    lw = u_ref.shape[-1]
    ub = u_ref[i]
    u = ub.astype(F32)
    gates = jnp.dot(ub, wgt_ref[...], preferred_element_type=F32) + bgt_ref[...]
    yield
    r_gate = _sigmoid(gates[:, :lw])
    i_gate = _sigmoid(gates[:, lw:])
    log_a = (-LRU_C) * r_gate * _softplus(-lam_ref[...])
    a = jnp.exp(log_a)
    bb = jnp.sqrt(1.0 - a * a) * (i_gate * u)
    hs, h_last = _lru_scan(a, bb, h_scr[i, 0:1], rev)
    h_scr[i, 0:1] = h_last
    lru_ref[i] = hs.astype(BF16)
    yield

    lg = _log_sigmoid(jnp.dot(lr_ref[i], wg_ref[...], preferred_element_type=F32) + bg_ref[...]) * (1.0 / GLA_TAU)
    lg_hi = lg.astype(BF16)
    lg_lo = (lg - lg_hi.astype(F32)).astype(BF16)
    yield
    cum = jnp.dot(sel_ref[...], jnp.concatenate([lg_hi, lg_lo], axis=1), preferred_element_type=F32)
    yield
    g = cum[:TOK, :GLA_KEY] + cum[:TOK, GLA_KEY:]
    tot = cum[TOK:, :GLA_KEY] + cum[TOK:, GLA_KEY:]
    q = q_ref[i].astype(F32)
    k = k_ref[i].astype(F32)
    dec_all = jnp.exp(tot)
    k_neg = k * jnp.exp(-g)
    qg = (q * jnp.exp(g) * (GLA_DK ** -0.5)).astype(BF16)
    kg = k_neg.astype(BF16)
    kd = (k_neg * dec_all).astype(BF16)
    v = v_ref[i]

    lane = lax.broadcasted_iota(I32, (1, GLA_KEY), 1) >> _DK_SHIFT
    head_masks = [(lane == h) for h in range(GLA_HEADS)]
    n_stack = GLA_HEADS * GLA_CHUNK
    ai = lax.broadcasted_iota(I32, (n_stack, GLA_CHUNK), 0) & (GLA_CHUNK - 1)
    aj = lax.broadcasted_iota(I32, (n_stack, GLA_CHUNK), 1)
    causal = (aj >= ai) if rev else (aj <= ai)

    n_sub = TOK // GLA_CHUNK
    order = range(n_sub - 1, -1, -1) if rev else range(n_sub)
    outs = [None] * n_sub
    st = s_scr[i]
    for n in order:
        sl = slice(n * GLA_CHUNK, (n + 1) * GLA_CHUNK)
        qn = qg[sl]
        qs = jnp.concatenate([jnp.where(head_masks[h], qn, jnp.zeros_like(qn)) for h in range(GLA_HEADS)], axis=0)
        att = lax.dot_general(qs, kg[sl], (((1,), (1,)), ((), ())), preferred_element_type=F32)
        yield
        att = jnp.where(causal, att, 0.0).astype(BF16)
        o_all = jnp.dot(att, v[sl], preferred_element_type=F32)
        yield
        o_int = lax.dot_general(qs, st.astype(BF16), (((1,), (1,)), ((), ())),
                                preferred_element_type=F32)
        yield
        outs[n] = jnp.concatenate(
            [o_all[h * GLA_CHUNK:(h + 1) * GLA_CHUNK, h * GLA_DV:(h + 1) * GLA_DV]
             + o_int[h * GLA_CHUNK:(h + 1) * GLA_CHUNK] for h in range(GLA_HEADS)], axis=1)
        vn, kdn = v[sl], kd[sl]
        v_stack = jnp.concatenate([vn[:, h * GLA_DV:(h + 1) * GLA_DV] for h in range(GLA_HEADS)], axis=0)
        kd_stack = jnp.concatenate([jnp.where(head_masks[h], kdn, jnp.zeros_like(kdn)) for h in range(GLA_HEADS)],
                                   axis=0)
        ds = lax.dot_general(v_stack, kd_stack, (((0,), (0,)), ((), ())), preferred_element_type=F32)
        yield
        st = dec_all[n * GLA_CHUNK:n * GLA_CHUNK + 1] * st + ds
    s_scr[i] = st
    gla_ref[i] = jnp.concatenate(outs, axis=0).astype(BF16)


def _cumsum_selector(rev):
    r = np.arange(TOK)[:, None]
    c = np.arange(TOK)[None, :]
    same = (r // GLA_CHUNK) == (c // GLA_CHUNK)
    tri = same & ((c >= r) if rev else (c <= r))
    return jnp.asarray(np.concatenate([tri, same], axis=0), BF16)


def _seqmix(rev, ncc, pm, lr, w_gate, b_gate, lam, wg_pad, bg, lw):
    batch, t, _ = pm.shape
    nch = t // TOK
    nl = nch - ncc

    def chunk(s):
        if rev:
            return jnp.where(s < ncc, ncc - 1 - s, ncc + (nl - 1) - (s - ncc))
        return s

    qcol = 2 * lw // GLA_KEY
    vcol = (2 * lw + 2 * GLA_KEY) // GLA_VAL
    const = lambda shape: pl.BlockSpec(shape, lambda b, s: (0,) * len(shape))
    nb = SEQ_ROWS if batch % SEQ_ROWS == 0 else 1
    return pl.pallas_call(
        functools.partial(_seqmix_kernel, rev),
        grid=(batch // nb, nch),
        in_specs=[pl.BlockSpec((nb, TOK, lw), lambda b, s: (b, chunk(s), 0)),
                  pl.BlockSpec((nb, TOK, GLA_KEY), lambda b, s: (b, chunk(s), qcol)),
                  pl.BlockSpec((nb, TOK, GLA_KEY), lambda b, s: (b, chunk(s), qcol + 1)),
                  pl.BlockSpec((nb, TOK, GLA_VAL), lambda b, s: (b, chunk(s), vcol)),
                  pl.BlockSpec((nb, TOK, LANES), lambda b, s: (b, chunk(s), 0)),
                  const((lw, 2 * lw)), const((1, 2 * lw)), const((1, lw)),
                  const((LANES, GLA_KEY)), const((1, GLA_KEY)), const((2 * TOK, TOK))],
        out_specs=[pl.BlockSpec((nb, TOK, lw), lambda b, s: (b, chunk(s), 0)),
                   pl.BlockSpec((nb, TOK, GLA_VAL), lambda b, s: (b, chunk(s), 0))],
        out_shape=[jax.ShapeDtypeStruct((batch, t, lw), BF16),
                   jax.ShapeDtypeStruct((batch, t, GLA_VAL), BF16)],
        scratch_shapes=[pltpu.VMEM((nb, SUBLANES, lw), F32), pltpu.VMEM((nb, GLA_DV, GLA_KEY), F32)],
        compiler_params=_cparams(("arbitrary", "arbitrary")),
        name="seqmix_rev" if rev else "seqmix_fwd",
    )(pm, pm, pm, pm, lr, w_gate, b_gate, lam, wg_pad, bg, _cumsum_selector(rev))


def _route_epilogue(first, xm, n2_ref, sh2_ref, sc2_ref, wr_ref, br_ref,
                    xmid_ref, h2_ref, cr_ref, cnt_ref, carry):
    d = xm.shape[-1]
    xmid_ref[0] = xm
    h2 = _rms(xm, n2_ref[...]) * (1.0 + sc2_ref[0]) + sh2_ref[0]
    h_hi = h2.astype(BF16)
    h_lo = (h2 - h_hi.astype(F32)).astype(BF16)
    nt = (((1,), (1,)), ((), ()))
    o1 = lax.dot_general(wr_ref[...], h_hi, nt, preferred_element_type=F32)
    o2 = lax.dot_general(wr_ref[:ROUTER_ROWS], h_lo, nt, preferred_element_type=F32)
    logits = o1[:ROUTER_ROWS] + o1[ROUTER_ROWS:] + o2 + br_ref[...]

    def col(j):
        return logits[j:j + 1]

    lgs = [col(g) for g in range(N_GROUPS)]
    gmax = functools.reduce(jnp.maximum, lgs)
    gi = jnp.where(lgs[0] == gmax, 0, jnp.where(lgs[1] == gmax, 1, jnp.where(lgs[2] == gmax, 2, 3)))
    w_group = 1.0 / functools.reduce(lambda p, q: p + q, [jnp.exp(l - gmax) for l in lgs])
    es = []
    for j in range(EXPERTS_PER_GROUP):
        acc = jnp.zeros_like(gmax)
        for g in range(N_GROUPS):
            acc = acc + jnp.where(gi == g, col(N_GROUPS + g * EXPERTS_PER_GROUP + j), 0.0)
        es.append(acc)
    m1 = functools.reduce(jnp.maximum, es)
    i1 = jnp.where(es[0] == m1, 0, jnp.where(es[1] == m1, 1, jnp.where(es[2] == m1, 2, 3)))
    rest = [jnp.where(i1 == j, -jnp.inf, es[j]) for j in range(EXPERTS_PER_GROUP)]
    m2 = functools.reduce(jnp.maximum, rest)
    i2 = jnp.where(rest[0] == m2, 0, jnp.where(rest[1] == m2, 1, jnp.where(rest[2] == m2, 2, 3)))
    e2 = jnp.exp(m2 - m1)
    w1 = w_group / (1.0 + e2)
    w2 = w_group * e2 / (1.0 + e2)
    first_lo = i1 < i2
    lo = jnp.where(first_lo, i1, i2)
    hi = jnp.where(first_lo, i2, i1)
    w_lo = jnp.where(first_lo, w1, w2)
    w_hi = jnp.where(first_lo, w2, w1)
    pidx = jnp.where(lo == 0, hi - 1, jnp.where(lo == 1, hi + 1, 5))
    cls = gi * PAIRS_PER_GROUP + pidx

    def terms(w):
        t1 = w.astype(BF16).astype(F32)
        t2 = (w - t1).astype(BF16).astype(F32)
        t3 = (w - t1 - t2).astype(BF16).astype(F32)
        return [t1, t2, t3]

    sub = lax.broadcasted_iota(I32, (2 * SUBLANES, TOK), 0)
    stacked = jnp.zeros((2 * SUBLANES, TOK), F32)
    for r, term in enumerate(terms(w_lo) + terms(w_hi)):
        stacked = jnp.where(sub == r, term, stacked)
    er = lax.broadcasted_iota(I32, (2 * SUBLANES, LANES), 0)
    el = lax.broadcasted_iota(I32, (2 * SUBLANES, LANES), 1)
    place = jnp.logical_or(jnp.logical_and(er < 3, el == 0),
                           jnp.logical_and(jnp.logical_and(er >= 3, er < 6), el == 1))
    w_cols = lax.dot_general(stacked.astype(BF16), place.astype(BF16), (((0,), (0,)), ((), ())),
                             preferred_element_type=F32)
    h2_ref[:, :d // 2] = _pack_bf16_pairs(h_hi)
    h2_ref[:, d // 2:] = pltpu.bitcast(w_cols, U32)

    @pl.when(first)
    def _():
        carry[...] = jnp.zeros_like(carry)

    krow = lax.broadcasted_iota(I32, (ROUTER_ROWS, TOK), 0)
    onehot = krow == cls
    ri = lax.broadcasted_iota(I32, (TOK, TOK), 0)
    ci = lax.broadcasted_iota(I32, (TOK, TOK), 1)
    triu = (ri <= ci).astype(BF16)
    pref = jnp.dot(onehot.astype(BF16), triu, preferred_element_type=F32)
    base = carry[:, 0:1]
    rank = jnp.sum(jnp.where(onehot, pref - 1.0 + base, 0.0), axis=0, keepdims=True)
    new_carry = jnp.broadcast_to(base + jnp.sum(onehot.astype(F32), axis=1, keepdims=True), carry.shape)
    carry[...] = new_carry
    cnt_ref[...] = new_carry
    sub8 = lax.broadcasted_iota(I32, (SUBLANES, TOK), 0)
    cr_ref[0] = jnp.where(sub8 == 0, cls, jnp.where(sub8 == 1, rank.astype(I32), 0))


def _post0_kernel(ncc, l0_ref, l1_ref, g0_ref, g1_ref, ga_ref, r_ref, x_ref, c_ref,
                  gate_ref, sh2_ref, sc2_ref, n2_ref, gn_ref, wout_ref, wr_ref, br_ref,
                  xmid_ref, h2_ref, cr_ref, cnt_ref, carry):
    b, c = pl.program_id(0), pl.program_id(1)
    lru = l0_ref[0].astype(F32) + l1_ref[0].astype(F32)
    ya = lru * _gelu_tanh(ga_ref[0].astype(F32))
    gla = g0_ref[0].astype(F32) + g1_ref[0].astype(F32)
    parts = []
    for h in range(GLA_HEADS):
        parts.append(_rms(gla[:, h * GLA_DV:(h + 1) * GLA_DV], gn_ref[...]))
    yb = jnp.concatenate(parts, axis=1) * _silu(r_ref[0].astype(F32))
    ycat = jnp.concatenate([ya, yb], axis=1).astype(BF16)
    y = jnp.dot(ycat, wout_ref[...], preferred_element_type=F32)
    x0 = jnp.where(c < ncc, c_ref[0], x_ref[0])
    xm = x0 + gate_ref[0] * y
    _route_epilogue(jnp.logical_and(b == 0, c == 0), xm, n2_ref, sh2_ref, sc2_ref, wr_ref, br_ref,
                    xmid_ref, h2_ref, cr_ref, cnt_ref, carry)


def _post1_kernel(o_ref, x_ref, gate_ref, sh2_ref, sc2_ref, n2_ref, wo_ref, wr_ref, br_ref,
                  xmid_ref, h2_ref, cr_ref, cnt_ref, carry):
    b, c = pl.program_id(0), pl.program_id(1)
    y = jnp.dot(o_ref[0], wo_ref[...], preferred_element_type=F32)
    xm = x_ref[0] + gate_ref[0] * y
    _route_epilogue(jnp.logical_and(b == 0, c == 0), xm, n2_ref, sh2_ref, sc2_ref, wr_ref, br_ref,
                    xmid_ref, h2_ref, cr_ref, cnt_ref, carry)


def _route_out(batch, nch, d):
    n = batch * nch * TOK
    specs = [pl.BlockSpec((1, TOK, d), lambda b, c: (b, c, 0)),
             pl.BlockSpec((TOK, d // 2 + WEXT), lambda b, c: (b * nch + c, 0)),
             pl.BlockSpec((1, SUBLANES, TOK), lambda b, c: (b * nch + c, 0, 0)),
             pl.BlockSpec((ROUTER_ROWS, LANES), lambda b, c: (0, 0))]
    shapes = [jax.ShapeDtypeStruct((batch, nch * TOK, d), F32),
              jax.ShapeDtypeStruct((n, d // 2 + WEXT), U32),
              jax.ShapeDtypeStruct((batch * nch, SUBLANES, TOK), I32),
              jax.ShapeDtypeStruct((ROUTER_ROWS, LANES), F32)]
    return specs, shapes


def _post0(ncc, lru_f, lru_r, gla_f, gla_r, pm, x, ctx, mods, n2, gn, w_out, wr, br, lw):
    batch, t, _ = pm.shape
    d = x.shape[-1]
    nch = t // TOK
    rcol = (2 * lw + 2 * GLA_KEY + GLA_VAL) // GLA_VAL
    tokspec = lambda w: pl.BlockSpec((1, TOK, w), lambda b, c: (b, c, 0))
    const = lambda shape: pl.BlockSpec(shape, lambda b, c: (0,) * len(shape))
    out_specs, out_shape = _route_out(batch, nch, d)
    return pl.pallas_call(
        functools.partial(_post0_kernel, ncc),
        grid=(batch, nch),
        in_specs=[tokspec(lw), tokspec(lw), tokspec(GLA_VAL), tokspec(GLA_VAL),
                  pl.BlockSpec((1, TOK, lw), lambda b, c: (b, c, 1)),
                  pl.BlockSpec((1, TOK, GLA_VAL), lambda b, c: (b, c, rcol)),
                  pl.BlockSpec((1, TOK, d), lambda b, c: (b, jnp.maximum(c - ncc, 0), 0)),
                  pl.BlockSpec((1, TOK, d), lambda b, c: (b, jnp.minimum(c, ncc - 1), 0)),
                  mods.spec(0, 2), mods.spec(0, 3), mods.spec(0, 4),
                  const((1, d)), const((1, GLA_DV)), const((lw + GLA_VAL, d)),
                  const((2 * ROUTER_ROWS, d)), const((ROUTER_ROWS, 1))],
        out_specs=out_specs, out_shape=out_shape,
        scratch_shapes=[pltpu.VMEM((ROUTER_ROWS, LANES), F32)],
        compiler_params=_cparams(("arbitrary", "arbitrary")),
        name="post0",
    )(lru_f, lru_r, gla_f, gla_r, pm, pm, x, ctx, mods.table, mods.table, mods.table, n2, gn, w_out, wr, br)


def _post1(ncc, att, x1, mods, n2, w_o, wr, br):
    batch, seq, aw = att.shape
    d = x1.shape[-1]
    nl = seq // TOK
    const = lambda shape: pl.BlockSpec(shape, lambda b, c: (0,) * len(shape))
    out_specs, out_shape = _route_out(batch, nl, d)
    return pl.pallas_call(
        _post1_kernel,
        grid=(batch, nl),
        in_specs=[pl.BlockSpec((1, TOK, aw), lambda b, c: (b, c, 0)),
                  pl.BlockSpec((1, TOK, d), lambda b, c: (b, c + ncc, 0)),
                  mods.spec(1, 2, True), mods.spec(1, 3, True), mods.spec(1, 4, True),
                  const((1, d)), const((aw, d)), const((2 * ROUTER_ROWS, d)), const((ROUTER_ROWS, 1))],
        out_specs=out_specs, out_shape=out_shape,
        scratch_shapes=[pltpu.VMEM((ROUTER_ROWS, LANES), F32)],
        compiler_params=_cparams(("arbitrary", "arbitrary")),
        name="post1",
    )(att, x1, mods.table, mods.table, mods.table, n2, w_o, wr, br)


_PAIR_LO = np.array([0, 0, 0, 1, 1, 2], np.int32)
_PAIR_HI = np.array([1, 2, 3, 2, 3, 3], np.int32)


def _plan(cr, cnt, n_tiles):
    cls, rank = cr[:, 0, :].reshape(-1), cr[:, 1, :].reshape(-1)
    counts = cnt[:N_CLASSES, 0].astype(I32)
    tiles = (counts + MOE_TILE - 1) // MOE_TILE
    ends = jnp.cumsum(tiles)
    offs = (ends - tiles) * MOE_TILE
    classes = jnp.arange(N_CLASSES, dtype=I32)
    pos = jnp.sum(jnp.where(cls[:, None] == classes[None, :], offs[None, :], 0), axis=1) + rank
    tile_ids = jnp.arange(n_tiles, dtype=I32)
    tile_cls = jnp.minimum(jnp.sum((tile_ids[:, None] >= ends[None, :]).astype(I32), axis=1), N_CLASSES - 1)
    grp = tile_cls // PAIRS_PER_GROUP
    pair = tile_cls % PAIRS_PER_GROUP
    lo = grp * EXPERTS_PER_GROUP + jnp.asarray(_PAIR_LO)[pair]
    hi = grp * EXPERTS_PER_GROUP + jnp.asarray(_PAIR_HI)[pair]
    return pos.astype(I32), lo.astype(I32), hi.astype(I32)


def _moe_kernel(d, lo_ref, hi_ref, x_ref, w1l, w3l, w2l, w1h, w3h, w2h, y_ref, wbuf, wbuf2):
    i = pl.program_id(0)

    changed = jnp.logical_or(i == 0, jnp.logical_or(lo_ref[i] != lo_ref[jnp.maximum(i - 1, 0)],
                                                    hi_ref[i] != hi_ref[jnp.maximum(i - 1, 0)]))

    @pl.when(changed)
    def _():
        for k, w in enumerate((w1l, w3l, w1h, w3h)):
            wbuf[k] = w[0].astype(BF16)
        wbuf2[0] = w2l[0].astype(BF16)
        wbuf2[1] = w2h[0].astype(BF16)

    x = _unpack_bf16_pairs(x_ref[:, :d // 2])
    wts = pltpu.bitcast(x_ref[:, d // 2:], F32)

    def expert(k1, k3, k2):
        a = jnp.dot(x, wbuf[k1], preferred_element_type=F32)
        b = jnp.dot(x, wbuf[k3], preferred_element_type=F32)
        return jnp.dot((_silu(a) * b).astype(BF16), wbuf2[k2], preferred_element_type=F32)

    y = wts[:, 0:1] * expert(0, 1, 0) + wts[:, 1:2] * expert(2, 3, 1)
    y_ref[...] = _pack_bf16_pairs(y.astype(BF16))


def _moe(xs, lo, hi, w1, w3, w2):
    n_pad, wd = xs.shape
    d = 2 * (wd - WEXT)
    hdim = w1.shape[-1]
    n_tiles = n_pad // MOE_TILE
    wl = lambda i, lo, hi: (lo[i], 0, 0)
    wh = lambda i, lo, hi: (hi[i], 0, 0)
    grid_spec = pltpu.PrefetchScalarGridSpec(
        num_scalar_prefetch=2,
        grid=(n_tiles,),
        in_specs=[pl.BlockSpec((MOE_TILE, wd), lambda i, lo, hi: (i, 0)),
                  pl.BlockSpec((1, d, hdim), wl), pl.BlockSpec((1, d, hdim), wl), pl.BlockSpec((1, hdim, d), wl),
                  pl.BlockSpec((1, d, hdim), wh), pl.BlockSpec((1, d, hdim), wh), pl.BlockSpec((1, hdim, d), wh)],
        out_specs=pl.BlockSpec((MOE_TILE, d // 2), lambda i, lo, hi: (i, 0)),
        scratch_shapes=[pltpu.VMEM((4, d, hdim), BF16), pltpu.VMEM((2, hdim, d), BF16)],
    )
    return pl.pallas_call(
        functools.partial(_moe_kernel, d),
        grid_spec=grid_spec,
        out_shape=jax.ShapeDtypeStruct((n_pad, d // 2), U32),
        compiler_params=_cparams(("arbitrary",)),
        name="moe_experts",
    )(lo, hi, xs, w1, w3, w2, w1, w3, w2)


def _row_dma_loops(tile_rows, make_copy, index, alternate_priority=False):
    def issue(grp, carry):
        for sub in range(SUBLANES):
            make_copy(grp, sub, index(grp * SUBLANES + sub)).start(priority=sub % 2 if alternate_priority else 0)
        return carry

    def drain(grp, carry):
        for _ in range(SUBLANES):
            make_copy(0, 0, 0).wait()
        return carry

    n_groups = tile_rows // SUBLANES
    return (lambda: lax.fori_loop(0, n_groups, issue, 0)), (lambda: lax.fori_loop(0, n_groups, drain, 0))


def _scatter_rows_kernel(rows, pos_ref, src_ref, init_ref, dst_ref, sem):
    del init_ref
    issue, drain = _row_dma_loops(
        rows,
        lambda grp, sub, p: pltpu.make_async_copy(src_ref.at[grp, pl.ds(sub, 1)], dst_ref.at[pl.ds(p, 1)], sem),
        lambda j: pos_ref[0, 0, j])
    issue()
    drain()


def _scatter_rows(src, pos, n_pad):
    n, w = src.shape
    rows = PERM_ROWS if n % PERM_ROWS == 0 else TOK
    steps = n // rows
    return pl.pallas_call(
        functools.partial(_scatter_rows_kernel, rows),
        grid=(steps,),
        in_specs=[pl.BlockSpec((1, 1, rows), lambda i: (i, 0, 0), memory_space=pltpu.SMEM),
                  pl.BlockSpec((rows // SUBLANES, SUBLANES, w), lambda i: (i, 0, 0)),
                  pl.BlockSpec(memory_space=pl.ANY)],
        out_specs=pl.BlockSpec(memory_space=pl.ANY),
        out_shape=jax.ShapeDtypeStruct((n_pad, w), src.dtype),
        scratch_shapes=[pltpu.SemaphoreType.DMA(())],
        input_output_aliases={2: 0},
        compiler_params=pltpu.CompilerParams(dimension_semantics=("arbitrary",), has_side_effects=True),
        name="scatter_rows",
    )(pos.reshape(steps, 1, rows), src.reshape(n // SUBLANES, SUBLANES, w), jnp.zeros((n_pad, w), src.dtype))


def _gather_residual_kernel(n_steps, pos_ref, posn_ref, ys_ref, xm_ref, gate_ref, o_ref, buf, sem):
    step = pl.program_id(0) * pl.num_programs(1) + pl.program_id(1)
    slot = step % 2

    def loops(idx_ref, s):
        return _row_dma_loops(
            TOK,
            lambda grp, sub, p: pltpu.make_async_copy(ys_ref.at[pl.ds(p, 1)], buf.at[s, grp, pl.ds(sub, 1)],
                                                      sem.at[s]),
            lambda j: idx_ref[0, 0, j], alternate_priority=True)

    @pl.when(step == 0)
    def _():
        loops(pos_ref, slot)[0]()

    @pl.when(step + 1 < n_steps)
    def _():
        loops(posn_ref, 1 - slot)[0]()

    loops(pos_ref, slot)[1]()
    y = _unpack_bf16_pairs(buf[slot].reshape(TOK, buf.shape[-1])).astype(F32)
    o_ref[0] = xm_ref[0] + gate_ref[0] * y


def _gather_residual(ys, pos, xmid, gate_spec, mod_table):
    batch, t, d = xmid.shape
    nch = t // TOK
    n_steps = batch * nch
    return pl.pallas_call(
        functools.partial(_gather_residual_kernel, n_steps),
        grid=(batch, nch),
        in_specs=[pl.BlockSpec((1, 1, TOK), lambda b, c: (b * nch + c, 0, 0), memory_space=pltpu.SMEM),
                  pl.BlockSpec((1, 1, TOK), lambda b, c: (jnp.minimum(b * nch + c + 1, n_steps - 1), 0, 0),
                               memory_space=pltpu.SMEM),
                  pl.BlockSpec(memory_space=pl.ANY),
                  pl.BlockSpec((1, TOK, d), lambda b, c: (b, c, 0)),
                  gate_spec],
        out_specs=pl.BlockSpec((1, TOK, d), lambda b, c: (b, c, 0)),
        out_shape=jax.ShapeDtypeStruct((batch, t, d), F32),
        scratch_shapes=[pltpu.VMEM((2, TOK // SUBLANES, SUBLANES, d // 2), U32), pltpu.SemaphoreType.DMA((2,))],
        compiler_params=_cparams(("arbitrary", "arbitrary")),
        name="gather_residual",
    )(pos.reshape(n_steps, 1, TOK), pos.reshape(n_steps, 1, TOK), ys, xmid, mod_table)


def _moe_block(h2ext, cr, cnt, xmid, gate_spec, mod_table, w1, w3, w2):
    n = h2ext.shape[0]
    n_tiles = n // MOE_TILE + N_CLASSES
    pos, lo, hi = _plan(cr, cnt, n_tiles)
    xs = _scatter_rows(h2ext, pos, n_tiles * MOE_TILE)
    ys = _moe(xs, lo, hi, w1, w3, w2)
    return _gather_residual(ys, pos, xmid, gate_spec, mod_table)


def _qkv_kernel(ncc, x_ref, g_ref, sh_ref, sc_ref, w_ref, qn_ref, kn_ref, cos_ref, sin_ref,
                q_ref, k_ref, v_ref):
    c = pl.program_id(1)
    latent = c >= ncc
    h = _rms(x_ref[0], g_ref[...]) * (1.0 + sc_ref[0]) + sh_ref[0]
    hb = h.astype(BF16)
    nq = ATT_HEADS * HEAD_DIM
    nk = KV_HEADS * HEAD_DIM
    cos = cos_ref[...]
    sin = sin_ref[...]

    def head(z, gain):
        z = _rms(z, gain)
        rot = z * cos + pltpu.roll(z, HEAD_DIM // 2, 1) * sin
        return jnp.where(latent, rot, z)

    kv = jnp.dot(hb, w_ref[:, nq:], preferred_element_type=F32)
    k_ref[0] = jnp.concatenate(
        [head(kv[:, i * HEAD_DIM:(i + 1) * HEAD_DIM], kn_ref[...]) for i in range(KV_HEADS)], axis=1).astype(BF16)
    ones = jnp.ones((TOK, HEAD_DIM), BF16)
    v_ref[0] = jnp.concatenate(
        [blk for i in range(KV_HEADS)
         for blk in (kv[:, nk + i * HEAD_DIM:nk + (i + 1) * HEAD_DIM].astype(BF16), ones)], axis=1)

    @pl.when(latent)
    def _():
        qq = jnp.dot(hb, w_ref[:, :nq], preferred_element_type=F32)
        q_ref[0] = jnp.concatenate(
            [head(qq[:, i * HEAD_DIM:(i + 1) * HEAD_DIM], qn_ref[...]) * (HEAD_DIM ** -0.5)
             for i in range(ATT_HEADS)], axis=1).astype(BF16)


def _qkv(ncc, x1, gain, mods, w_qkv, qn, kn, cos, sin):
    batch, t, d = x1.shape
    nch = t // TOK
    seq = t - ncc * TOK
    nq = ATT_HEADS * HEAD_DIM
    nk = KV_HEADS * HEAD_DIM
    const = lambda shape: pl.BlockSpec(shape, lambda b, c: (0,) * len(shape))
    lat = lambda b, c: (b, jnp.maximum(c - ncc, 0), 0)
    return pl.pallas_call(
        functools.partial(_qkv_kernel, ncc),
        grid=(batch, nch),
        in_specs=[pl.BlockSpec((1, TOK, d), lambda b, c: (b, c, 0)),
                  const((1, d)), mods.spec(1, 0), mods.spec(1, 1),
                  const((d, nq + 2 * nk)), const((1, HEAD_DIM)), const((1, HEAD_DIM)),
                  pl.BlockSpec((TOK, HEAD_DIM), lambda b, c: (jnp.maximum(c - ncc, 0), 0)),
                  pl.BlockSpec((TOK, HEAD_DIM), lambda b, c: (jnp.maximum(c - ncc, 0), 0))],
        out_specs=[pl.BlockSpec((1, TOK, nq), lat),
                   pl.BlockSpec((1, TOK, nk), lambda b, c: (b, c, 0)),
                   pl.BlockSpec((1, TOK, 2 * nk), lambda b, c: (b, c, 0))],
        out_shape=[jax.ShapeDtypeStruct((batch, seq, nq), BF16),
                   jax.ShapeDtypeStruct((batch, t, nk), BF16),
                   jax.ShapeDtypeStruct((batch, t, 2 * nk), BF16)],
        compiler_params=_cparams(("arbitrary", "arbitrary")),
        name="qkv",
    )(x1, gain, mods.table, mods.table, w_qkv, qn, kn, cos, sin)


def _attn_kernel(n_kb, q_ref, k_ref, v_ref, o_ref):
    q = q_ref[0]
    qs = jnp.concatenate([q[:, g * HEAD_DIM:(g + 1) * HEAD_DIM] for g in range(Q_PER_KV)], axis=0)
    kb = k_ref.shape[1] // n_kb
    m = acc = None
    for j in range(n_kb):
        s = lax.dot_general(qs, k_ref[0, j * kb:(j + 1) * kb, :], (((1,), (1,)), ((), ())),
                            preferred_element_type=F32)
        m_blk = jnp.max(s, axis=-1, keepdims=True)
        m_new = m_blk if m is None else jnp.maximum(m, m_blk)
        p = jnp.exp((s - m_new).astype(BF16))
        pv = jnp.dot(p, v_ref[0, j * kb:(j + 1) * kb, :], preferred_element_type=F32)
        acc = pv if acc is None else jnp.exp(m - m_new) * acc + pv
        m = m_new
    o = acc[:, :HEAD_DIM] / acc[:, HEAD_DIM:HEAD_DIM + 1]
    o_ref[0] = jnp.concatenate([o[g * ATT_Q:(g + 1) * ATT_Q] for g in range(Q_PER_KV)], axis=1).astype(BF16)


def _attention(q, k, v):
    batch, seq, nq = q.shape
    t = k.shape[1]
    gw = Q_PER_KV * HEAD_DIM
    n_kb = 2 if t % (2 * LANES) == 0 else 1
    return pl.pallas_call(
        functools.partial(_attn_kernel, n_kb),
        grid=(batch, KV_HEADS, seq // ATT_Q),
        in_specs=[pl.BlockSpec((1, ATT_Q, gw), lambda b, h, i: (b, i, h)),
                  pl.BlockSpec((1, t, HEAD_DIM), lambda b, h, i: (b, 0, h)),
                  pl.BlockSpec((1, t, 2 * HEAD_DIM), lambda b, h, i: (b, 0, h))],
        out_specs=pl.BlockSpec((1, ATT_Q, gw), lambda b, h, i: (b, i, h)),
        out_shape=jax.ShapeDtypeStruct((batch, seq, nq), BF16),
        compiler_params=_cparams(("arbitrary", "arbitrary", "arbitrary")),
        name="attention",
    )(q, k, v)


def _block_diag(w):
    nb, bs, _ = w.shape
    eye = jnp.eye(nb, dtype=w.dtype)
    return (eye[:, None, :, None] * w[:, :, None, :]).reshape(nb * bs, nb * bs)


def _router_weights(wg, bg, we, be):
    d = wg.shape[0]
    n = N_GROUPS + N_EXPERTS
    wr = jnp.zeros((ROUTER_ROWS, d), F32).at[:N_GROUPS].set(wg.T).at[N_GROUPS:n].set(we.T)
    br = jnp.zeros((ROUTER_ROWS, 1), F32).at[:N_GROUPS, 0].set(bg).at[N_GROUPS:n, 0].set(be)
    w_hi = wr.astype(BF16)
    w_lo = (wr - w_hi.astype(F32)).astype(BF16)
    return jnp.concatenate([w_hi, w_lo], axis=0), br


def _rope_tables(seq):
    rows = seq // GRID_W
    row = np.repeat(np.arange(rows, dtype=np.float32), GRID_W)
    col = np.tile(np.arange(GRID_W, dtype=np.float32), rows)
    ppa = HEAD_DIM // 4
    freqs = (ROPE_THETA ** (-np.arange(ppa, dtype=np.float32) / ppa)).astype(np.float32)
    ang = np.concatenate([row[:, None] * freqs, col[:, None] * freqs], axis=-1)
    cos, sin = np.cos(ang), np.sin(ang)
    return (jnp.asarray(np.concatenate([cos, cos], axis=-1), F32),
            jnp.asarray(np.concatenate([-sin, sin], axis=-1), F32))


_HALF_SPLIT = np.concatenate([np.arange(0, HEAD_DIM, 2), np.arange(1, HEAD_DIM, 2)])


def kernel(x, c, ctx, c_ctx, norm1, norm2, w_ada, b_ada, ev_w_in, ev_conv_w, ev_conv_b, ev_lru_wa, ev_lru_ba, ev_lru_wi, ev_lru_bi, ev_lru_lam, ev_gla_wg, ev_gla_bg, ev_gla_norm, ev_w_out, od_w_qkv, od_q_norm, od_k_norm, od_w_o, moe_wg, moe_bg, moe_we, moe_be, moe_w1, moe_w3, moe_w2):
    batch, seq, d = x.shape
    ctx_len = ctx.shape[1]
    assert seq % TOK == 0 and ctx_len % TOK == 0 and d % GLA_VAL == 0 and seq % GRID_W == 0
    ncc = ctx_len // TOK
    lw = d // 2

    rows = -(-(batch + 1) // SUBLANES) * SUBLANES
    cv = jnp.zeros((rows, d), F32).at[:batch].set(c).at[batch].set(c_ctx)
    table = _adaln(cv, w_ada, b_ada)
    mods = _Mods(table.reshape(table.shape[0] * rows * 6, 1, d), rows, batch, ncc)

    w_in = ev_w_in[0]
    nm = 2 * lw + 2 * GLA_KEY + 2 * GLA_VAL
    w_main = w_in[:, :nm].astype(BF16)
    w_lr = jnp.zeros((d, LANES), F32).at[:, :2 * GLA_RANK].set(w_in[:, nm:]).astype(BF16)
    pm, lr = _inproj(x, ctx, norm1[0][None], mods, w_main, w_lr, ev_conv_w[0], ev_conv_b[0][None], lw)

    mixed = []
    for dr in range(2):
        w_gate = jnp.concatenate([_block_diag(ev_lru_wa[0, dr]), _block_diag(ev_lru_wi[0, dr])], axis=1).astype(BF16)
        b_gate = jnp.concatenate([ev_lru_ba[0, dr], ev_lru_bi[0, dr]])[None]
        wg_pad = jnp.zeros((LANES, GLA_KEY), F32).at[dr * GLA_RANK:(dr + 1) * GLA_RANK].set(ev_gla_wg[0, dr]).astype(BF16)
        mixed.append(_seqmix(dr == 1, ncc, pm, lr, w_gate, b_gate,
                             ev_lru_lam[0, dr][None], wg_pad, ev_gla_bg[0, dr][None], lw))
    (lru_f, gla_f), (lru_r, gla_r) = mixed

    wr0, br0 = _router_weights(moe_wg[0], moe_bg[0], moe_we[0], moe_be[0])
    xmid0, h2e0, cr0, cnt0 = _post0(ncc, lru_f, lru_r, gla_f, gla_r, pm, x, ctx, mods, norm2[0][None],
                                    ev_gla_norm[0][None], ev_w_out[0].astype(BF16), wr0, br0, lw)
    x1 = _moe_block(h2e0, cr0, cnt0, xmid0, mods.spec(0, 5), mods.table, moe_w1[0], moe_w3[0], moe_w2[0])

    nq = ATT_HEADS * HEAD_DIM
    nk = KV_HEADS * HEAD_DIM
    perm = np.concatenate([h * HEAD_DIM + _HALF_SPLIT for h in range(ATT_HEADS + KV_HEADS)]
                          + [np.arange(nq + nk, nq + 2 * nk)])
    w_qkv = od_w_qkv[0][:, perm].astype(BF16)
    cos, sin = _rope_tables(seq)
    q, k, v = _qkv(ncc, x1, norm1[1][None], mods, w_qkv, od_q_norm[0][_HALF_SPLIT][None],
                   od_k_norm[0][_HALF_SPLIT][None], cos, sin)
    att = _attention(q, k, v)
    wr1, br1 = _router_weights(moe_wg[1], moe_bg[1], moe_we[1], moe_be[1])
    xmid1, h2e1, cr1, cnt1 = _post1(ncc, att, x1, mods, norm2[1][None], od_w_o[0].astype(BF16), wr1, br1)
    return _moe_block(h2e1, cr1, cnt1, xmid1, mods.spec(1, 5, True), mods.table,
                      moe_w1[1], moe_w3[1], moe_w2[1])
```

```python
import functools
import itertools

import numpy as np
import jax
import jax.numpy as jnp
from jax import lax
from jax.experimental import pallas as pl
from jax.experimental.pallas import tpu as pltpu

F32 = jnp.float32
BF16 = jnp.bfloat16
I32 = jnp.int32
U32 = jnp.uint32
HIGHEST = lax.Precision.HIGHEST

EPS = 1e-6
GRID_W = 64
LRU_BLOCKS = 8
LRU_C = 8.0
GLA_HEADS = 4
GLA_DK = 64
GLA_DV = 128
GLA_KEY = GLA_HEADS * GLA_DK
GLA_VAL = GLA_HEADS * GLA_DV
GLA_RANK = 16
GLA_TAU = 16.0
GLA_CHUNK = 64
_CHUNK_SHIFT = GLA_CHUNK.bit_length() - 1
_DK_SHIFT = GLA_DK.bit_length() - 1
ATT_HEADS = 8
KV_HEADS = 2
Q_PER_KV = ATT_HEADS // KV_HEADS
HEAD_DIM = 128
ROPE_THETA = 10000.0
N_GROUPS = 4
EXPERTS_PER_GROUP = 4
N_EXPERTS = N_GROUPS * EXPERTS_PER_GROUP
PAIRS_PER_GROUP = 6
N_CLASSES = N_GROUPS * PAIRS_PER_GROUP

LANES = 128
SUBLANES = 8
TOK = 256
MOE_TILE = 512
ATT_Q = 256
PERM_ROWS = 512
TOK_ROWS = 4
QKV_ROWS = 2
SEQ_ROWS = 4
WEXT = LANES
ROUTER_ROWS = 32
VMEM_LIMIT = 56 * 1024 * 1024


def _cparams(sem):
    return pltpu.CompilerParams(dimension_semantics=sem, vmem_limit_bytes=VMEM_LIMIT)


def _rms(x, g):
    return x * lax.rsqrt(jnp.mean(x * x, axis=-1, keepdims=True) + EPS) * g


def _sigmoid(x):
    return 1.0 / (1.0 + jnp.exp(-x))


def _silu(x):
    return x * _sigmoid(x)


def _gelu_tanh(x):
    return 0.5 * x * (1.0 + jnp.tanh(np.sqrt(2.0 / np.pi).astype(np.float32) * (x + 0.044715 * (x * x * x))))


def _pack_bf16_pairs(xb):
    k = xb.shape[-1] // 2
    lo = lax.shift_right_logical(pltpu.bitcast(xb[:, :k].astype(F32), U32), jnp.uint32(16))
    hi = pltpu.bitcast(xb[:, k:].astype(F32), U32) & jnp.uint32(0xFFFF0000)
    return hi | lo


def _unpack_bf16_pairs(words):
    lo = pltpu.bitcast(lax.shift_left(words, jnp.uint32(16)), F32).astype(BF16)
    hi = pltpu.bitcast(words & jnp.uint32(0xFFFF0000), F32).astype(BF16)
    return jnp.concatenate([lo, hi], axis=1)


def _softplus(x):
    return jnp.maximum(x, 0.0) + jnp.log(1.0 + jnp.exp(-jnp.abs(x)))


def _log_sigmoid(x):
    return -_softplus(-x)


def _adaln_kernel(cv_ref, w_ref, b_ref, o_ref):
    s = _silu(cv_ref[...])
    o_ref[0] = jnp.dot(s, w_ref[0], precision=HIGHEST, preferred_element_type=F32) + b_ref[0]


def _adaln(cv, w_ada, b_ada):
    depth, d, n6 = w_ada.shape
    rows = cv.shape[0]
    tn = 6 * d // 4
    return pl.pallas_call(
        _adaln_kernel,
        grid=(depth, n6 // tn),
        in_specs=[pl.BlockSpec((rows, d), lambda l, j: (0, 0)),
                  pl.BlockSpec((1, d, tn), lambda l, j: (l, 0, j)),
                  pl.BlockSpec((1, 1, tn), lambda l, j: (l, 0, j))],
        out_specs=pl.BlockSpec((1, rows, tn), lambda l, j: (l, 0, j)),
        out_shape=jax.ShapeDtypeStruct((depth, rows, n6), F32),
        compiler_params=_cparams(("arbitrary", "arbitrary")),
        name="adaln",
    )(cv, w_ada, b_ada.reshape(depth, 1, n6))


class _Mods:
    def __init__(self, table, batch, ncc):
        self.table, self.batch, self.ncc = table, batch, ncc
        self.d = table.shape[-1]

    def spec(self, layer, j, latent_only=False, nb=1):
        batch, ncc = self.batch, self.ncc

        def imap(b, c):
            r = b if latent_only else jnp.where(c < ncc, batch // nb, b)
            return (layer * 6 + j, r, 0, 0)

        return pl.BlockSpec((1, nb, 1, self.d), imap)


def _lockstep(generators):
    for _ in itertools.zip_longest(*generators):
        pass


def _inproj_kernel(ncc, nch, lw, x_ref, c_ref, xp_ref, xn_ref, cp_ref, cn_ref, g_ref, sh_ref, sc_ref,
                   w_ref, wlr_ref, cw_ref, cb_ref, pm_ref, lr_ref):
    _lockstep([_inproj_row(ncc, nch, lw, i, x_ref, c_ref, xp_ref, xn_ref, cp_ref, cn_ref, g_ref, sh_ref, sc_ref,
                           w_ref, wlr_ref, cw_ref, cb_ref, pm_ref, lr_ref) for i in range(x_ref.shape[0])])


def _inproj_row(ncc, nch, lw, i, x_ref, c_ref, xp_ref, xn_ref, cp_ref, cn_ref, g_ref, sh_ref, sc_ref,
                w_ref, wlr_ref, cw_ref, cb_ref, pm_ref, lr_ref):
    c = pl.program_id(1)
    is_ctx = c < ncc

    def modulated(rows):
        return (_rms(rows, g_ref[...]) * (1.0 + sc_ref[0, i]) + sh_ref[0, i]).astype(BF16)

    hb = modulated(jnp.where(is_ctx, c_ref[i], x_ref[i]))
    pm = jnp.dot(hb, w_ref[...], preferred_element_type=F32)
    yield
    lr_ref[i] = jnp.dot(hb, wlr_ref[...], preferred_element_type=F32).astype(BF16)

    halo = jnp.concatenate([jnp.where(is_ctx, cp_ref[i], xp_ref[i]), jnp.where(is_ctx, cn_ref[i], xn_ref[i])], axis=0)
    xa_halo = jnp.dot(modulated(halo), w_ref[:, :lw], preferred_element_type=F32)
    yield
    has_prev = jnp.logical_and(c != 0, c != ncc)
    has_next = jnp.logical_and(c != ncc - 1, c != nch - 1)
    p2 = jnp.where(has_prev, xa_halo[SUBLANES - 2:SUBLANES - 1], 0.0)
    p1 = jnp.where(has_prev, xa_halo[SUBLANES - 1:SUBLANES], 0.0)
    n0 = jnp.where(has_next, xa_halo[SUBLANES:SUBLANES + 1], 0.0)
    xa = pm[:, :lw]
    row = lax.broadcasted_iota(I32, xa.shape, 0)
    x_m1 = jnp.where(row == 0, p1, pltpu.roll(xa, 1, 0))
    x_m2 = jnp.where(row == 0, p2, jnp.where(row == 1, p1, pltpu.roll(xa, 2, 0)))
    x_p1 = jnp.where(row == TOK - 1, n0, pltpu.roll(xa, TOK - 1, 0))
    cw = cw_ref[...]
    u = cw[0:1] * x_m2 + cw[1:2] * x_m1 + cw[2:3] * xa + cw[3:4] * x_p1 + cb_ref[...]
    pm_ref[i, :, :lw] = u.astype(BF16)
    pm_ref[i, :, lw:] = pm[:, lw:].astype(BF16)


def _inproj(x, ctx, gain, mods, w_main, w_lr, conv_w, conv_b, lw):
    batch, seq, d = x.shape
    ncc = ctx.shape[1] // TOK
    nl = seq // TOK
    nch = ncc + nl
    nm = w_main.shape[1]
    hb = TOK // SUBLANES
    nb = TOK_ROWS
    return pl.pallas_call(
        functools.partial(_inproj_kernel, ncc, nch, lw),
        grid=(batch // nb, nch),
        in_specs=[pl.BlockSpec((nb, TOK, d), lambda b, c: (b, jnp.maximum(c - ncc, 0), 0)),
                  pl.BlockSpec((nb, TOK, d), lambda b, c: (b, jnp.minimum(c, ncc - 1), 0)),
                  pl.BlockSpec((nb, SUBLANES, d), lambda b, c: (b, jnp.maximum((c - ncc) * hb - 1, 0), 0)),
                  pl.BlockSpec((nb, SUBLANES, d),
                               lambda b, c: (b, jnp.clip((c - ncc + 1) * hb, 0, nl * hb - 1), 0)),
                  pl.BlockSpec((nb, SUBLANES, d), lambda b, c: (b, jnp.clip(c * hb - 1, 0, ncc * hb - 1), 0)),
                  pl.BlockSpec((nb, SUBLANES, d), lambda b, c: (b, jnp.clip((c + 1) * hb, 0, ncc * hb - 1), 0)),
                  pl.BlockSpec((1, d), lambda b, c: (0, 0)),
                  mods.spec(0, 0, nb=nb), mods.spec(0, 1, nb=nb),
                  pl.BlockSpec((d, nm), lambda b, c: (0, 0)),
                  pl.BlockSpec((d, LANES), lambda b, c: (0, 0)),
                  pl.BlockSpec((4, lw), lambda b, c: (0, 0)),
                  pl.BlockSpec((1, lw), lambda b, c: (0, 0))],
        out_specs=[pl.BlockSpec((nb, TOK, nm), lambda b, c: (b, c, 0)),
                   pl.BlockSpec((nb, TOK, LANES), lambda b, c: (b, c, 0))],
        out_shape=[jax.ShapeDtypeStruct((batch, nch * TOK, nm), BF16),
                   jax.ShapeDtypeStruct((batch, nch * TOK, LANES), BF16)],
        compiler_params=_cparams(("arbitrary", "arbitrary")),
        name="inproj",
    )(x, ctx, x, x, ctx, ctx, gain, mods.table, mods.table, w_main, w_lr, conv_w, conv_b)


def _lru_scan(a, b, h0, rev):
    n_groups = TOK // SUBLANES
    a = a.reshape(n_groups, SUBLANES, a.shape[-1])
    b = b.reshape(a.shape)
    sub = lax.broadcasted_iota(I32, a.shape, 1)
    for dist in (1, 2, 4):
        shift = SUBLANES - dist if rev else dist
        a_s = pltpu.roll(a, shift, 1)
        b_s = pltpu.roll(b, shift, 1)
        m = (sub < SUBLANES - dist) if rev else (sub >= dist)
        b = jnp.where(m, a * b_s + b, b)
        a = jnp.where(m, a * a_s, a)
    order = range(n_groups - 1, -1, -1) if rev else range(n_groups)
    outs = [None] * n_groups
    h = h0
    for r in order:
        hr = a[r] * h + b[r]
        outs[r] = hr
        h = hr[0:1] if rev else hr[SUBLANES - 1:SUBLANES]
    return jnp.concatenate(outs, axis=0), h


def _seqmix_kernel(rev, *refs):
    h_scr, s_scr = refs[-2:]

    @pl.when(pl.program_id(1) == 0)
    def _():
        h_scr[...] = jnp.zeros_like(h_scr)
        s_scr[...] = jnp.zeros_like(s_scr)

    _lockstep([_seqmix_row(rev, i, *refs) for i in range(refs[0].shape[0])])


def _seqmix_row(rev, i, u_ref, q_ref, k_ref, v_ref, lr_ref,
                wgt_ref, bgt_ref, lam_ref, wg_ref, bg_ref, sel_ref,
                lru_ref, gla_ref, h_scr, s_scr):
    lw = u_ref.shape[-1]
    ub = u_ref[i]
    u = ub.astype(F32)
    gates = jnp.dot(ub, wgt_ref[...], preferred_element_type=F32) + bgt_ref[...]
    yield
    r_gate = _sigmoid(gates[:, :lw])
    i_gate = _sigmoid(gates[:, lw:])
    log_a = (-LRU_C) * r_gate * _softplus(-lam_ref[...])
    a = jnp.exp(log_a)
    bb = jnp.sqrt(1.0 - a * a) * (i_gate * u)
    hs, h_last = _lru_scan(a, bb, h_scr[i, 0:1], rev)
    h_scr[i, 0:1] = h_last
    lru_ref[i] = hs.astype(BF16)
    yield

    lg = _log_sigmoid(jnp.dot(lr_ref[i], wg_ref[...], preferred_element_type=F32) + bg_ref[...]) * (1.0 / GLA_TAU)
    lg_hi = lg.astype(BF16)
    lg_lo = (lg - lg_hi.astype(F32)).astype(BF16)
    yield
    cum = jnp.dot(sel_ref[...], jnp.concatenate([lg_hi, lg_lo], axis=1), preferred_element_type=F32)
    yield
    g = cum[:TOK, :GLA_KEY] + cum[:TOK, GLA_KEY:]
    tot = cum[TOK:, :GLA_KEY] + cum[TOK:, GLA_KEY:]
    q = q_ref[i].astype(F32)
    k = k_ref[i].astype(F32)
    dec_all = jnp.exp(tot)
    k_neg = k * jnp.exp(-g)
    qg = (q * jnp.exp(g) * (GLA_DK ** -0.5)).astype(BF16)
    kg = k_neg.astype(BF16)
    kd = (k_neg * dec_all).astype(BF16)
    v = v_ref[i]

    lane = lax.broadcasted_iota(I32, (1, GLA_KEY), 1) >> _DK_SHIFT
    head_masks = [(lane == h) for h in range(GLA_HEADS)]
    n_stack = GLA_HEADS * GLA_CHUNK
    ai = lax.broadcasted_iota(I32, (n_stack, GLA_CHUNK), 0) & (GLA_CHUNK - 1)
    aj = lax.broadcasted_iota(I32, (n_stack, GLA_CHUNK), 1)
    causal = (aj >= ai) if rev else (aj <= ai)

    n_sub = TOK // GLA_CHUNK
    order = range(n_sub - 1, -1, -1) if rev else range(n_sub)
    outs = [None] * n_sub
    st = s_scr[i]
    for n in order:
        sl = slice(n * GLA_CHUNK, (n + 1) * GLA_CHUNK)
        qn = qg[sl]
        qs = jnp.concatenate([jnp.where(head_masks[h], qn, jnp.zeros_like(qn)) for h in range(GLA_HEADS)], axis=0)
        att = lax.dot_general(qs, kg[sl], (((1,), (1,)), ((), ())), preferred_element_type=F32)
        yield
        att = jnp.where(causal, att, 0.0).astype(BF16)
        o_all = jnp.dot(att, v[sl], preferred_element_type=F32)
        yield
        o_int = lax.dot_general(qs, st.astype(BF16), (((1,), (1,)), ((), ())),
                                preferred_element_type=F32)
        yield
        outs[n] = jnp.concatenate(
            [o_all[h * GLA_CHUNK:(h + 1) * GLA_CHUNK, h * GLA_DV:(h + 1) * GLA_DV]
             + o_int[h * GLA_CHUNK:(h + 1) * GLA_CHUNK] for h in range(GLA_HEADS)], axis=1)
        vn, kdn = v[sl], kd[sl]
        v_stack = jnp.concatenate([vn[:, h * GLA_DV:(h + 1) * GLA_DV] for h in range(GLA_HEADS)], axis=0)
        kd_stack = jnp.concatenate([jnp.where(head_masks[h], kdn, jnp.zeros_like(kdn)) for h in range(GLA_HEADS)],
                                   axis=0)
        ds = lax.dot_general(v_stack, kd_stack, (((0,), (0,)), ((), ())), preferred_element_type=F32)
        yield
        st = dec_all[n * GLA_CHUNK:n * GLA_CHUNK + 1] * st + ds
    s_scr[i] = st
    gla_ref[i] = jnp.concatenate(outs, axis=0).astype(BF16)


def _cumsum_selector(rev):
    r = np.arange(TOK)[:, None]
    c = np.arange(TOK)[None, :]
    same = (r // GLA_CHUNK) == (c // GLA_CHUNK)
    tri = same & ((c >= r) if rev else (c <= r))
    return jnp.asarray(np.concatenate([tri, same], axis=0), BF16)


def _seqmix(rev, ncc, pm, lr, w_gate, b_gate, lam, wg_pad, bg, lw):
    batch, t, _ = pm.shape
    nch = t // TOK
    nl = nch - ncc

    def chunk(s):
        if rev:
            return jnp.where(s < ncc, ncc - 1 - s, ncc + (nl - 1) - (s - ncc))
        return s

    qcol = 2 * lw // GLA_KEY
    vcol = (2 * lw + 2 * GLA_KEY) // GLA_VAL
    const = lambda shape: pl.BlockSpec(shape, lambda b, s: (0,) * len(shape))
    nb = SEQ_ROWS if batch % SEQ_ROWS == 0 else 1
    return pl.pallas_call(
        functools.partial(_seqmix_kernel, rev),
        grid=(batch // nb, nch),
        in_specs=[pl.BlockSpec((nb, TOK, lw), lambda b, s: (b, chunk(s), 0)),
                  pl.BlockSpec((nb, TOK, GLA_KEY), lambda b, s: (b, chunk(s), qcol)),
                  pl.BlockSpec((nb, TOK, GLA_KEY), lambda b, s: (b, chunk(s), qcol + 1)),
                  pl.BlockSpec((nb, TOK, GLA_VAL), lambda b, s: (b, chunk(s), vcol)),
                  pl.BlockSpec((nb, TOK, LANES), lambda b, s: (b, chunk(s), 0)),
                  const((lw, 2 * lw)), const((1, 2 * lw)), const((1, lw)),
                  const((LANES, GLA_KEY)), const((1, GLA_KEY)), const((2 * TOK, TOK))],
        out_specs=[pl.BlockSpec((nb, TOK, lw), lambda b, s: (b, chunk(s), 0)),
                   pl.BlockSpec((nb, TOK, GLA_VAL), lambda b, s: (b, chunk(s), 0))],
        out_shape=[jax.ShapeDtypeStruct((batch, t, lw), BF16),
                   jax.ShapeDtypeStruct((batch, t, GLA_VAL), BF16)],
        scratch_shapes=[pltpu.VMEM((nb, SUBLANES, lw), F32), pltpu.VMEM((nb, GLA_DV, GLA_KEY), F32)],
        compiler_params=_cparams(("arbitrary", "arbitrary")),
        name="seqmix_rev" if rev else "seqmix_fwd",
    )(pm, pm, pm, pm, lr, w_gate, b_gate, lam, wg_pad, bg, _cumsum_selector(rev))


def _route_epilogue(i, first, xm, n2_ref, sh2_ref, sc2_ref, wr_ref, br_ref,
                    xmid_ref, h2_ref, cr_ref, cnt_ref, carry):
    d = xm.shape[-1]
    xmid_ref[i] = xm
    h2 = _rms(xm, n2_ref[...]) * (1.0 + sc2_ref[0, i]) + sh2_ref[0, i]
    h_hi = h2.astype(BF16)
    h_lo = (h2 - h_hi.astype(F32)).astype(BF16)
    nt = (((1,), (1,)), ((), ()))
    o1 = lax.dot_general(wr_ref[...], h_hi, nt, preferred_element_type=F32)
    o2 = lax.dot_general(wr_ref[:ROUTER_ROWS], h_lo, nt, preferred_element_type=F32)
    yield
    logits = o1[:ROUTER_ROWS] + o1[ROUTER_ROWS:] + o2 + br_ref[...]

    def col(j):
        return logits[j:j + 1]

    lgs = [col(g) for g in range(N_GROUPS)]
    gmax = functools.reduce(jnp.maximum, lgs)
    gi = jnp.where(lgs[0] == gmax, 0, jnp.where(lgs[1] == gmax, 1, jnp.where(lgs[2] == gmax, 2, 3)))
    w_group = 1.0 / functools.reduce(lambda p, q: p + q, [jnp.exp(l - gmax) for l in lgs])
    es = []
    for j in range(EXPERTS_PER_GROUP):
        acc = jnp.zeros_like(gmax)
        for g in range(N_GROUPS):
            acc = acc + jnp.where(gi == g, col(N_GROUPS + g * EXPERTS_PER_GROUP + j), 0.0)
        es.append(acc)
    m1 = functools.reduce(jnp.maximum, es)
    i1 = jnp.where(es[0] == m1, 0, jnp.where(es[1] == m1, 1, jnp.where(es[2] == m1, 2, 3)))
    rest = [jnp.where(i1 == j, -jnp.inf, es[j]) for j in range(EXPERTS_PER_GROUP)]
    m2 = functools.reduce(jnp.maximum, rest)
    i2 = jnp.where(rest[0] == m2, 0, jnp.where(rest[1] == m2, 1, jnp.where(rest[2] == m2, 2, 3)))
    e2 = jnp.exp(m2 - m1)
    w1 = w_group / (1.0 + e2)
    w2 = w_group * e2 / (1.0 + e2)
    first_lo = i1 < i2
    lo = jnp.where(first_lo, i1, i2)
    hi = jnp.where(first_lo, i2, i1)
    w_lo = jnp.where(first_lo, w1, w2)
    w_hi = jnp.where(first_lo, w2, w1)
    pidx = jnp.where(lo == 0, hi - 1, jnp.where(lo == 1, hi + 1, 5))
    cls = gi * PAIRS_PER_GROUP + pidx

    def terms(w):
        t1 = w.astype(BF16).astype(F32)
        t2 = (w - t1).astype(BF16).astype(F32)
        t3 = (w - t1 - t2).astype(BF16).astype(F32)
        return [t1, t2, t3]

    sub = lax.broadcasted_iota(I32, (2 * SUBLANES, TOK), 0)
    stacked = jnp.zeros((2 * SUBLANES, TOK), F32)
    for r, term in enumerate(terms(w_lo) + terms(w_hi)):
        stacked = jnp.where(sub == r, term, stacked)
    er = lax.broadcasted_iota(I32, (2 * SUBLANES, LANES), 0)
    el = lax.broadcasted_iota(I32, (2 * SUBLANES, LANES), 1)
    place = jnp.logical_or(jnp.logical_and(er < 3, el == 0),
                           jnp.logical_and(jnp.logical_and(er >= 3, er < 6), el == 1))
    w_cols = lax.dot_general(stacked.astype(BF16), place.astype(BF16), (((0,), (0,)), ((), ())),
                             preferred_element_type=F32)
    h2_ref[i, :, :d // 2] = _pack_bf16_pairs(h_hi)
    h2_ref[i, :, d // 2:] = pltpu.bitcast(w_cols, U32)

    krow = lax.broadcasted_iota(I32, (ROUTER_ROWS, TOK), 0)
    onehot = krow == cls
    ri = lax.broadcasted_iota(I32, (TOK, TOK), 0)
    ci = lax.broadcasted_iota(I32, (TOK, TOK), 1)
    triu = (ri <= ci).astype(BF16)
    pref = jnp.dot(onehot.astype(BF16), triu, preferred_element_type=F32)
    yield

    if i == 0:
        @pl.when(first)
        def _():
            carry[...] = jnp.zeros_like(carry)

    base = carry[:, 0:1]
    rank = jnp.sum(jnp.where(onehot, pref - 1.0 + base, 0.0), axis=0, keepdims=True)
    new_carry = jnp.broadcast_to(base + jnp.sum(onehot.astype(F32), axis=1, keepdims=True), carry.shape)
    carry[...] = new_carry
    cnt_ref[...] = new_carry
    sub8 = lax.broadcasted_iota(I32, (SUBLANES, TOK), 0)
    cr_ref[i, 0] = jnp.where(sub8 == 0, cls, jnp.where(sub8 == 1, rank.astype(I32), 0))


def _post0_kernel(ncc, *refs):
    _lockstep([_post0_row(ncc, i, *refs) for i in range(refs[0].shape[0])])


def _post0_row(ncc, i, l0_ref, l1_ref, g0_ref, g1_ref, ga_ref, r_ref, x_ref, c_ref,
               gate_ref, sh2_ref, sc2_ref, n2_ref, gn_ref, wout_ref, wr_ref, br_ref,
               xmid_ref, h2_ref, cr_ref, cnt_ref, carry):
    b, c = pl.program_id(0), pl.program_id(1)
    lru = l0_ref[i].astype(F32) + l1_ref[i].astype(F32)
    ya = lru * _gelu_tanh(ga_ref[i].astype(F32))
    gla = g0_ref[i].astype(F32) + g1_ref[i].astype(F32)
    parts = []
    for h in range(GLA_HEADS):
        parts.append(_rms(gla[:, h * GLA_DV:(h + 1) * GLA_DV], gn_ref[...]))
    yb = jnp.concatenate(parts, axis=1) * _silu(r_ref[i].astype(F32))
    ycat = jnp.concatenate([ya, yb], axis=1).astype(BF16)
    y = jnp.dot(ycat, wout_ref[...], preferred_element_type=F32)
    yield
    x0 = jnp.where(c < ncc, c_ref[i], x_ref[i])
    xm = x0 + gate_ref[0, i] * y
    yield from _route_epilogue(i, jnp.logical_and(b == 0, c == 0), xm, n2_ref, sh2_ref, sc2_ref, wr_ref, br_ref,
                               xmid_ref, h2_ref, cr_ref, cnt_ref, carry)


def _post1_kernel(*refs):
    _lockstep([_post1_row(i, *refs) for i in range(refs[0].shape[0])])


def _post1_row(i, o_ref, x_ref, gate_ref, sh2_ref, sc2_ref, n2_ref, wo_ref, wr_ref, br_ref,
               xmid_ref, h2_ref, cr_ref, cnt_ref, carry):
    b, c = pl.program_id(0), pl.program_id(1)
    y = jnp.dot(o_ref[i], wo_ref[...], preferred_element_type=F32)
    yield
    xm = x_ref[i] + gate_ref[0, i] * y
    yield from _route_epilogue(i, jnp.logical_and(b == 0, c == 0), xm, n2_ref, sh2_ref, sc2_ref, wr_ref, br_ref,
                               xmid_ref, h2_ref, cr_ref, cnt_ref, carry)


def _route_out(batch, nch, d, nb):
    specs = [pl.BlockSpec((nb, TOK, d), lambda b, c: (b, c, 0)),
             pl.BlockSpec((nb, TOK, d // 2 + WEXT), lambda b, c: (b, c, 0)),
             pl.BlockSpec((nb, 1, SUBLANES, TOK), lambda b, c: (b, c, 0, 0)),
             pl.BlockSpec((ROUTER_ROWS, LANES), lambda b, c: (0, 0))]
    shapes = [jax.ShapeDtypeStruct((batch, nch * TOK, d), F32),
              jax.ShapeDtypeStruct((batch, nch * TOK, d // 2 + WEXT), U32),
              jax.ShapeDtypeStruct((batch, nch, SUBLANES, TOK), I32),
              jax.ShapeDtypeStruct((ROUTER_ROWS, LANES), F32)]
    return specs, shapes


def _post0(ncc, lru_f, lru_r, gla_f, gla_r, pm, x, ctx, mods, n2, gn, w_out, wr, br, lw):
    batch, t, _ = pm.shape
    d = x.shape[-1]
    nch = t // TOK
    rcol = (2 * lw + 2 * GLA_KEY + GLA_VAL) // GLA_VAL
    nb = TOK_ROWS
    tokspec = lambda w: pl.BlockSpec((nb, TOK, w), lambda b, c: (b, c, 0))
    const = lambda shape: pl.BlockSpec(shape, lambda b, c: (0,) * len(shape))
    out_specs, out_shape = _route_out(batch, nch, d, nb)
    return pl.pallas_call(
        functools.partial(_post0_kernel, ncc),
        grid=(batch // nb, nch),
        in_specs=[tokspec(lw), tokspec(lw), tokspec(GLA_VAL), tokspec(GLA_VAL),
                  pl.BlockSpec((nb, TOK, lw), lambda b, c: (b, c, 1)),
                  pl.BlockSpec((nb, TOK, GLA_VAL), lambda b, c: (b, c, rcol)),
                  pl.BlockSpec((nb, TOK, d), lambda b, c: (b, jnp.maximum(c - ncc, 0), 0)),
                  pl.BlockSpec((nb, TOK, d), lambda b, c: (b, jnp.minimum(c, ncc - 1), 0)),
                  mods.spec(0, 2, nb=nb), mods.spec(0, 3, nb=nb), mods.spec(0, 4, nb=nb),
                  const((1, d)), const((1, GLA_DV)), const((lw + GLA_VAL, d)),
                  const((2 * ROUTER_ROWS, d)), const((ROUTER_ROWS, 1))],
        out_specs=out_specs, out_shape=out_shape,
        scratch_shapes=[pltpu.VMEM((ROUTER_ROWS, LANES), F32)],
        compiler_params=_cparams(("arbitrary", "arbitrary")),
        name="post0",
    )(lru_f, lru_r, gla_f, gla_r, pm, pm, x, ctx, mods.table, mods.table, mods.table, n2, gn, w_out, wr, br)


def _post1(ncc, att, x1, mods, n2, w_o, wr, br):
    batch, seq, aw = att.shape
    d = x1.shape[-1]
    nl = seq // TOK
    const = lambda shape: pl.BlockSpec(shape, lambda b, c: (0,) * len(shape))
    nb = TOK_ROWS
    out_specs, out_shape = _route_out(batch, nl, d, nb)
    return pl.pallas_call(
        _post1_kernel,
        grid=(batch // nb, nl),
        in_specs=[pl.BlockSpec((nb, TOK, aw), lambda b, c: (b, c, 0)),
                  pl.BlockSpec((nb, TOK, d), lambda b, c: (b, c + ncc, 0)),
                  mods.spec(1, 2, True, nb), mods.spec(1, 3, True, nb), mods.spec(1, 4, True, nb),
                  const((1, d)), const((aw, d)), const((2 * ROUTER_ROWS, d)), const((ROUTER_ROWS, 1))],
        out_specs=out_specs, out_shape=out_shape,
        scratch_shapes=[pltpu.VMEM((ROUTER_ROWS, LANES), F32)],
        compiler_params=_cparams(("arbitrary", "arbitrary")),
        name="post1",
    )(att, x1, mods.table, mods.table, mods.table, n2, w_o, wr, br)


_PAIR_LO = np.array([0, 0, 0, 1, 1, 2], np.int32)
_PAIR_HI = np.array([1, 2, 3, 2, 3, 3], np.int32)


def _plan(cr, cnt, n_tiles):
    cls, rank = cr[:, :, 0, :].reshape(-1), cr[:, :, 1, :].reshape(-1)
    counts = cnt[:N_CLASSES, 0].astype(I32)
    tiles = (counts + MOE_TILE - 1) // MOE_TILE
    ends = jnp.cumsum(tiles)
    offs = (ends - tiles) * MOE_TILE
    classes = jnp.arange(N_CLASSES, dtype=I32)
    pos = jnp.sum(jnp.where(cls[:, None] == classes[None, :], offs[None, :], 0), axis=1) + rank
    tile_ids = jnp.arange(n_tiles, dtype=I32)
    tile_cls = jnp.minimum(jnp.sum((tile_ids[:, None] >= ends[None, :]).astype(I32), axis=1), N_CLASSES - 1)
    grp = tile_cls // PAIRS_PER_GROUP
    pair = tile_cls % PAIRS_PER_GROUP
    lo = grp * EXPERTS_PER_GROUP + jnp.asarray(_PAIR_LO)[pair]
    hi = grp * EXPERTS_PER_GROUP + jnp.asarray(_PAIR_HI)[pair]
    return pos.astype(I32), lo.astype(I32), hi.astype(I32)


def _moe_kernel(d, lo_ref, hi_ref, x_ref, w1l, w3l, w2l, w1h, w3h, w2h, y_ref, wbuf, wbuf2):
    i = pl.program_id(0)

    changed = jnp.logical_or(i == 0, jnp.logical_or(lo_ref[i] != lo_ref[jnp.maximum(i - 1, 0)],
                                                    hi_ref[i] != hi_ref[jnp.maximum(i - 1, 0)]))

    @pl.when(changed)
    def _():
        for k, w in enumerate((w1l, w3l, w1h, w3h)):
            wbuf[k] = w[0].astype(BF16)
        wbuf2[0] = w2l[0].astype(BF16)
        wbuf2[1] = w2h[0].astype(BF16)

    x = _unpack_bf16_pairs(x_ref[:, :d // 2])
    wts = pltpu.bitcast(x_ref[:, d // 2:], F32)

    def expert(k1, k3, k2):
        a = jnp.dot(x, wbuf[k1], preferred_element_type=F32)
        b = jnp.dot(x, wbuf[k3], preferred_element_type=F32)
        return jnp.dot((_silu(a) * b).astype(BF16), wbuf2[k2], preferred_element_type=F32)

    y = wts[:, 0:1] * expert(0, 1, 0) + wts[:, 1:2] * expert(2, 3, 1)
    y_ref[...] = _pack_bf16_pairs(y.astype(BF16))


def _moe(xs, lo, hi, w1, w3, w2):
    n_pad, wd = xs.shape
    d = 2 * (wd - WEXT)
    hdim = w1.shape[-1]
    n_tiles = n_pad // MOE_TILE
    wl = lambda i, lo, hi: (lo[i], 0, 0)
    wh = lambda i, lo, hi: (hi[i], 0, 0)
    grid_spec = pltpu.PrefetchScalarGridSpec(
        num_scalar_prefetch=2,
        grid=(n_tiles,),
        in_specs=[pl.BlockSpec((MOE_TILE, wd), lambda i, lo, hi: (i, 0)),
                  pl.BlockSpec((1, d, hdim), wl), pl.BlockSpec((1, d, hdim), wl), pl.BlockSpec((1, hdim, d), wl),
                  pl.BlockSpec((1, d, hdim), wh), pl.BlockSpec((1, d, hdim), wh), pl.BlockSpec((1, hdim, d), wh)],
        out_specs=pl.BlockSpec((MOE_TILE, d // 2), lambda i, lo, hi: (i, 0)),
        scratch_shapes=[pltpu.VMEM((4, d, hdim), BF16), pltpu.VMEM((2, hdim, d), BF16)],
    )
    return pl.pallas_call(
        functools.partial(_moe_kernel, d),
        grid_spec=grid_spec,
        out_shape=jax.ShapeDtypeStruct((n_pad, d // 2), U32),
        compiler_params=_cparams(("arbitrary",)),
        name="moe_experts",
    )(lo, hi, xs, w1, w3, w2, w1, w3, w2)


def _row_dma_loops(tile_rows, make_copy, index):
    def issue(grp, carry):
        for sub in range(SUBLANES):
            make_copy(grp, sub, index(grp * SUBLANES + sub)).start()
        return carry

    def drain(grp, carry):
        for _ in range(SUBLANES):
            make_copy(0, 0, 0).wait()
        return carry

    n_groups = tile_rows // SUBLANES
    return (lambda: lax.fori_loop(0, n_groups, issue, 0)), (lambda: lax.fori_loop(0, n_groups, drain, 0))


def _scatter_rows_kernel(rows, pos_ref, src_ref, init_ref, dst_ref, sem):
    del init_ref
    issue, drain = _row_dma_loops(
        rows,
        lambda grp, sub, p: pltpu.make_async_copy(src_ref.at[grp, pl.ds(sub, 1)], dst_ref.at[pl.ds(p, 1)], sem),
        lambda j: pos_ref[0, 0, j])
    issue()
    drain()


def _scatter_rows(src, pos, n_pad):
    n, w = src.shape
    rows = PERM_ROWS if n % PERM_ROWS == 0 else TOK
    steps = n // rows
    return pl.pallas_call(
        functools.partial(_scatter_rows_kernel, rows),
        grid=(steps,),
        in_specs=[pl.BlockSpec((1, 1, rows), lambda i: (i, 0, 0), memory_space=pltpu.SMEM),
                  pl.BlockSpec((rows // SUBLANES, SUBLANES, w), lambda i: (i, 0, 0)),
                  pl.BlockSpec(memory_space=pl.ANY)],
        out_specs=pl.BlockSpec(memory_space=pl.ANY),
        out_shape=jax.ShapeDtypeStruct((n_pad, w), src.dtype),
        scratch_shapes=[pltpu.SemaphoreType.DMA(())],
        input_output_aliases={2: 0},
        compiler_params=pltpu.CompilerParams(dimension_semantics=("arbitrary",), has_side_effects=True),
        name="scatter_rows",
    )(pos.reshape(steps, 1, rows), src.reshape(n // SUBLANES, SUBLANES, w), jnp.zeros((n_pad, w), src.dtype))


def _gather_residual_kernel(n_steps, pos_ref, posn_ref, ys_ref, xm_ref, gate_ref, o_ref, buf, sem):
    step = pl.program_id(0) * pl.num_programs(1) + pl.program_id(1)
    slot = step % 2

    def loops(idx_ref, s):
        return _row_dma_loops(
            TOK,
            lambda grp, sub, p: pltpu.make_async_copy(ys_ref.at[pl.ds(p, 1)], buf.at[s, grp, pl.ds(sub, 1)],
                                                      sem.at[s]),
            lambda j: idx_ref[0, 0, j])

    @pl.when(step == 0)
    def _():
        loops(pos_ref, slot)[0]()

    @pl.when(step + 1 < n_steps)
    def _():
        loops(posn_ref, 1 - slot)[0]()

    loops(pos_ref, slot)[1]()
    y = _unpack_bf16_pairs(buf[slot].reshape(TOK, buf.shape[-1])).astype(F32)
    o_ref[0] = xm_ref[0] + gate_ref[0, 0] * y


def _gather_residual(ys, pos, xmid, gate_spec, mod_table):
    batch, t, d = xmid.shape
    nch = t // TOK
    n_steps = batch * nch
    return pl.pallas_call(
        functools.partial(_gather_residual_kernel, n_steps),
        grid=(batch, nch),
        in_specs=[pl.BlockSpec((1, 1, TOK), lambda b, c: (b * nch + c, 0, 0), memory_space=pltpu.SMEM),
                  pl.BlockSpec((1, 1, TOK), lambda b, c: (jnp.minimum(b * nch + c + 1, n_steps - 1), 0, 0),
                               memory_space=pltpu.SMEM),
                  pl.BlockSpec(memory_space=pl.ANY),
                  pl.BlockSpec((1, TOK, d), lambda b, c: (b, c, 0)),
                  gate_spec],
        out_specs=pl.BlockSpec((1, TOK, d), lambda b, c: (b, c, 0)),
        out_shape=jax.ShapeDtypeStruct((batch, t, d), F32),
        scratch_shapes=[pltpu.VMEM((2, TOK // SUBLANES, SUBLANES, d // 2), U32), pltpu.SemaphoreType.DMA((2,))],
        compiler_params=_cparams(("arbitrary", "arbitrary")),
        name="gather_residual",
    )(pos.reshape(n_steps, 1, TOK), pos.reshape(n_steps, 1, TOK), ys, xmid, mod_table)


def _moe_block(h2ext, cr, cnt, xmid, gate_spec, mod_table, w1, w3, w2):
    h2ext = h2ext.reshape(-1, h2ext.shape[-1])
    n = h2ext.shape[0]
    n_tiles = n // MOE_TILE + N_CLASSES
    pos, lo, hi = _plan(cr, cnt, n_tiles)
    xs = _scatter_rows(h2ext, pos, n_tiles * MOE_TILE)
    ys = _moe(xs, lo, hi, w1, w3, w2)
    return _gather_residual(ys, pos, xmid, gate_spec, mod_table)


def _qkv_kernel(ncc, x_ref, g_ref, sh_ref, sc_ref, w_ref, qn_ref, kn_ref, cos_ref, sin_ref,
                q_ref, k_ref, v_ref):
    c = pl.program_id(1)
    latent = c >= ncc
    nb = x_ref.shape[0]
    nq = ATT_HEADS * HEAD_DIM
    nk = KV_HEADS * HEAD_DIM
    cos = cos_ref[...]
    sin = sin_ref[...]
    hbs = [None] * nb

    def head(z, gain):
        z = _rms(z, gain)
        rot = z * cos + pltpu.roll(z, HEAD_DIM // 2, 1) * sin
        return jnp.where(latent, rot, z)

    def kv_row(r):
        h = _rms(x_ref[r], g_ref[...]) * (1.0 + sc_ref[0, r]) + sh_ref[0, r]
        hbs[r] = h.astype(BF16)
        kv = jnp.dot(hbs[r], w_ref[:, nq:], preferred_element_type=F32)
        yield
        k_ref[r] = jnp.concatenate(
            [head(kv[:, i * HEAD_DIM:(i + 1) * HEAD_DIM], kn_ref[...]) for i in range(KV_HEADS)],
            axis=1).astype(BF16)
        ones = jnp.ones((TOK, HEAD_DIM), BF16)
        v_ref[r] = jnp.concatenate(
            [blk for i in range(KV_HEADS)
             for blk in (kv[:, nk + i * HEAD_DIM:nk + (i + 1) * HEAD_DIM].astype(BF16), ones)], axis=1)

    def q_row(r):
        qq = jnp.dot(hbs[r], w_ref[:, :nq], preferred_element_type=F32)
        yield
        q_ref[r] = jnp.concatenate(
            [head(qq[:, i * HEAD_DIM:(i + 1) * HEAD_DIM], qn_ref[...]) * (HEAD_DIM ** -0.5)
             for i in range(ATT_HEADS)], axis=1).astype(BF16)

    _lockstep([kv_row(r) for r in range(nb)])

    @pl.when(latent)
    def _():
        _lockstep([q_row(r) for r in range(nb)])


def _qkv(ncc, x1, gain, mods, w_qkv, qn, kn, cos, sin):
    batch, t, d = x1.shape
    nch = t // TOK
    seq = t - ncc * TOK
    nq = ATT_HEADS * HEAD_DIM
    nk = KV_HEADS * HEAD_DIM
    const = lambda shape: pl.BlockSpec(shape, lambda b, c: (0,) * len(shape))
    lat = lambda b, c: (b, jnp.maximum(c - ncc, 0), 0)
    nb = QKV_ROWS
    return pl.pallas_call(
        functools.partial(_qkv_kernel, ncc),
        grid=(batch // nb, nch),
        in_specs=[pl.BlockSpec((nb, TOK, d), lambda b, c: (b, c, 0)),
                  const((1, d)), mods.spec(1, 0, nb=nb), mods.spec(1, 1, nb=nb),
                  const((d, nq + 2 * nk)), const((1, HEAD_DIM)), const((1, HEAD_DIM)),
                  pl.BlockSpec((TOK, HEAD_DIM), lambda b, c: (jnp.maximum(c - ncc, 0), 0)),
                  pl.BlockSpec((TOK, HEAD_DIM), lambda b, c: (jnp.maximum(c - ncc, 0), 0))],
        out_specs=[pl.BlockSpec((nb, TOK, nq), lat),
                   pl.BlockSpec((nb, TOK, nk), lambda b, c: (b, c, 0)),
                   pl.BlockSpec((nb, TOK, 2 * nk), lambda b, c: (b, c, 0))],
        out_shape=[jax.ShapeDtypeStruct((batch, seq, nq), BF16),
                   jax.ShapeDtypeStruct((batch, t, nk), BF16),
                   jax.ShapeDtypeStruct((batch, t, 2 * nk), BF16)],
        compiler_params=_cparams(("arbitrary", "arbitrary")),
        name="qkv",
    )(x1, gain, mods.table, mods.table, w_qkv, qn, kn, cos, sin)


def _attn_kernel(n_kb, q_ref, k_ref, v_ref, o_ref):
    q = q_ref[0]
    qs = jnp.concatenate([q[:, g * HEAD_DIM:(g + 1) * HEAD_DIM] for g in range(Q_PER_KV)], axis=0)
    kb = k_ref.shape[1] // n_kb
    m = acc = None
    for j in range(n_kb):
        s = lax.dot_general(qs, k_ref[0, j * kb:(j + 1) * kb, :], (((1,), (1,)), ((), ())),
                            preferred_element_type=F32)
        m_blk = jnp.max(s, axis=-1, keepdims=True)
        m_new = m_blk if m is None else jnp.maximum(m, m_blk)
        p = jnp.exp((s - m_new).astype(BF16))
        pv = jnp.dot(p, v_ref[0, j * kb:(j + 1) * kb, :], preferred_element_type=F32)
        acc = pv if acc is None else jnp.exp(m - m_new) * acc + pv
        m = m_new
    o = acc[:, :HEAD_DIM] / acc[:, HEAD_DIM:HEAD_DIM + 1]
    o_ref[0] = jnp.concatenate([o[g * ATT_Q:(g + 1) * ATT_Q] for g in range(Q_PER_KV)], axis=1).astype(BF16)


def _attention(q, k, v):
    batch, seq, nq = q.shape
    t = k.shape[1]
    gw = Q_PER_KV * HEAD_DIM
    n_kb = 2 if t % (2 * LANES) == 0 else 1
    return pl.pallas_call(
        functools.partial(_attn_kernel, n_kb),
        grid=(batch, KV_HEADS, seq // ATT_Q),
        in_specs=[pl.BlockSpec((1, ATT_Q, gw), lambda b, h, i: (b, i, h)),
                  pl.BlockSpec((1, t, HEAD_DIM), lambda b, h, i: (b, 0, h)),
                  pl.BlockSpec((1, t, 2 * HEAD_DIM), lambda b, h, i: (b, 0, h))],
        out_specs=pl.BlockSpec((1, ATT_Q, gw), lambda b, h, i: (b, i, h)),
        out_shape=jax.ShapeDtypeStruct((batch, seq, nq), BF16),
        compiler_params=_cparams(("arbitrary", "arbitrary", "arbitrary")),
        name="attention",
    )(q, k, v)


def _block_diag(w):
    nb, bs, _ = w.shape
    eye = jnp.eye(nb, dtype=w.dtype)
    return (eye[:, None, :, None] * w[:, :, None, :]).reshape(nb * bs, nb * bs)


def _router_weights(wg, bg, we, be):
    d = wg.shape[0]
    n = N_GROUPS + N_EXPERTS
    wr = jnp.zeros((ROUTER_ROWS, d), F32).at[:N_GROUPS].set(wg.T).at[N_GROUPS:n].set(we.T)
    br = jnp.zeros((ROUTER_ROWS, 1), F32).at[:N_GROUPS, 0].set(bg).at[N_GROUPS:n, 0].set(be)
    w_hi = wr.astype(BF16)
    w_lo = (wr - w_hi.astype(F32)).astype(BF16)
    return jnp.concatenate([w_hi, w_lo], axis=0), br


def _rope_tables(seq):
    rows = seq // GRID_W
    row = np.repeat(np.arange(rows, dtype=np.float32), GRID_W)
    col = np.tile(np.arange(GRID_W, dtype=np.float32), rows)
    ppa = HEAD_DIM // 4
    freqs = (ROPE_THETA ** (-np.arange(ppa, dtype=np.float32) / ppa)).astype(np.float32)
    ang = np.concatenate([row[:, None] * freqs, col[:, None] * freqs], axis=-1)
    cos, sin = np.cos(ang), np.sin(ang)
    return (jnp.asarray(np.concatenate([cos, cos], axis=-1), F32),
            jnp.asarray(np.concatenate([-sin, sin], axis=-1), F32))


_HALF_SPLIT = np.concatenate([np.arange(0, HEAD_DIM, 2), np.arange(1, HEAD_DIM, 2)])


def kernel(x, c, ctx, c_ctx, norm1, norm2, w_ada, b_ada, ev_w_in, ev_conv_w, ev_conv_b, ev_lru_wa, ev_lru_ba, ev_lru_wi, ev_lru_bi, ev_lru_lam, ev_gla_wg, ev_gla_bg, ev_gla_norm, ev_w_out, od_w_qkv, od_q_norm, od_k_norm, od_w_o, moe_wg, moe_bg, moe_we, moe_be, moe_w1, moe_w3, moe_w2):
    batch, seq, d = x.shape
    ctx_len = ctx.shape[1]
    assert seq % TOK == 0 and ctx_len % TOK == 0 and d % GLA_VAL == 0 and seq % GRID_W == 0
    ncc = ctx_len // TOK
    lw = d // 2

    assert batch % TOK_ROWS == 0
    rows = -(-(batch + TOK_ROWS) // SUBLANES) * SUBLANES
    cv = jnp.zeros((rows, d), F32).at[:batch].set(c).at[batch:batch + TOK_ROWS].set(c_ctx)
    table = _adaln(cv, w_ada, b_ada)
    depth = table.shape[0]
    table = table.reshape(depth, rows, 6, d).transpose(0, 2, 1, 3).reshape(depth * 6, rows, 1, d)
    mods = _Mods(table, batch, ncc)

    w_in = ev_w_in[0]
    nm = 2 * lw + 2 * GLA_KEY + 2 * GLA_VAL
    w_main = w_in[:, :nm].astype(BF16)
    w_lr = jnp.zeros((d, LANES), F32).at[:, :2 * GLA_RANK].set(w_in[:, nm:]).astype(BF16)
    pm, lr = _inproj(x, ctx, norm1[0][None], mods, w_main, w_lr, ev_conv_w[0], ev_conv_b[0][None], lw)

    mixed = []
    for dr in range(2):
        w_gate = jnp.concatenate([_block_diag(ev_lru_wa[0, dr]), _block_diag(ev_lru_wi[0, dr])], axis=1).astype(BF16)
        b_gate = jnp.concatenate([ev_lru_ba[0, dr], ev_lru_bi[0, dr]])[None]
        wg_pad = jnp.zeros((LANES, GLA_KEY), F32).at[dr * GLA_RANK:(dr + 1) * GLA_RANK].set(ev_gla_wg[0, dr]).astype(BF16)
        mixed.append(_seqmix(dr == 1, ncc, pm, lr, w_gate, b_gate,
                             ev_lru_lam[0, dr][None], wg_pad, ev_gla_bg[0, dr][None], lw))
    (lru_f, gla_f), (lru_r, gla_r) = mixed

    wr0, br0 = _router_weights(moe_wg[0], moe_bg[0], moe_we[0], moe_be[0])
    xmid0, h2e0, cr0, cnt0 = _post0(ncc, lru_f, lru_r, gla_f, gla_r, pm, x, ctx, mods, norm2[0][None],
                                    ev_gla_norm[0][None], ev_w_out[0].astype(BF16), wr0, br0, lw)
    x1 = _moe_block(h2e0, cr0, cnt0, xmid0, mods.spec(0, 5), mods.table, moe_w1[0], moe_w3[0], moe_w2[0])

    nq = ATT_HEADS * HEAD_DIM
    nk = KV_HEADS * HEAD_DIM
    perm = np.concatenate([h * HEAD_DIM + _HALF_SPLIT for h in range(ATT_HEADS + KV_HEADS)]
                          + [np.arange(nq + nk, nq + 2 * nk)])
    w_qkv = od_w_qkv[0][:, perm].astype(BF16)
    cos, sin = _rope_tables(seq)
    q, k, v = _qkv(ncc, x1, norm1[1][None], mods, w_qkv, od_q_norm[0][_HALF_SPLIT][None],
                   od_k_norm[0][_HALF_SPLIT][None], cos, sin)
    att = _attention(q, k, v)
    wr1, br1 = _router_weights(moe_wg[1], moe_bg[1], moe_we[1], moe_be[1])
    xmid1, h2e1, cr1, cnt1 = _post1(ncc, att, x1, mods, norm2[1][None], od_w_o[0].astype(BF16), wr1, br1)
    return _moe_block(h2e1, cr1, cnt1, xmid1, mods.spec(1, 5, True), mods.table,
                      moe_w1[1], moe_w3[1], moe_w2[1])
```

```python
import functools
import itertools

import numpy as np
import jax
import jax.numpy as jnp
from jax import lax
from jax.experimental import pallas as pl
from jax.experimental.pallas import tpu as pltpu

F32 = jnp.float32
BF16 = jnp.bfloat16
I32 = jnp.int32
U32 = jnp.uint32
HIGHEST = lax.Precision.HIGHEST

EPS = 1e-6
GRID_W = 64
LRU_BLOCKS = 8
LRU_C = 8.0
GLA_HEADS = 4
GLA_DK = 64
GLA_DV = 128
GLA_KEY = GLA_HEADS * GLA_DK
GLA_VAL = GLA_HEADS * GLA_DV
GLA_RANK = 16
GLA_TAU = 16.0
GLA_CHUNK = 64
_CHUNK_SHIFT = GLA_CHUNK.bit_length() - 1
_DK_SHIFT = GLA_DK.bit_length() - 1
ATT_HEADS = 8
KV_HEADS = 2
Q_PER_KV = ATT_HEADS // KV_HEADS
HEAD_DIM = 128
ROPE_THETA = 10000.0
N_GROUPS = 4
EXPERTS_PER_GROUP = 4
N_EXPERTS = N_GROUPS * EXPERTS_PER_GROUP
PAIRS_PER_GROUP = 6
N_CLASSES = N_GROUPS * PAIRS_PER_GROUP

LANES = 128
SUBLANES = 8
TOK = 256
MOE_TILE = 512
ATT_Q = 256
PERM_ROWS = 512
TOK_ROWS = 4
QKV_ROWS = 2
SEQ_ROWS = 4
WEXT = LANES
ROUTER_ROWS = 32
VMEM_LIMIT = 56 * 1024 * 1024


def _cparams(sem):
    return pltpu.CompilerParams(dimension_semantics=sem, vmem_limit_bytes=VMEM_LIMIT)


def _rms(x, g):
    return x * lax.rsqrt(jnp.mean(x * x, axis=-1, keepdims=True) + EPS) * g


def _sigmoid(x):
    return 1.0 / (1.0 + jnp.exp(-x))


def _silu(x):
    return x * _sigmoid(x)


def _gelu_tanh(x):
    return 0.5 * x * (1.0 + jnp.tanh(np.sqrt(2.0 / np.pi).astype(np.float32) * (x + 0.044715 * (x * x * x))))


def _pack_bf16_pairs(xb):
    k = xb.shape[-1] // 2
    lo = lax.shift_right_logical(pltpu.bitcast(xb[:, :k].astype(F32), U32), jnp.uint32(16))
    hi = pltpu.bitcast(xb[:, k:].astype(F32), U32) & jnp.uint32(0xFFFF0000)
    return hi | lo


def _unpack_bf16_pairs(words):
    lo = pltpu.bitcast(lax.shift_left(words, jnp.uint32(16)), F32).astype(BF16)
    hi = pltpu.bitcast(words & jnp.uint32(0xFFFF0000), F32).astype(BF16)
    return jnp.concatenate([lo, hi], axis=1)


def _softplus(x):
    return jnp.maximum(x, 0.0) + jnp.log(1.0 + jnp.exp(-jnp.abs(x)))


def _log_sigmoid(x):
    return -_softplus(-x)


def _adaln_kernel(cv_ref, w_ref, b_ref, o_ref):
    s = _silu(cv_ref[...])
    o_ref[0] = jnp.dot(s, w_ref[0], precision=HIGHEST, preferred_element_type=F32) + b_ref[0]


def _adaln(cv, w_ada, b_ada):
    depth, d, n6 = w_ada.shape
    rows = cv.shape[0]
    tn = 6 * d // 4
    return pl.pallas_call(
        _adaln_kernel,
        grid=(depth, n6 // tn),
        in_specs=[pl.BlockSpec((rows, d), lambda l, j: (0, 0)),
                  pl.BlockSpec((1, d, tn), lambda l, j: (l, 0, j)),
                  pl.BlockSpec((1, 1, tn), lambda l, j: (l, 0, j))],
        out_specs=pl.BlockSpec((1, rows, tn), lambda l, j: (l, 0, j)),
        out_shape=jax.ShapeDtypeStruct((depth, rows, n6), F32),
        compiler_params=_cparams(("arbitrary", "arbitrary")),
        name="adaln",
    )(cv, w_ada, b_ada.reshape(depth, 1, n6))


class _Mods:
    def __init__(self, table, batch, ncc):
        self.table, self.batch, self.ncc = table, batch, ncc
        self.d = table.shape[-1]

    def spec(self, layer, j, latent_only=False, nb=1):
        batch, ncc = self.batch, self.ncc

        def imap(b, c):
            r = b if latent_only else jnp.where(c < ncc, batch // nb, b)
            return (layer * 6 + j, r, 0, 0)

        return pl.BlockSpec((1, nb, 1, self.d), imap)


def _lockstep(generators):
    for _ in itertools.zip_longest(*generators):
        pass


def _inproj_kernel(ncc, nch, lw, x_ref, c_ref, xp_ref, xn_ref, cp_ref, cn_ref, g_ref, sh_ref, sc_ref,
                   w_ref, wlr_ref, cw_ref, cb_ref, pm_ref, lr_ref):
    _lockstep([_inproj_row(ncc, nch, lw, i, x_ref, c_ref, xp_ref, xn_ref, cp_ref, cn_ref, g_ref, sh_ref, sc_ref,
                           w_ref, wlr_ref, cw_ref, cb_ref, pm_ref, lr_ref) for i in range(x_ref.shape[0])])


def _inproj_row(ncc, nch, lw, i, x_ref, c_ref, xp_ref, xn_ref, cp_ref, cn_ref, g_ref, sh_ref, sc_ref,
                w_ref, wlr_ref, cw_ref, cb_ref, pm_ref, lr_ref):
    c = pl.program_id(1)
    is_ctx = c < ncc

    def modulated(rows):
        return (_rms(rows, g_ref[...]) * (1.0 + sc_ref[0, i]) + sh_ref[0, i]).astype(BF16)

    hb = modulated(jnp.where(is_ctx, c_ref[i], x_ref[i]))
    pm = jnp.dot(hb, w_ref[...], preferred_element_type=F32)
    yield
    lr_ref[i] = jnp.dot(hb, wlr_ref[...], preferred_element_type=F32).astype(BF16)

    halo = jnp.concatenate([jnp.where(is_ctx, cp_ref[i], xp_ref[i]), jnp.where(is_ctx, cn_ref[i], xn_ref[i])], axis=0)
    xa_halo = jnp.dot(modulated(halo), w_ref[:, :lw], preferred_element_type=F32)
    yield
    has_prev = jnp.logical_and(c != 0, c != ncc)
    has_next = jnp.logical_and(c != ncc - 1, c != nch - 1)
    p2 = jnp.where(has_prev, xa_halo[SUBLANES - 2:SUBLANES - 1], 0.0)
    p1 = jnp.where(has_prev, xa_halo[SUBLANES - 1:SUBLANES], 0.0)
    n0 = jnp.where(has_next, xa_halo[SUBLANES:SUBLANES + 1], 0.0)
    xa = pm[:, :lw]
    row = lax.broadcasted_iota(I32, xa.shape, 0)
    x_m1 = jnp.where(row == 0, p1, pltpu.roll(xa, 1, 0))
    x_m2 = jnp.where(row == 0, p2, jnp.where(row == 1, p1, pltpu.roll(xa, 2, 0)))
    x_p1 = jnp.where(row == TOK - 1, n0, pltpu.roll(xa, TOK - 1, 0))
    cw = cw_ref[...]
    u = cw[0:1] * x_m2 + cw[1:2] * x_m1 + cw[2:3] * xa + cw[3:4] * x_p1 + cb_ref[...]
    pm_ref[i, :, :lw] = u.astype(BF16)
    pm_ref[i, :, lw:] = pm[:, lw:].astype(BF16)


def _inproj(x, ctx, gain, mods, w_main, w_lr, conv_w, conv_b, lw):
    batch, seq, d = x.shape
    ncc = ctx.shape[1] // TOK
    nl = seq // TOK
    nch = ncc + nl
    nm = w_main.shape[1]
    hb = TOK // SUBLANES
    nb = TOK_ROWS
    return pl.pallas_call(
        functools.partial(_inproj_kernel, ncc, nch, lw),
        grid=(batch // nb, nch),
        in_specs=[pl.BlockSpec((nb, TOK, d), lambda b, c: (b, jnp.maximum(c - ncc, 0), 0)),
                  pl.BlockSpec((nb, TOK, d), lambda b, c: (b, jnp.minimum(c, ncc - 1), 0)),
                  pl.BlockSpec((nb, SUBLANES, d), lambda b, c: (b, jnp.maximum((c - ncc) * hb - 1, 0), 0)),
                  pl.BlockSpec((nb, SUBLANES, d),
                               lambda b, c: (b, jnp.clip((c - ncc + 1) * hb, 0, nl * hb - 1), 0)),
                  pl.BlockSpec((nb, SUBLANES, d), lambda b, c: (b, jnp.clip(c * hb - 1, 0, ncc * hb - 1), 0)),
                  pl.BlockSpec((nb, SUBLANES, d), lambda b, c: (b, jnp.clip((c + 1) * hb, 0, ncc * hb - 1), 0)),
                  pl.BlockSpec((1, d), lambda b, c: (0, 0)),
                  mods.spec(0, 0, nb=nb), mods.spec(0, 1, nb=nb),
                  pl.BlockSpec((d, nm), lambda b, c: (0, 0)),
                  pl.BlockSpec((d, LANES), lambda b, c: (0, 0)),
                  pl.BlockSpec((4, lw), lambda b, c: (0, 0)),
                  pl.BlockSpec((1, lw), lambda b, c: (0, 0))],
        out_specs=[pl.BlockSpec((nb, TOK, nm), lambda b, c: (b, c, 0)),
                   pl.BlockSpec((nb, TOK, LANES), lambda b, c: (b, c, 0))],
        out_shape=[jax.ShapeDtypeStruct((batch, nch * TOK, nm), BF16),
                   jax.ShapeDtypeStruct((batch, nch * TOK, LANES), BF16)],
        compiler_params=_cparams(("arbitrary", "arbitrary")),
        name="inproj",
    )(x, ctx, x, x, ctx, ctx, gain, mods.table, mods.table, w_main, w_lr, conv_w, conv_b)


def _lru_scan(a, b, h0, rev):
    n_groups = TOK // SUBLANES
    a = a.reshape(n_groups, SUBLANES, a.shape[-1])
    b = b.reshape(a.shape)
    sub = lax.broadcasted_iota(I32, a.shape, 1)
    for dist in (1, 2, 4):
        shift = SUBLANES - dist if rev else dist
        a_s = pltpu.roll(a, shift, 1)
        b_s = pltpu.roll(b, shift, 1)
        m = (sub < SUBLANES - dist) if rev else (sub >= dist)
        b = jnp.where(m, a * b_s + b, b)
        a = jnp.where(m, a * a_s, a)
    order = range(n_groups - 1, -1, -1) if rev else range(n_groups)
    outs = [None] * n_groups
    h = h0
    for r in order:
        hr = a[r] * h + b[r]
        outs[r] = hr
        h = hr[0:1] if rev else hr[SUBLANES - 1:SUBLANES]
    return jnp.concatenate(outs, axis=0), h


def _seqmix_kernel(rev, *refs):
    h_scr, s_scr = refs[-2:]

    @pl.when(pl.program_id(1) == 0)
    def _():
        h_scr[...] = jnp.zeros_like(h_scr)
        s_scr[...] = jnp.zeros_like(s_scr)

    _lockstep([_seqmix_row(rev, i, *refs) for i in range(refs[0].shape[0])])


def _seqmix_row(rev, i, u_ref, q_ref, k_ref, v_ref, lr_ref,
                wgt_ref, bgt_ref, lam_ref, wg_ref, bg_ref, sel_ref,
                lru_ref, gla_ref, h_scr, s_scr):
    lw = u_ref.shape[-1]
    ub = u_ref[i]
    u = ub.astype(F32)
    gates = jnp.dot(ub, wgt_ref[...], preferred_element_type=F32) + bgt_ref[...]
    yield
    r_gate = _sigmoid(gates[:, :lw])
    i_gate = _sigmoid(gates[:, lw:])
    log_a = (-LRU_C) * r_gate * _softplus(-lam_ref[...])
    a = jnp.exp(log_a)
    bb = jnp.sqrt(1.0 - a * a) * (i_gate * u)
    hs, h_last = _lru_scan(a, bb, h_scr[i, 0:1], rev)
    h_scr[i, 0:1] = h_last
    lru_ref[i] = hs.astype(BF16)
    yield

    lg = _log_sigmoid(jnp.dot(lr_ref[i], wg_ref[...], preferred_element_type=F32) + bg_ref[...]) * (1.0 / GLA_TAU)
    lg_hi = lg.astype(BF16)
    lg_lo = (lg - lg_hi.astype(F32)).astype(BF16)
    yield
    cum = jnp.dot(sel_ref[...], jnp.concatenate([lg_hi, lg_lo], axis=1), preferred_element_type=F32)
    yield
    g = cum[:TOK, :GLA_KEY] + cum[:TOK, GLA_KEY:]
    tot = cum[TOK:, :GLA_KEY] + cum[TOK:, GLA_KEY:]
    q = q_ref[i].astype(F32)
    k = k_ref[i].astype(F32)
    dec_all = jnp.exp(tot)
    k_neg = k * jnp.exp(-g)
    qg = (q * jnp.exp(g) * (GLA_DK ** -0.5)).astype(BF16)
    kg = k_neg.astype(BF16)
    kd = (k_neg * dec_all).astype(BF16)
    v = v_ref[i]

    lane = lax.broadcasted_iota(I32, (1, GLA_KEY), 1) >> _DK_SHIFT
    head_masks = [(lane == h) for h in range(GLA_HEADS)]
    n_stack = GLA_HEADS * GLA_CHUNK
    ai = lax.broadcasted_iota(I32, (n_stack, GLA_CHUNK), 0) & (GLA_CHUNK - 1)
    aj = lax.broadcasted_iota(I32, (n_stack, GLA_CHUNK), 1)
    causal = (aj >= ai) if rev else (aj <= ai)

    n_sub = TOK // GLA_CHUNK
    order = range(n_sub - 1, -1, -1) if rev else range(n_sub)
    outs = [None] * n_sub
    st = s_scr[i]
    for n in order:
        sl = slice(n * GLA_CHUNK, (n + 1) * GLA_CHUNK)
        qn = qg[sl]
        qs = jnp.concatenate([jnp.where(head_masks[h], qn, jnp.zeros_like(qn)) for h in range(GLA_HEADS)], axis=0)
        att = lax.dot_general(qs, kg[sl], (((1,), (1,)), ((), ())), preferred_element_type=F32)
        yield
        att = jnp.where(causal, att, 0.0).astype(BF16)
        o_all = jnp.dot(att, v[sl], preferred_element_type=F32)
        yield
        o_int = lax.dot_general(qs, st.astype(BF16), (((1,), (1,)), ((), ())),
                                preferred_element_type=F32)
        yield
        outs[n] = jnp.concatenate(
            [o_all[h * GLA_CHUNK:(h + 1) * GLA_CHUNK, h * GLA_DV:(h + 1) * GLA_DV]
             + o_int[h * GLA_CHUNK:(h + 1) * GLA_CHUNK] for h in range(GLA_HEADS)], axis=1)
        vn, kdn = v[sl], kd[sl]
        v_stack = jnp.concatenate([vn[:, h * GLA_DV:(h + 1) * GLA_DV] for h in range(GLA_HEADS)], axis=0)
        kd_stack = jnp.concatenate([jnp.where(head_masks[h], kdn, jnp.zeros_like(kdn)) for h in range(GLA_HEADS)],
                                   axis=0)
        ds = lax.dot_general(v_stack, kd_stack, (((0,), (0,)), ((), ())), preferred_element_type=F32)
        yield
        st = dec_all[n * GLA_CHUNK:n * GLA_CHUNK + 1] * st + ds
    s_scr[i] = st
    gla_ref[i] = jnp.concatenate(outs, axis=0).astype(BF16)


def _cumsum_selector(rev):
    r = np.arange(TOK)[:, None]
    c = np.arange(TOK)[None, :]
    same = (r // GLA_CHUNK) == (c // GLA_CHUNK)
    tri = same & ((c >= r) if rev else (c <= r))
    return jnp.asarray(np.concatenate([tri, same], axis=0), BF16)


def _seqmix(rev, ncc, pm, lr, w_gate, b_gate, lam, wg_pad, bg, lw):
    batch, t, _ = pm.shape
    nch = t // TOK
    nl = nch - ncc

    def chunk(s):
        if rev:
            return jnp.where(s < ncc, ncc - 1 - s, ncc + (nl - 1) - (s - ncc))
        return s

    qcol = 2 * lw // GLA_KEY
    vcol = (2 * lw + 2 * GLA_KEY) // GLA_VAL
    const = lambda shape: pl.BlockSpec(shape, lambda b, s: (0,) * len(shape))
    nb = SEQ_ROWS if batch % SEQ_ROWS == 0 else 1
    return pl.pallas_call(
        functools.partial(_seqmix_kernel, rev),
        grid=(batch // nb, nch),
        in_specs=[pl.BlockSpec((nb, TOK, lw), lambda b, s: (b, chunk(s), 0)),
                  pl.BlockSpec((nb, TOK, GLA_KEY), lambda b, s: (b, chunk(s), qcol)),
                  pl.BlockSpec((nb, TOK, GLA_KEY), lambda b, s: (b, chunk(s), qcol + 1)),
                  pl.BlockSpec((nb, TOK, GLA_VAL), lambda b, s: (b, chunk(s), vcol)),
                  pl.BlockSpec((nb, TOK, LANES), lambda b, s: (b, chunk(s), 0)),
                  const((lw, 2 * lw)), const((1, 2 * lw)), const((1, lw)),
                  const((LANES, GLA_KEY)), const((1, GLA_KEY)), const((2 * TOK, TOK))],
        out_specs=[pl.BlockSpec((nb, TOK, lw), lambda b, s: (b, chunk(s), 0)),
                   pl.BlockSpec((nb, TOK, GLA_VAL), lambda b, s: (b, chunk(s), 0))],
        out_shape=[jax.ShapeDtypeStruct((batch, t, lw), BF16),
                   jax.ShapeDtypeStruct((batch, t, GLA_VAL), BF16)],
        scratch_shapes=[pltpu.VMEM((nb, SUBLANES, lw), F32), pltpu.VMEM((nb, GLA_DV, GLA_KEY), F32)],
        compiler_params=_cparams(("arbitrary", "arbitrary")),
        name="seqmix_rev" if rev else "seqmix_fwd",
    )(pm, pm, pm, pm, lr, w_gate, b_gate, lam, wg_pad, bg, _cumsum_selector(rev))


def _route_epilogue(i, first, xm, n2_ref, sh2_ref, sc2_ref, wr_ref, br_ref,
                    xmid_ref, h2_ref, cr_ref, cnt_ref, carry):
    d = xm.shape[-1]
    xmid_ref[i] = xm
    h2 = _rms(xm, n2_ref[...]) * (1.0 + sc2_ref[0, i]) + sh2_ref[0, i]
    h_hi = h2.astype(BF16)
    h_lo = (h2 - h_hi.astype(F32)).astype(BF16)
    nt = (((1,), (1,)), ((), ()))
    o1 = lax.dot_general(wr_ref[...], h_hi, nt, preferred_element_type=F32)
    o2 = lax.dot_general(wr_ref[:ROUTER_ROWS], h_lo, nt, preferred_element_type=F32)
    yield
    logits = o1[:ROUTER_ROWS] + o1[ROUTER_ROWS:] + o2 + br_ref[...]

    def col(j):
        return logits[j:j + 1]

    lgs = [col(g) for g in range(N_GROUPS)]
    gmax = functools.reduce(jnp.maximum, lgs)
    gi = jnp.where(lgs[0] == gmax, 0, jnp.where(lgs[1] == gmax, 1, jnp.where(lgs[2] == gmax, 2, 3)))
    w_group = 1.0 / functools.reduce(lambda p, q: p + q, [jnp.exp(l - gmax) for l in lgs])
    es = []
    for j in range(EXPERTS_PER_GROUP):
        acc = jnp.zeros_like(gmax)
        for g in range(N_GROUPS):
            acc = acc + jnp.where(gi == g, col(N_GROUPS + g * EXPERTS_PER_GROUP + j), 0.0)
        es.append(acc)
    m1 = functools.reduce(jnp.maximum, es)
    i1 = jnp.where(es[0] == m1, 0, jnp.where(es[1] == m1, 1, jnp.where(es[2] == m1, 2, 3)))
    rest = [jnp.where(i1 == j, -jnp.inf, es[j]) for j in range(EXPERTS_PER_GROUP)]
    m2 = functools.reduce(jnp.maximum, rest)
    i2 = jnp.where(rest[0] == m2, 0, jnp.where(rest[1] == m2, 1, jnp.where(rest[2] == m2, 2, 3)))
    e2 = jnp.exp(m2 - m1)
    w1 = w_group / (1.0 + e2)
    w2 = w_group * e2 / (1.0 + e2)
    first_lo = i1 < i2
    lo = jnp.where(first_lo, i1, i2)
    hi = jnp.where(first_lo, i2, i1)
    w_lo = jnp.where(first_lo, w1, w2)
    w_hi = jnp.where(first_lo, w2, w1)
    pidx = jnp.where(lo == 0, hi - 1, jnp.where(lo == 1, hi + 1, 5))
    cls = gi * PAIRS_PER_GROUP + pidx

    def terms(w):
        t1 = w.astype(BF16).astype(F32)
        t2 = (w - t1).astype(BF16).astype(F32)
        t3 = (w - t1 - t2).astype(BF16).astype(F32)
        return [t1, t2, t3]

    sub = lax.broadcasted_iota(I32, (2 * SUBLANES, TOK), 0)
    stacked = jnp.zeros((2 * SUBLANES, TOK), F32)
    for r, term in enumerate(terms(w_lo) + terms(w_hi)):
        stacked = jnp.where(sub == r, term, stacked)
    er = lax.broadcasted_iota(I32, (2 * SUBLANES, LANES), 0)
    el = lax.broadcasted_iota(I32, (2 * SUBLANES, LANES), 1)
    place = jnp.logical_or(jnp.logical_and(er < 3, el == 0),
                           jnp.logical_and(jnp.logical_and(er >= 3, er < 6), el == 1))
    w_cols = lax.dot_general(stacked.astype(BF16), place.astype(BF16), (((0,), (0,)), ((), ())),
                             preferred_element_type=F32)
    h2_ref[i, :, :d // 2] = _pack_bf16_pairs(h_hi)
    h2_ref[i, :, d // 2:] = pltpu.bitcast(w_cols, U32)

    krow = lax.broadcasted_iota(I32, (ROUTER_ROWS, TOK), 0)
    onehot = krow == cls
    ri = lax.broadcasted_iota(I32, (TOK, TOK), 0)
    ci = lax.broadcasted_iota(I32, (TOK, TOK), 1)
    triu = (ri <= ci).astype(BF16)
    pref = jnp.dot(onehot.astype(BF16), triu, preferred_element_type=F32)
    yield

    if i == 0:
        @pl.when(first)
        def _():
            carry[...] = jnp.zeros_like(carry)

    base = carry[:, 0:1]
    rank = jnp.sum(jnp.where(onehot, pref - 1.0 + base, 0.0), axis=0, keepdims=True)
    new_carry = jnp.broadcast_to(base + jnp.sum(onehot.astype(F32), axis=1, keepdims=True), carry.shape)
    carry[...] = new_carry
    cnt_ref[...] = new_carry
    sub8 = lax.broadcasted_iota(I32, (SUBLANES, TOK), 0)
    cr_ref[i, 0] = jnp.where(sub8 == 0, cls, jnp.where(sub8 == 1, rank.astype(I32), 0))


def _post0_kernel(ncc, *refs):
    _lockstep([_post0_row(ncc, i, *refs) for i in range(refs[0].shape[0])])


def _post0_row(ncc, i, l0_ref, l1_ref, g0_ref, g1_ref, ga_ref, r_ref, x_ref, c_ref,
               gate_ref, sh2_ref, sc2_ref, n2_ref, gn_ref, wout_ref, wr_ref, br_ref,
               xmid_ref, h2_ref, cr_ref, cnt_ref, carry):
    b, c = pl.program_id(0), pl.program_id(1)
    lru = l0_ref[i].astype(F32) + l1_ref[i].astype(F32)
    ya = lru * _gelu_tanh(ga_ref[i].astype(F32))
    gla = g0_ref[i].astype(F32) + g1_ref[i].astype(F32)
    parts = []
    for h in range(GLA_HEADS):
        parts.append(_rms(gla[:, h * GLA_DV:(h + 1) * GLA_DV], gn_ref[...]))
    yb = jnp.concatenate(parts, axis=1) * _silu(r_ref[i].astype(F32))
    ycat = jnp.concatenate([ya, yb], axis=1).astype(BF16)
    y = jnp.dot(ycat, wout_ref[...], preferred_element_type=F32)
    yield
    x0 = jnp.where(c < ncc, c_ref[i], x_ref[i])
    xm = x0 + gate_ref[0, i] * y
    yield from _route_epilogue(i, jnp.logical_and(b == 0, c == 0), xm, n2_ref, sh2_ref, sc2_ref, wr_ref, br_ref,
                               xmid_ref, h2_ref, cr_ref, cnt_ref, carry)


def _post1_kernel(*refs):
    _lockstep([_post1_row(i, *refs) for i in range(refs[0].shape[0])])


def _post1_row(i, o_ref, x_ref, gate_ref, sh2_ref, sc2_ref, n2_ref, wo_ref, wr_ref, br_ref,
               xmid_ref, h2_ref, cr_ref, cnt_ref, carry):
    b, c = pl.program_id(0), pl.program_id(1)
    y = jnp.dot(o_ref[i], wo_ref[...], preferred_element_type=F32)
    yield
    xm = x_ref[i] + gate_ref[0, i] * y
    yield from _route_epilogue(i, jnp.logical_and(b == 0, c == 0), xm, n2_ref, sh2_ref, sc2_ref, wr_ref, br_ref,
                               xmid_ref, h2_ref, cr_ref, cnt_ref, carry)


def _route_out(batch, nch, d, nb):
    specs = [pl.BlockSpec((nb, TOK, d), lambda b, c: (b, c, 0)),
             pl.BlockSpec((nb, TOK, d // 2 + WEXT), lambda b, c: (b, c, 0)),
             pl.BlockSpec((nb, 1, SUBLANES, TOK), lambda b, c: (b, c, 0, 0)),
             pl.BlockSpec((ROUTER_ROWS, LANES), lambda b, c: (0, 0))]
    shapes = [jax.ShapeDtypeStruct((batch, nch * TOK, d), F32),
              jax.ShapeDtypeStruct((batch, nch * TOK, d // 2 + WEXT), U32),
              jax.ShapeDtypeStruct((batch, nch, SUBLANES, TOK), I32),
              jax.ShapeDtypeStruct((ROUTER_ROWS, LANES), F32)]
    return specs, shapes


def _post0(ncc, lru_f, lru_r, gla_f, gla_r, pm, x, ctx, mods, n2, gn, w_out, wr, br, lw):
    batch, t, _ = pm.shape
    d = x.shape[-1]
    nch = t // TOK
    rcol = (2 * lw + 2 * GLA_KEY + GLA_VAL) // GLA_VAL
    nb = TOK_ROWS
    tokspec = lambda w: pl.BlockSpec((nb, TOK, w), lambda b, c: (b, c, 0))
    const = lambda shape: pl.BlockSpec(shape, lambda b, c: (0,) * len(shape))
    out_specs, out_shape = _route_out(batch, nch, d, nb)
    return pl.pallas_call(
        functools.partial(_post0_kernel, ncc),
        grid=(batch // nb, nch),
        in_specs=[tokspec(lw), tokspec(lw), tokspec(GLA_VAL), tokspec(GLA_VAL),
                  pl.BlockSpec((nb, TOK, lw), lambda b, c: (b, c, 1)),
                  pl.BlockSpec((nb, TOK, GLA_VAL), lambda b, c: (b, c, rcol)),
                  pl.BlockSpec((nb, TOK, d), lambda b, c: (b, jnp.maximum(c - ncc, 0), 0)),
                  pl.BlockSpec((nb, TOK, d), lambda b, c: (b, jnp.minimum(c, ncc - 1), 0)),
                  mods.spec(0, 2, nb=nb), mods.spec(0, 3, nb=nb), mods.spec(0, 4, nb=nb),
                  const((1, d)), const((1, GLA_DV)), const((lw + GLA_VAL, d)),
                  const((2 * ROUTER_ROWS, d)), const((ROUTER_ROWS, 1))],
        out_specs=out_specs, out_shape=out_shape,
        scratch_shapes=[pltpu.VMEM((ROUTER_ROWS, LANES), F32)],
        compiler_params=_cparams(("arbitrary", "arbitrary")),
        name="post0",
    )(lru_f, lru_r, gla_f, gla_r, pm, pm, x, ctx, mods.table, mods.table, mods.table, n2, gn, w_out, wr, br)


def _post1(ncc, att, x1, mods, n2, w_o, wr, br):
    batch, seq, aw = att.shape
    d = x1.shape[-1]
    nl = seq // TOK
    const = lambda shape: pl.BlockSpec(shape, lambda b, c: (0,) * len(shape))
    nb = TOK_ROWS
    out_specs, out_shape = _route_out(batch, nl, d, nb)
    return pl.pallas_call(
        _post1_kernel,
        grid=(batch // nb, nl),
        in_specs=[pl.BlockSpec((nb, TOK, aw), lambda b, c: (b, c, 0)),
                  pl.BlockSpec((nb, TOK, d), lambda b, c: (b, c + ncc, 0)),
                  mods.spec(1, 2, True, nb), mods.spec(1, 3, True, nb), mods.spec(1, 4, True, nb),
                  const((1, d)), const((aw, d)), const((2 * ROUTER_ROWS, d)), const((ROUTER_ROWS, 1))],
        out_specs=out_specs, out_shape=out_shape,
        scratch_shapes=[pltpu.VMEM((ROUTER_ROWS, LANES), F32)],
        compiler_params=_cparams(("arbitrary", "arbitrary")),
        name="post1",
    )(att, x1, mods.table, mods.table, mods.table, n2, w_o, wr, br)


_PAIR_LO = np.array([0, 0, 0, 1, 1, 2], np.int32)
_PAIR_HI = np.array([1, 2, 3, 2, 3, 3], np.int32)


def _plan(cr, cnt, n_tiles):
    cls, rank = cr[:, :, 0, :], cr[:, :, 1, :]
    counts = cnt[:N_CLASSES, 0].astype(I32)
    tiles = (counts + MOE_TILE - 1) // MOE_TILE
    ends = jnp.cumsum(tiles)
    offs = (ends - tiles) * MOE_TILE
    pos = rank
    for c in range(N_CLASSES):
        pos = pos + jnp.where(cls == c, offs[c], 0)
    pos = pos.reshape(-1)
    tile_ids = jnp.arange(n_tiles, dtype=I32)
    tile_cls = jnp.minimum(jnp.sum((tile_ids[:, None] >= ends[None, :]).astype(I32), axis=1), N_CLASSES - 1)
    grp = tile_cls // PAIRS_PER_GROUP
    pair = tile_cls % PAIRS_PER_GROUP
    lo = grp * EXPERTS_PER_GROUP + jnp.asarray(_PAIR_LO)[pair]
    hi = grp * EXPERTS_PER_GROUP + jnp.asarray(_PAIR_HI)[pair]
    return pos.astype(I32), lo.astype(I32), hi.astype(I32)


def _moe_kernel(d, lo_ref, hi_ref, x_ref, w1l, w3l, w2l, w1h, w3h, w2h, y_ref, wbuf, wbuf2):
    i = pl.program_id(0)

    changed = jnp.logical_or(i == 0, jnp.logical_or(lo_ref[i] != lo_ref[jnp.maximum(i - 1, 0)],
                                                    hi_ref[i] != hi_ref[jnp.maximum(i - 1, 0)]))

    @pl.when(changed)
    def _():
        for k, w in enumerate((w1l, w3l, w1h, w3h)):
            wbuf[k] = w[0].astype(BF16)
        wbuf2[0] = w2l[0].astype(BF16)
        wbuf2[1] = w2h[0].astype(BF16)

    x = _unpack_bf16_pairs(x_ref[:, :d // 2])
    wts = pltpu.bitcast(x_ref[:, d // 2:], F32)

    def expert(k1, k3, k2):
        a = jnp.dot(x, wbuf[k1], preferred_element_type=F32)
        b = jnp.dot(x, wbuf[k3], preferred_element_type=F32)
        return jnp.dot((_silu(a) * b).astype(BF16), wbuf2[k2], preferred_element_type=F32)

    y = wts[:, 0:1] * expert(0, 1, 0) + wts[:, 1:2] * expert(2, 3, 1)
    y_ref[...] = _pack_bf16_pairs(y.astype(BF16))


def _moe(xs, lo, hi, w1, w3, w2):
    n_pad, wd = xs.shape
    d = 2 * (wd - WEXT)
    hdim = w1.shape[-1]
    n_tiles = n_pad // MOE_TILE
    wl = lambda i, lo, hi: (lo[i], 0, 0)
    wh = lambda i, lo, hi: (hi[i], 0, 0)
    grid_spec = pltpu.PrefetchScalarGridSpec(
        num_scalar_prefetch=2,
        grid=(n_tiles,),
        in_specs=[pl.BlockSpec((MOE_TILE, wd), lambda i, lo, hi: (i, 0)),
                  pl.BlockSpec((1, d, hdim), wl), pl.BlockSpec((1, d, hdim), wl), pl.BlockSpec((1, hdim, d), wl),
                  pl.BlockSpec((1, d, hdim), wh), pl.BlockSpec((1, d, hdim), wh), pl.BlockSpec((1, hdim, d), wh)],
        out_specs=pl.BlockSpec((MOE_TILE, d // 2), lambda i, lo, hi: (i, 0)),
        scratch_shapes=[pltpu.VMEM((4, d, hdim), BF16), pltpu.VMEM((2, hdim, d), BF16)],
    )
    return pl.pallas_call(
        functools.partial(_moe_kernel, d),
        grid_spec=grid_spec,
        out_shape=jax.ShapeDtypeStruct((n_pad, d // 2), U32),
        compiler_params=_cparams(("arbitrary",)),
        name="moe_experts",
    )(lo, hi, xs, w1, w3, w2, w1, w3, w2)


def _row_dma_loops(tile_rows, make_copy, index):
    def issue(grp, carry):
        for sub in range(SUBLANES):
            make_copy(grp, sub, index(grp * SUBLANES + sub)).start()
        return carry

    def drain(grp, carry):
        for _ in range(SUBLANES):
            make_copy(0, 0, 0).wait()
        return carry

    n_groups = tile_rows // SUBLANES

    def issue_all():
        for grp in range(n_groups):
            issue(grp, 0)

    return issue_all, (lambda: lax.fori_loop(0, n_groups, drain, 0))


def _scatter_rows_kernel(rows, pos_ref, src_ref, init_ref, dst_ref, sem):
    del init_ref
    issue, drain = _row_dma_loops(
        rows,
        lambda grp, sub, p: pltpu.make_async_copy(src_ref.at[grp, pl.ds(sub, 1)], dst_ref.at[pl.ds(p, 1)], sem),
        lambda j: pos_ref[0, 0, j])
    issue()
    drain()


def _scatter_rows(src, pos, n_pad):
    n, w = src.shape
    rows = PERM_ROWS if n % PERM_ROWS == 0 else TOK
    steps = n // rows
    return pl.pallas_call(
        functools.partial(_scatter_rows_kernel, rows),
        grid=(steps,),
        in_specs=[pl.BlockSpec((1, 1, rows), lambda i: (i, 0, 0), memory_space=pltpu.SMEM),
                  pl.BlockSpec((rows // SUBLANES, SUBLANES, w), lambda i: (i, 0, 0)),
                  pl.BlockSpec(memory_space=pl.ANY)],
        out_specs=pl.BlockSpec(memory_space=pl.ANY),
        out_shape=jax.ShapeDtypeStruct((n_pad, w), src.dtype),
        scratch_shapes=[pltpu.SemaphoreType.DMA(())],
        input_output_aliases={2: 0},
        compiler_params=pltpu.CompilerParams(dimension_semantics=("arbitrary",), has_side_effects=True),
        name="scatter_rows",
    )(pos.reshape(steps, 1, rows), src.reshape(n // SUBLANES, SUBLANES, w), jnp.zeros((n_pad, w), src.dtype))


def _gather_residual_kernel(n_steps, pos_ref, posn_ref, ys_ref, xm_ref, gate_ref, o_ref, buf, sem):
    step = pl.program_id(0) * pl.num_programs(1) + pl.program_id(1)
    slot = step % 2

    def loops(idx_ref, s):
        return _row_dma_loops(
            TOK,
            lambda grp, sub, p: pltpu.make_async_copy(ys_ref.at[pl.ds(p, 1)], buf.at[s, grp, pl.ds(sub, 1)],
                                                      sem.at[s]),
            lambda j: idx_ref[0, 0, j])

    @pl.when(step == 0)
    def _():
        loops(pos_ref, slot)[0]()

    @pl.when(step + 1 < n_steps)
    def _():
        loops(posn_ref, 1 - slot)[0]()

    loops(pos_ref, slot)[1]()
    y = _unpack_bf16_pairs(buf[slot].reshape(TOK, buf.shape[-1])).astype(F32)
    o_ref[0] = xm_ref[0] + gate_ref[0, 0] * y


def _gather_residual(ys, pos, xmid, gate_spec, mod_table):
    batch, t, d = xmid.shape
    nch = t // TOK
    n_steps = batch * nch
    return pl.pallas_call(
        functools.partial(_gather_residual_kernel, n_steps),
        grid=(batch, nch),
        in_specs=[pl.BlockSpec((1, 1, TOK), lambda b, c: (b * nch + c, 0, 0), memory_space=pltpu.SMEM),
                  pl.BlockSpec((1, 1, TOK), lambda b, c: (jnp.minimum(b * nch + c + 1, n_steps - 1), 0, 0),
                               memory_space=pltpu.SMEM),
                  pl.BlockSpec(memory_space=pl.ANY),
                  pl.BlockSpec((1, TOK, d), lambda b, c: (b, c, 0)),
                  gate_spec],
        out_specs=pl.BlockSpec((1, TOK, d), lambda b, c: (b, c, 0)),
        out_shape=jax.ShapeDtypeStruct((batch, t, d), F32),
        scratch_shapes=[pltpu.VMEM((2, TOK // SUBLANES, SUBLANES, d // 2), U32), pltpu.SemaphoreType.DMA((2,))],
        compiler_params=_cparams(("arbitrary", "arbitrary")),
        name="gather_residual",
    )(pos.reshape(n_steps, 1, TOK), pos.reshape(n_steps, 1, TOK), ys, xmid, mod_table)


def _moe_block(h2ext, cr, cnt, xmid, gate_spec, mod_table, w1, w3, w2):
    h2ext = h2ext.reshape(-1, h2ext.shape[-1])
    n = h2ext.shape[0]
    n_tiles = n // MOE_TILE + N_CLASSES
    pos, lo, hi = _plan(cr, cnt, n_tiles)
    xs = _scatter_rows(h2ext, pos, n_tiles * MOE_TILE)
    ys = _moe(xs, lo, hi, w1, w3, w2)
    return _gather_residual(ys, pos, xmid, gate_spec, mod_table)


def _qkv_kernel(ncc, x_ref, g_ref, sh_ref, sc_ref, w_ref, qn_ref, kn_ref, cos_ref, sin_ref,
                q_ref, k_ref, v_ref):
    c = pl.program_id(1)
    latent = c >= ncc
    nb = x_ref.shape[0]
    nq = ATT_HEADS * HEAD_DIM
    nk = KV_HEADS * HEAD_DIM
    cos = cos_ref[...]
    sin = sin_ref[...]
    hbs = [None] * nb

    def head(z, gain):
        z = _rms(z, gain)
        rot = z * cos + pltpu.roll(z, HEAD_DIM // 2, 1) * sin
        return jnp.where(latent, rot, z)

    def kv_row(r):
        h = _rms(x_ref[r], g_ref[...]) * (1.0 + sc_ref[0, r]) + sh_ref[0, r]
        hbs[r] = h.astype(BF16)
        kv = jnp.dot(hbs[r], w_ref[:, nq:], preferred_element_type=F32)
        yield
        k_ref[r] = jnp.concatenate(
            [head(kv[:, i * HEAD_DIM:(i + 1) * HEAD_DIM], kn_ref[...]) for i in range(KV_HEADS)],
            axis=1).astype(BF16)
        ones = jnp.ones((TOK, HEAD_DIM), BF16)
        v_ref[r] = jnp.concatenate(
            [blk for i in range(KV_HEADS)
             for blk in (kv[:, nk + i * HEAD_DIM:nk + (i + 1) * HEAD_DIM].astype(BF16), ones)], axis=1)

    def q_row(r):
        qq = jnp.dot(hbs[r], w_ref[:, :nq], preferred_element_type=F32)
        yield
        q_ref[r] = jnp.concatenate(
            [head(qq[:, i * HEAD_DIM:(i + 1) * HEAD_DIM], qn_ref[...]) * (HEAD_DIM ** -0.5)
             for i in range(ATT_HEADS)], axis=1).astype(BF16)

    _lockstep([kv_row(r) for r in range(nb)])

    @pl.when(latent)
    def _():
        _lockstep([q_row(r) for r in range(nb)])


def _qkv(ncc, x1, gain, mods, w_qkv, qn, kn, cos, sin):
    batch, t, d = x1.shape
    nch = t // TOK
    seq = t - ncc * TOK
    nq = ATT_HEADS * HEAD_DIM
    nk = KV_HEADS * HEAD_DIM
    const = lambda shape: pl.BlockSpec(shape, lambda b, c: (0,) * len(shape))
    lat = lambda b, c: (b, jnp.maximum(c - ncc, 0), 0)
    nb = QKV_ROWS
    return pl.pallas_call(
        functools.partial(_qkv_kernel, ncc),
        grid=(batch // nb, nch),
        in_specs=[pl.BlockSpec((nb, TOK, d), lambda b, c: (b, c, 0)),
                  const((1, d)), mods.spec(1, 0, nb=nb), mods.spec(1, 1, nb=nb),
                  const((d, nq + 2 * nk)), const((1, HEAD_DIM)), const((1, HEAD_DIM)),
                  pl.BlockSpec((TOK, HEAD_DIM), lambda b, c: (jnp.maximum(c - ncc, 0), 0)),
                  pl.BlockSpec((TOK, HEAD_DIM), lambda b, c: (jnp.maximum(c - ncc, 0), 0))],
        out_specs=[pl.BlockSpec((nb, TOK, nq), lat),
                   pl.BlockSpec((nb, TOK, nk), lambda b, c: (b, c, 0)),
                   pl.BlockSpec((nb, TOK, 2 * nk), lambda b, c: (b, c, 0))],
        out_shape=[jax.ShapeDtypeStruct((batch, seq, nq), BF16),
                   jax.ShapeDtypeStruct((batch, t, nk), BF16),
                   jax.ShapeDtypeStruct((batch, t, 2 * nk), BF16)],
        compiler_params=_cparams(("arbitrary", "arbitrary")),
        name="qkv",
    )(x1, gain, mods.table, mods.table, w_qkv, qn, kn, cos, sin)


def _attn_kernel(n_kb, q_ref, k_ref, v_ref, o_ref):
    q = q_ref[0]
    qs = jnp.concatenate([q[:, g * HEAD_DIM:(g + 1) * HEAD_DIM] for g in range(Q_PER_KV)], axis=0)
    kb = k_ref.shape[1] // n_kb
    m = acc = None
    for j in range(n_kb):
        s = lax.dot_general(qs, k_ref[0, j * kb:(j + 1) * kb, :], (((1,), (1,)), ((), ())),
                            preferred_element_type=F32)
        m_blk = jnp.max(s, axis=-1, keepdims=True)
        m_new = m_blk if m is None else jnp.maximum(m, m_blk)
        p = jnp.exp((s - m_new).astype(BF16))
        pv = jnp.dot(p, v_ref[0, j * kb:(j + 1) * kb, :], preferred_element_type=F32)
        acc = pv if acc is None else jnp.exp(m - m_new) * acc + pv
        m = m_new
    o = acc[:, :HEAD_DIM] / acc[:, HEAD_DIM:HEAD_DIM + 1]
    o_ref[0] = jnp.concatenate([o[g * ATT_Q:(g + 1) * ATT_Q] for g in range(Q_PER_KV)], axis=1).astype(BF16)


def _attention(q, k, v):
    batch, seq, nq = q.shape
    t = k.shape[1]
    gw = Q_PER_KV * HEAD_DIM
    n_kb = 2 if t % (2 * LANES) == 0 else 1
    return pl.pallas_call(
        functools.partial(_attn_kernel, n_kb),
        grid=(batch, KV_HEADS, seq // ATT_Q),
        in_specs=[pl.BlockSpec((1, ATT_Q, gw), lambda b, h, i: (b, i, h)),
                  pl.BlockSpec((1, t, HEAD_DIM), lambda b, h, i: (b, 0, h)),
                  pl.BlockSpec((1, t, 2 * HEAD_DIM), lambda b, h, i: (b, 0, h))],
        out_specs=pl.BlockSpec((1, ATT_Q, gw), lambda b, h, i: (b, i, h)),
        out_shape=jax.ShapeDtypeStruct((batch, seq, nq), BF16),
        compiler_params=_cparams(("arbitrary", "arbitrary", "arbitrary")),
        name="attention",
    )(q, k, v)


def _block_diag(w):
    nb, bs, _ = w.shape
    eye = jnp.eye(nb, dtype=w.dtype)
    return (eye[:, None, :, None] * w[:, :, None, :]).reshape(nb * bs, nb * bs)


def _router_weights(wg, bg, we, be):
    d = wg.shape[0]
    n = N_GROUPS + N_EXPERTS
    wr = jnp.zeros((ROUTER_ROWS, d), F32).at[:N_GROUPS].set(wg.T).at[N_GROUPS:n].set(we.T)
    br = jnp.zeros((ROUTER_ROWS, 1), F32).at[:N_GROUPS, 0].set(bg).at[N_GROUPS:n, 0].set(be)
    w_hi = wr.astype(BF16)
    w_lo = (wr - w_hi.astype(F32)).astype(BF16)
    return jnp.concatenate([w_hi, w_lo], axis=0), br


def _rope_tables(seq):
    rows = seq // GRID_W
    row = np.repeat(np.arange(rows, dtype=np.float32), GRID_W)
    col = np.tile(np.arange(GRID_W, dtype=np.float32), rows)
    ppa = HEAD_DIM // 4
    freqs = (ROPE_THETA ** (-np.arange(ppa, dtype=np.float32) / ppa)).astype(np.float32)
    ang = np.concatenate([row[:, None] * freqs, col[:, None] * freqs], axis=-1)
    cos, sin = np.cos(ang), np.sin(ang)
    return (jnp.asarray(np.concatenate([cos, cos], axis=-1), F32),
            jnp.asarray(np.concatenate([-sin, sin], axis=-1), F32))


_HALF_SPLIT = np.concatenate([np.arange(0, HEAD_DIM, 2), np.arange(1, HEAD_DIM, 2)])


def kernel(x, c, ctx, c_ctx, norm1, norm2, w_ada, b_ada, ev_w_in, ev_conv_w, ev_conv_b, ev_lru_wa, ev_lru_ba, ev_lru_wi, ev_lru_bi, ev_lru_lam, ev_gla_wg, ev_gla_bg, ev_gla_norm, ev_w_out, od_w_qkv, od_q_norm, od_k_norm, od_w_o, moe_wg, moe_bg, moe_we, moe_be, moe_w1, moe_w3, moe_w2):
    batch, seq, d = x.shape
    ctx_len = ctx.shape[1]
    assert seq % TOK == 0 and ctx_len % TOK == 0 and d % GLA_VAL == 0 and seq % GRID_W == 0
    ncc = ctx_len // TOK
    lw = d // 2

    assert batch % TOK_ROWS == 0
    rows = -(-(batch + TOK_ROWS) // SUBLANES) * SUBLANES
    cv = jnp.zeros((rows, d), F32).at[:batch].set(c).at[batch:batch + TOK_ROWS].set(c_ctx)
    table = _adaln(cv, w_ada, b_ada)
    depth = table.shape[0]
    table = table.reshape(depth, rows, 6, d).transpose(0, 2, 1, 3).reshape(depth * 6, rows, 1, d)
    mods = _Mods(table, batch, ncc)

    w_in = ev_w_in[0]
    nm = 2 * lw + 2 * GLA_KEY + 2 * GLA_VAL
    w_main = w_in[:, :nm].astype(BF16)
    w_lr = jnp.zeros((d, LANES), F32).at[:, :2 * GLA_RANK].set(w_in[:, nm:]).astype(BF16)
    pm, lr = _inproj(x, ctx, norm1[0][None], mods, w_main, w_lr, ev_conv_w[0], ev_conv_b[0][None], lw)

    mixed = []
    for dr in range(2):
        w_gate = jnp.concatenate([_block_diag(ev_lru_wa[0, dr]), _block_diag(ev_lru_wi[0, dr])], axis=1).astype(BF16)
        b_gate = jnp.concatenate([ev_lru_ba[0, dr], ev_lru_bi[0, dr]])[None]
        wg_pad = jnp.zeros((LANES, GLA_KEY), F32).at[dr * GLA_RANK:(dr + 1) * GLA_RANK].set(ev_gla_wg[0, dr]).astype(BF16)
        mixed.append(_seqmix(dr == 1, ncc, pm, lr, w_gate, b_gate,
                             ev_lru_lam[0, dr][None], wg_pad, ev_gla_bg[0, dr][None], lw))
    (lru_f, gla_f), (lru_r, gla_r) = mixed

    wr0, br0 = _router_weights(moe_wg[0], moe_bg[0], moe_we[0], moe_be[0])
    xmid0, h2e0, cr0, cnt0 = _post0(ncc, lru_f, lru_r, gla_f, gla_r, pm, x, ctx, mods, norm2[0][None],
                                    ev_gla_norm[0][None], ev_w_out[0].astype(BF16), wr0, br0, lw)
    x1 = _moe_block(h2e0, cr0, cnt0, xmid0, mods.spec(0, 5), mods.table, moe_w1[0], moe_w3[0], moe_w2[0])

    nq = ATT_HEADS * HEAD_DIM
    nk = KV_HEADS * HEAD_DIM
    perm = np.concatenate([h * HEAD_DIM + _HALF_SPLIT for h in range(ATT_HEADS + KV_HEADS)]
                          + [np.arange(nq + nk, nq + 2 * nk)])
    w_qkv = od_w_qkv[0][:, perm].astype(BF16)
    cos, sin = _rope_tables(seq)
    q, k, v = _qkv(ncc, x1, norm1[1][None], mods, w_qkv, od_q_norm[0][_HALF_SPLIT][None],
                   od_k_norm[0][_HALF_SPLIT][None], cos, sin)
    att = _attention(q, k, v)
    wr1, br1 = _router_weights(moe_wg[1], moe_bg[1], moe_we[1], moe_be[1])
    xmid1, h2e1, cr1, cnt1 = _post1(ncc, att, x1, mods, norm2[1][None], od_w_o[0].astype(BF16), wr1, br1)
    return _moe_block(h2e1, cr1, cnt1, xmid1, mods.spec(1, 5, True), mods.table,
                      moe_w1[1], moe_w3[1], moe_w2[1])
```

```python
import functools
import itertools

import numpy as np
import jax
import jax.numpy as jnp
from jax import lax
from jax.experimental import pallas as pl
from jax.experimental.pallas import tpu as pltpu

F32 = jnp.float32
BF16 = jnp.bfloat16
I32 = jnp.int32
U32 = jnp.uint32
HIGHEST = lax.Precision.HIGHEST

EPS = 1e-6
GRID_W = 64
LRU_BLOCKS = 8
LRU_C = 8.0
GLA_HEADS = 4
GLA_DK = 64
GLA_DV = 128
GLA_KEY = GLA_HEADS * GLA_DK
GLA_VAL = GLA_HEADS * GLA_DV
GLA_RANK = 16
GLA_TAU = 16.0
GLA_CHUNK = 64
_CHUNK_SHIFT = GLA_CHUNK.bit_length() - 1
_DK_SHIFT = GLA_DK.bit_length() - 1
ATT_HEADS = 8
KV_HEADS = 2
Q_PER_KV = ATT_HEADS // KV_HEADS
HEAD_DIM = 128
ROPE_THETA = 10000.0
N_GROUPS = 4
EXPERTS_PER_GROUP = 4
N_EXPERTS = N_GROUPS * EXPERTS_PER_GROUP
PAIRS_PER_GROUP = 6
N_CLASSES = N_GROUPS * PAIRS_PER_GROUP

LANES = 128
SUBLANES = 8
TOK = 256
MOE_TILE = 512
ATT_Q = 256
PERM_ROWS = 512
TOK_ROWS = 4
QKV_ROWS = 2
SEQ_ROWS = 4
WEXT = LANES
ROUTER_ROWS = 32
VMEM_LIMIT = 56 * 1024 * 1024


def _cparams(sem):
    return pltpu.CompilerParams(dimension_semantics=sem, vmem_limit_bytes=VMEM_LIMIT)


def _rms(x, g):
    return x * lax.rsqrt(jnp.mean(x * x, axis=-1, keepdims=True) + EPS) * g


def _sigmoid(x):
    return 1.0 / (1.0 + jnp.exp(-x))


def _silu(x):
    return x * _sigmoid(x)


def _gelu_tanh(x):
    return 0.5 * x * (1.0 + jnp.tanh(np.sqrt(2.0 / np.pi).astype(np.float32) * (x + 0.044715 * (x * x * x))))


def _pack_bf16_pairs(xb):
    k = xb.shape[-1] // 2
    lo = lax.shift_right_logical(pltpu.bitcast(xb[:, :k].astype(F32), U32), jnp.uint32(16))
    hi = pltpu.bitcast(xb[:, k:].astype(F32), U32) & jnp.uint32(0xFFFF0000)
    return hi | lo


def _unpack_bf16_pairs(words):
    lo = pltpu.bitcast(lax.shift_left(words, jnp.uint32(16)), F32).astype(BF16)
    hi = pltpu.bitcast(words & jnp.uint32(0xFFFF0000), F32).astype(BF16)
    return jnp.concatenate([lo, hi], axis=1)


def _softplus(x):
    return jnp.maximum(x, 0.0) + jnp.log(1.0 + jnp.exp(-jnp.abs(x)))


def _log_sigmoid(x):
    return -_softplus(-x)


def _adaln_kernel(cv_ref, w_ref, b_ref, o_ref):
    s = _silu(cv_ref[...])
    o_ref[0] = jnp.dot(s, w_ref[0], precision=HIGHEST, preferred_element_type=F32) + b_ref[0]


def _adaln(cv, w_ada, b_ada):
    depth, d, n6 = w_ada.shape
    rows = cv.shape[0]
    tn = 6 * d // 4
    return pl.pallas_call(
        _adaln_kernel,
        grid=(depth, n6 // tn),
        in_specs=[pl.BlockSpec((rows, d), lambda l, j: (0, 0)),
                  pl.BlockSpec((1, d, tn), lambda l, j: (l, 0, j)),
                  pl.BlockSpec((1, 1, tn), lambda l, j: (l, 0, j))],
        out_specs=pl.BlockSpec((1, rows, tn), lambda l, j: (l, 0, j)),
        out_shape=jax.ShapeDtypeStruct((depth, rows, n6), F32),
        compiler_params=_cparams(("arbitrary", "arbitrary")),
        name="adaln",
    )(cv, w_ada, b_ada.reshape(depth, 1, n6))


class _Mods:
    def __init__(self, table, batch, ncc):
        self.table, self.batch, self.ncc = table, batch, ncc
        self.d = table.shape[-1]

    def spec(self, layer, j, latent_only=False, nb=1):
        batch, ncc = self.batch, self.ncc

        def imap(b, c):
            r = b if latent_only else jnp.where(c < ncc, batch // nb, b)
            return (layer * 6 + j, r, 0, 0)

        return pl.BlockSpec((1, nb, 1, self.d), imap)


def _lockstep(generators):
    for _ in itertools.zip_longest(*generators):
        pass


def _inproj_kernel(ncc, nch, lw, x_ref, c_ref, xp_ref, xn_ref, cp_ref, cn_ref, g_ref, sh_ref, sc_ref,
                   w_ref, wlr_ref, cw_ref, cb_ref, pm_ref, lr_ref):
    _lockstep([_inproj_row(ncc, nch, lw, i, x_ref, c_ref, xp_ref, xn_ref, cp_ref, cn_ref, g_ref, sh_ref, sc_ref,
                           w_ref, wlr_ref, cw_ref, cb_ref, pm_ref, lr_ref) for i in range(x_ref.shape[0])])


def _inproj_row(ncc, nch, lw, i, x_ref, c_ref, xp_ref, xn_ref, cp_ref, cn_ref, g_ref, sh_ref, sc_ref,
                w_ref, wlr_ref, cw_ref, cb_ref, pm_ref, lr_ref):
    c = pl.program_id(1)
    is_ctx = c < ncc

    def modulated(rows):
        return (_rms(rows, g_ref[...]) * (1.0 + sc_ref[0, i]) + sh_ref[0, i]).astype(BF16)

    hb = modulated(jnp.where(is_ctx, c_ref[i], x_ref[i]))
    pm = jnp.dot(hb, w_ref[...], preferred_element_type=F32)
    yield
    lr_ref[i] = jnp.dot(hb, wlr_ref[...], preferred_element_type=F32).astype(BF16)

    halo = jnp.concatenate([jnp.where(is_ctx, cp_ref[i], xp_ref[i]), jnp.where(is_ctx, cn_ref[i], xn_ref[i])], axis=0)
    xa_halo = jnp.dot(modulated(halo), w_ref[:, :lw], preferred_element_type=F32)
    yield
    has_prev = jnp.logical_and(c != 0, c != ncc)
    has_next = jnp.logical_and(c != ncc - 1, c != nch - 1)
    p2 = jnp.where(has_prev, xa_halo[SUBLANES - 2:SUBLANES - 1], 0.0)
    p1 = jnp.where(has_prev, xa_halo[SUBLANES - 1:SUBLANES], 0.0)
    n0 = jnp.where(has_next, xa_halo[SUBLANES:SUBLANES + 1], 0.0)
    xa = pm[:, :lw]
    row = lax.broadcasted_iota(I32, xa.shape, 0)
    x_m1 = jnp.where(row == 0, p1, pltpu.roll(xa, 1, 0))
    x_m2 = jnp.where(row == 0, p2, jnp.where(row == 1, p1, pltpu.roll(xa, 2, 0)))
    x_p1 = jnp.where(row == TOK - 1, n0, pltpu.roll(xa, TOK - 1, 0))
    cw = cw_ref[...]
    u = cw[0:1] * x_m2 + cw[1:2] * x_m1 + cw[2:3] * xa + cw[3:4] * x_p1 + cb_ref[...]
    pm_ref[i, :, :lw] = u.astype(BF16)
    pm_ref[i, :, lw:] = pm[:, lw:].astype(BF16)


def _inproj(x, ctx, gain, mods, w_main, w_lr, conv_w, conv_b, lw):
    batch, seq, d = x.shape
    ncc = ctx.shape[1] // TOK
    nl = seq // TOK
    nch = ncc + nl
    nm = w_main.shape[1]
    hb = TOK // SUBLANES
    nb = TOK_ROWS
    return pl.pallas_call(
        functools.partial(_inproj_kernel, ncc, nch, lw),
        grid=(batch // nb, nch),
        in_specs=[pl.BlockSpec((nb, TOK, d), lambda b, c: (b, jnp.maximum(c - ncc, 0), 0)),
                  pl.BlockSpec((nb, TOK, d), lambda b, c: (b, jnp.minimum(c, ncc - 1), 0)),
                  pl.BlockSpec((nb, SUBLANES, d), lambda b, c: (b, jnp.maximum((c - ncc) * hb - 1, 0), 0)),
                  pl.BlockSpec((nb, SUBLANES, d),
                               lambda b, c: (b, jnp.clip((c - ncc + 1) * hb, 0, nl * hb - 1), 0)),
                  pl.BlockSpec((nb, SUBLANES, d), lambda b, c: (b, jnp.clip(c * hb - 1, 0, ncc * hb - 1), 0)),
                  pl.BlockSpec((nb, SUBLANES, d), lambda b, c: (b, jnp.clip((c + 1) * hb, 0, ncc * hb - 1), 0)),
                  pl.BlockSpec((1, d), lambda b, c: (0, 0)),
                  mods.spec(0, 0, nb=nb), mods.spec(0, 1, nb=nb),
                  pl.BlockSpec((d, nm), lambda b, c: (0, 0)),
                  pl.BlockSpec((d, LANES), lambda b, c: (0, 0)),
                  pl.BlockSpec((4, lw), lambda b, c: (0, 0)),
                  pl.BlockSpec((1, lw), lambda b, c: (0, 0))],
        out_specs=[pl.BlockSpec((nb, TOK, nm), lambda b, c: (b, c, 0)),
                   pl.BlockSpec((nb, TOK, LANES), lambda b, c: (b, c, 0))],
        out_shape=[jax.ShapeDtypeStruct((batch, nch * TOK, nm), BF16),
                   jax.ShapeDtypeStruct((batch, nch * TOK, LANES), BF16)],
        compiler_params=_cparams(("arbitrary", "arbitrary")),
        name="inproj",
    )(x, ctx, x, x, ctx, ctx, gain, mods.table, mods.table, w_main, w_lr, conv_w, conv_b)


def _lru_scan(a, b, h0, rev):
    n_groups = TOK // SUBLANES
    a = a.reshape(n_groups, SUBLANES, a.shape[-1])
    b = b.reshape(a.shape)
    sub = lax.broadcasted_iota(I32, a.shape, 1)
    for dist in (1, 2, 4):
        shift = SUBLANES - dist if rev else dist
        a_s = pltpu.roll(a, shift, 1)
        b_s = pltpu.roll(b, shift, 1)
        m = (sub < SUBLANES - dist) if rev else (sub >= dist)
        b = jnp.where(m, a * b_s + b, b)
        a = jnp.where(m, a * a_s, a)
    order = range(n_groups - 1, -1, -1) if rev else range(n_groups)
    outs = [None] * n_groups
    h = h0
    for r in order:
        hr = a[r] * h + b[r]
        outs[r] = hr
        h = hr[0:1] if rev else hr[SUBLANES - 1:SUBLANES]
    return jnp.concatenate(outs, axis=0), h


def _seqmix_kernel(rev, *refs):
    h_scr, s_scr = refs[-2:]

    @pl.when(pl.program_id(1) == 0)
    def _():
        h_scr[...] = jnp.zeros_like(h_scr)
        s_scr[...] = jnp.zeros_like(s_scr)

    _lockstep([_seqmix_row(rev, i, *refs) for i in range(refs[0].shape[0])])


def _seqmix_row(rev, i, u_ref, q_ref, k_ref, v_ref, lr_ref,
                wgt_ref, bgt_ref, lam_ref, wg_ref, bg_ref, sel_ref,
                lru_ref, gla_ref, h_scr, s_scr):
    lw = u_ref.shape[-1]
    ub = u_ref[i]
    u = ub.astype(F32)
    gates = jnp.dot(ub, wgt_ref[...], preferred_element_type=F32) + bgt_ref[...]
    yield
    r_gate = _sigmoid(gates[:, :lw])
    i_gate = _sigmoid(gates[:, lw:])
    log_a = (-LRU_C) * r_gate * _softplus(-lam_ref[...])
    a = jnp.exp(log_a)
    bb = jnp.sqrt(1.0 - a * a) * (i_gate * u)
    hs, h_last = _lru_scan(a, bb, h_scr[i, 0:1], rev)
    h_scr[i, 0:1] = h_last
    lru_ref[i] = hs.astype(BF16)
    yield

    lg = _log_sigmoid(jnp.dot(lr_ref[i], wg_ref[...], preferred_element_type=F32) + bg_ref[...]) * (1.0 / GLA_TAU)
    lg_hi = lg.astype(BF16)
    lg_lo = (lg - lg_hi.astype(F32)).astype(BF16)
    yield
    cum = jnp.dot(sel_ref[...], jnp.concatenate([lg_hi, lg_lo], axis=1), preferred_element_type=F32)
    yield
    g = cum[:TOK, :GLA_KEY] + cum[:TOK, GLA_KEY:]
    tot = cum[TOK:, :GLA_KEY] + cum[TOK:, GLA_KEY:]
    q = q_ref[i].astype(F32)
    k = k_ref[i].astype(F32)
    dec_all = jnp.exp(tot)
    k_neg = k * jnp.exp(-g)
    qg = (q * jnp.exp(g) * (GLA_DK ** -0.5)).astype(BF16)
    kg = k_neg.astype(BF16)
    kd = (k_neg * dec_all).astype(BF16)
    v = v_ref[i]

    lane = lax.broadcasted_iota(I32, (1, GLA_KEY), 1) >> _DK_SHIFT
    head_masks = [(lane == h) for h in range(GLA_HEADS)]
    n_stack = GLA_HEADS * GLA_CHUNK
    ai = lax.broadcasted_iota(I32, (n_stack, GLA_CHUNK), 0) & (GLA_CHUNK - 1)
    aj = lax.broadcasted_iota(I32, (n_stack, GLA_CHUNK), 1)
    causal = (aj >= ai) if rev else (aj <= ai)

    n_sub = TOK // GLA_CHUNK
    order = range(n_sub - 1, -1, -1) if rev else range(n_sub)
    outs = [None] * n_sub
    st = s_scr[i]
    for n in order:
        sl = slice(n * GLA_CHUNK, (n + 1) * GLA_CHUNK)
        qn = qg[sl]
        qs = jnp.concatenate([jnp.where(head_masks[h], qn, jnp.zeros_like(qn)) for h in range(GLA_HEADS)], axis=0)
        att = lax.dot_general(qs, kg[sl], (((1,), (1,)), ((), ())), preferred_element_type=F32)
        yield
        att = jnp.where(causal, att, 0.0).astype(BF16)
        o_all = jnp.dot(att, v[sl], preferred_element_type=F32)
        yield
        o_int = lax.dot_general(qs, st.astype(BF16), (((1,), (1,)), ((), ())),
                                preferred_element_type=F32)
        yield
        outs[n] = jnp.concatenate(
            [o_all[h * GLA_CHUNK:(h + 1) * GLA_CHUNK, h * GLA_DV:(h + 1) * GLA_DV]
             + o_int[h * GLA_CHUNK:(h + 1) * GLA_CHUNK] for h in range(GLA_HEADS)], axis=1)
        vn, kdn = v[sl], kd[sl]
        v_stack = jnp.concatenate([vn[:, h * GLA_DV:(h + 1) * GLA_DV] for h in range(GLA_HEADS)], axis=0)
        kd_stack = jnp.concatenate([jnp.where(head_masks[h], kdn, jnp.zeros_like(kdn)) for h in range(GLA_HEADS)],
                                   axis=0)
        ds = lax.dot_general(v_stack, kd_stack, (((0,), (0,)), ((), ())), preferred_element_type=F32)
        yield
        st = dec_all[n * GLA_CHUNK:n * GLA_CHUNK + 1] * st + ds
    s_scr[i] = st
    gla_ref[i] = jnp.concatenate(outs, axis=0).astype(BF16)


def _cumsum_selector(rev):
    r = np.arange(TOK)[:, None]
    c = np.arange(TOK)[None, :]
    same = (r // GLA_CHUNK) == (c // GLA_CHUNK)
    tri = same & ((c >= r) if rev else (c <= r))
    return jnp.asarray(np.concatenate([tri, same], axis=0), BF16)


def _seqmix(rev, ncc, pm, lr, w_gate, b_gate, lam, wg_pad, bg, lw):
    batch, t, _ = pm.shape
    nch = t // TOK
    nl = nch - ncc

    def chunk(s):
        if rev:
            return jnp.where(s < ncc, ncc - 1 - s, ncc + (nl - 1) - (s - ncc))
        return s

    qcol = 2 * lw // GLA_KEY
    vcol = (2 * lw + 2 * GLA_KEY) // GLA_VAL
    const = lambda shape: pl.BlockSpec(shape, lambda b, s: (0,) * len(shape))
    nb = SEQ_ROWS if batch % SEQ_ROWS == 0 else 1
    return pl.pallas_call(
        functools.partial(_seqmix_kernel, rev),
        grid=(batch // nb, nch),
        in_specs=[pl.BlockSpec((nb, TOK, lw), lambda b, s: (b, chunk(s), 0)),
                  pl.BlockSpec((nb, TOK, GLA_KEY), lambda b, s: (b, chunk(s), qcol)),
                  pl.BlockSpec((nb, TOK, GLA_KEY), lambda b, s: (b, chunk(s), qcol + 1)),
                  pl.BlockSpec((nb, TOK, GLA_VAL), lambda b, s: (b, chunk(s), vcol)),
                  pl.BlockSpec((nb, TOK, LANES), lambda b, s: (b, chunk(s), 0)),
                  const((lw, 2 * lw)), const((1, 2 * lw)), const((1, lw)),
                  const((LANES, GLA_KEY)), const((1, GLA_KEY)), const((2 * TOK, TOK))],
        out_specs=[pl.BlockSpec((nb, TOK, lw), lambda b, s: (b, chunk(s), 0)),
                   pl.BlockSpec((nb, TOK, GLA_VAL), lambda b, s: (b, chunk(s), 0))],
        out_shape=[jax.ShapeDtypeStruct((batch, t, lw), BF16),
                   jax.ShapeDtypeStruct((batch, t, GLA_VAL), BF16)],
        scratch_shapes=[pltpu.VMEM((nb, SUBLANES, lw), F32), pltpu.VMEM((nb, GLA_DV, GLA_KEY), F32)],
        compiler_params=_cparams(("arbitrary", "arbitrary")),
        name="seqmix_rev" if rev else "seqmix_fwd",
    )(pm, pm, pm, pm, lr, w_gate, b_gate, lam, wg_pad, bg, _cumsum_selector(rev))


def _route_epilogue(i, first, xm, n2_ref, sh2_ref, sc2_ref, wr_ref, br_ref,
                    xmid_ref, h2_ref, cr_ref, cnt_ref, carry):
    d = xm.shape[-1]
    xmid_ref[i] = xm
    h2 = _rms(xm, n2_ref[...]) * (1.0 + sc2_ref[0, i]) + sh2_ref[0, i]
    h_hi = h2.astype(BF16)
    h_lo = (h2 - h_hi.astype(F32)).astype(BF16)
    nt = (((1,), (1,)), ((), ()))
    o1 = lax.dot_general(wr_ref[...], h_hi, nt, preferred_element_type=F32)
    o2 = lax.dot_general(wr_ref[:ROUTER_ROWS], h_lo, nt, preferred_element_type=F32)
    yield
    logits = o1[:ROUTER_ROWS] + o1[ROUTER_ROWS:] + o2 + br_ref[...]

    def col(j):
        return logits[j:j + 1]

    lgs = [col(g) for g in range(N_GROUPS)]
    gmax = functools.reduce(jnp.maximum, lgs)
    gi = jnp.where(lgs[0] == gmax, 0, jnp.where(lgs[1] == gmax, 1, jnp.where(lgs[2] == gmax, 2, 3)))
    w_group = 1.0 / functools.reduce(lambda p, q: p + q, [jnp.exp(l - gmax) for l in lgs])
    es = []
    for j in range(EXPERTS_PER_GROUP):
        acc = jnp.zeros_like(gmax)
        for g in range(N_GROUPS):
            acc = acc + jnp.where(gi == g, col(N_GROUPS + g * EXPERTS_PER_GROUP + j), 0.0)
        es.append(acc)
    m1 = functools.reduce(jnp.maximum, es)
    i1 = jnp.where(es[0] == m1, 0, jnp.where(es[1] == m1, 1, jnp.where(es[2] == m1, 2, 3)))
    rest = [jnp.where(i1 == j, -jnp.inf, es[j]) for j in range(EXPERTS_PER_GROUP)]
    m2 = functools.reduce(jnp.maximum, rest)
    i2 = jnp.where(rest[0] == m2, 0, jnp.where(rest[1] == m2, 1, jnp.where(rest[2] == m2, 2, 3)))
    e2 = jnp.exp(m2 - m1)
    w1 = w_group / (1.0 + e2)
    w2 = w_group * e2 / (1.0 + e2)
    first_lo = i1 < i2
    lo = jnp.where(first_lo, i1, i2)
    hi = jnp.where(first_lo, i2, i1)
    w_lo = jnp.where(first_lo, w1, w2)
    w_hi = jnp.where(first_lo, w2, w1)
    pidx = jnp.where(lo == 0, hi - 1, jnp.where(lo == 1, hi + 1, 5))
    cls = gi * PAIRS_PER_GROUP + pidx

    def terms(w):
        t1 = w.astype(BF16).astype(F32)
        t2 = (w - t1).astype(BF16).astype(F32)
        t3 = (w - t1 - t2).astype(BF16).astype(F32)
        return [t1, t2, t3]

    sub = lax.broadcasted_iota(I32, (2 * SUBLANES, TOK), 0)
    stacked = jnp.zeros((2 * SUBLANES, TOK), F32)
    for r, term in enumerate(terms(w_lo) + terms(w_hi)):
        stacked = jnp.where(sub == r, term, stacked)
    er = lax.broadcasted_iota(I32, (2 * SUBLANES, LANES), 0)
    el = lax.broadcasted_iota(I32, (2 * SUBLANES, LANES), 1)
    place = jnp.logical_or(jnp.logical_and(er < 3, el == 0),
                           jnp.logical_and(jnp.logical_and(er >= 3, er < 6), el == 1))
    w_cols = lax.dot_general(stacked.astype(BF16), place.astype(BF16), (((0,), (0,)), ((), ())),
                             preferred_element_type=F32)
    h2_ref[i, :, :d // 2] = _pack_bf16_pairs(h_hi)
    h2_ref[i, :, d // 2:] = pltpu.bitcast(w_cols, U32)

    krow = lax.broadcasted_iota(I32, (ROUTER_ROWS, TOK), 0)
    onehot = krow == cls
    ri = lax.broadcasted_iota(I32, (TOK, TOK), 0)
    ci = lax.broadcasted_iota(I32, (TOK, TOK), 1)
    triu = (ri <= ci).astype(BF16)
    pref = jnp.dot(onehot.astype(BF16), triu, preferred_element_type=F32)
    yield

    if i == 0:
        @pl.when(first)
        def _():
            carry[...] = jnp.zeros_like(carry)

    base = carry[:, 0:1]
    rank = jnp.sum(jnp.where(onehot, pref - 1.0 + base, 0.0), axis=0, keepdims=True)
    new_carry = jnp.broadcast_to(base + jnp.sum(onehot.astype(F32), axis=1, keepdims=True), carry.shape)
    carry[...] = new_carry
    cnt_ref[...] = new_carry
    sub8 = lax.broadcasted_iota(I32, (SUBLANES, TOK), 0)
    cr_ref[i, 0] = jnp.where(sub8 == 0, cls, jnp.where(sub8 == 1, rank.astype(I32), 0))


def _post0_kernel(ncc, *refs):
    _lockstep([_post0_row(ncc, i, *refs) for i in range(refs[0].shape[0])])


def _post0_row(ncc, i, l0_ref, l1_ref, g0_ref, g1_ref, ga_ref, r_ref, x_ref, c_ref,
               gate_ref, sh2_ref, sc2_ref, n2_ref, gn_ref, wout_ref, wr_ref, br_ref,
               xmid_ref, h2_ref, cr_ref, cnt_ref, carry):
    b, c = pl.program_id(0), pl.program_id(1)
    lru = l0_ref[i].astype(F32) + l1_ref[i].astype(F32)
    ya = lru * _gelu_tanh(ga_ref[i].astype(F32))
    gla = g0_ref[i].astype(F32) + g1_ref[i].astype(F32)
    parts = []
    for h in range(GLA_HEADS):
        parts.append(_rms(gla[:, h * GLA_DV:(h + 1) * GLA_DV], gn_ref[...]))
    yb = jnp.concatenate(parts, axis=1) * _silu(r_ref[i].astype(F32))
    ycat = jnp.concatenate([ya, yb], axis=1).astype(BF16)
    y = jnp.dot(ycat, wout_ref[...], preferred_element_type=F32)
    yield
    x0 = jnp.where(c < ncc, c_ref[i], x_ref[i])
    xm = x0 + gate_ref[0, i] * y
    yield from _route_epilogue(i, jnp.logical_and(b == 0, c == 0), xm, n2_ref, sh2_ref, sc2_ref, wr_ref, br_ref,
                               xmid_ref, h2_ref, cr_ref, cnt_ref, carry)


def _post1_kernel(*refs):
    _lockstep([_post1_row(i, *refs) for i in range(refs[0].shape[0])])


def _post1_row(i, o_ref, x_ref, gate_ref, sh2_ref, sc2_ref, n2_ref, wo_ref, wr_ref, br_ref,
               xmid_ref, h2_ref, cr_ref, cnt_ref, carry):
    b, c = pl.program_id(0), pl.program_id(1)
    y = jnp.dot(o_ref[i], wo_ref[...], preferred_element_type=F32)
    yield
    xm = x_ref[i] + gate_ref[0, i] * y
    yield from _route_epilogue(i, jnp.logical_and(b == 0, c == 0), xm, n2_ref, sh2_ref, sc2_ref, wr_ref, br_ref,
                               xmid_ref, h2_ref, cr_ref, cnt_ref, carry)


def _route_out(batch, nch, d, nb):
    specs = [pl.BlockSpec((nb, TOK, d), lambda b, c: (b, c, 0)),
             pl.BlockSpec((nb, TOK, d // 2 + WEXT), lambda b, c: (b, c, 0)),
             pl.BlockSpec((nb, 1, SUBLANES, TOK), lambda b, c: (b, c, 0, 0)),
             pl.BlockSpec((ROUTER_ROWS, LANES), lambda b, c: (0, 0))]
    shapes = [jax.ShapeDtypeStruct((batch, nch * TOK, d), F32),
              jax.ShapeDtypeStruct((batch, nch * TOK, d // 2 + WEXT), U32),
              jax.ShapeDtypeStruct((batch, nch, SUBLANES, TOK), I32),
              jax.ShapeDtypeStruct((ROUTER_ROWS, LANES), F32)]
    return specs, shapes


def _post0(ncc, lru_f, lru_r, gla_f, gla_r, pm, x, ctx, mods, n2, gn, w_out, wr, br, lw):
    batch, t, _ = pm.shape
    d = x.shape[-1]
    nch = t // TOK
    rcol = (2 * lw + 2 * GLA_KEY + GLA_VAL) // GLA_VAL
    nb = TOK_ROWS
    tokspec = lambda w: pl.BlockSpec((nb, TOK, w), lambda b, c: (b, c, 0))
    const = lambda shape: pl.BlockSpec(shape, lambda b, c: (0,) * len(shape))
    out_specs, out_shape = _route_out(batch, nch, d, nb)
    return pl.pallas_call(
        functools.partial(_post0_kernel, ncc),
        grid=(batch // nb, nch),
        in_specs=[tokspec(lw), tokspec(lw), tokspec(GLA_VAL), tokspec(GLA_VAL),
                  pl.BlockSpec((nb, TOK, lw), lambda b, c: (b, c, 1)),
                  pl.BlockSpec((nb, TOK, GLA_VAL), lambda b, c: (b, c, rcol)),
                  pl.BlockSpec((nb, TOK, d), lambda b, c: (b, jnp.maximum(c - ncc, 0), 0)),
                  pl.BlockSpec((nb, TOK, d), lambda b, c: (b, jnp.minimum(c, ncc - 1), 0)),
                  mods.spec(0, 2, nb=nb), mods.spec(0, 3, nb=nb), mods.spec(0, 4, nb=nb),
                  const((1, d)), const((1, GLA_DV)), const((lw + GLA_VAL, d)),
                  const((2 * ROUTER_ROWS, d)), const((ROUTER_ROWS, 1))],
        out_specs=out_specs, out_shape=out_shape,
        scratch_shapes=[pltpu.VMEM((ROUTER_ROWS, LANES), F32)],
        compiler_params=_cparams(("arbitrary", "arbitrary")),
        name="post0",
    )(lru_f, lru_r, gla_f, gla_r, pm, pm, x, ctx, mods.table, mods.table, mods.table, n2, gn, w_out, wr, br)


def _post1(ncc, att, x1, mods, n2, w_o, wr, br):
    batch, seq, aw = att.shape
    d = x1.shape[-1]
    nl = seq // TOK
    const = lambda shape: pl.BlockSpec(shape, lambda b, c: (0,) * len(shape))
    nb = TOK_ROWS
    out_specs, out_shape = _route_out(batch, nl, d, nb)
    return pl.pallas_call(
        _post1_kernel,
        grid=(batch // nb, nl),
        in_specs=[pl.BlockSpec((nb, TOK, aw), lambda b, c: (b, c, 0)),
                  pl.BlockSpec((nb, TOK, d), lambda b, c: (b, c + ncc, 0)),
                  mods.spec(1, 2, True, nb), mods.spec(1, 3, True, nb), mods.spec(1, 4, True, nb),
                  const((1, d)), const((aw, d)), const((2 * ROUTER_ROWS, d)), const((ROUTER_ROWS, 1))],
        out_specs=out_specs, out_shape=out_shape,
        scratch_shapes=[pltpu.VMEM((ROUTER_ROWS, LANES), F32)],
        compiler_params=_cparams(("arbitrary", "arbitrary")),
        name="post1",
    )(att, x1, mods.table, mods.table, mods.table, n2, w_o, wr, br)


_PAIR_LO = np.array([0, 0, 0, 1, 1, 2], np.int32)
_PAIR_HI = np.array([1, 2, 3, 2, 3, 3], np.int32)


def _plan(cr, cnt, n_tiles):
    cls, rank = cr[:, :, 0, :], cr[:, :, 1, :]
    counts = cnt[:N_CLASSES, 0].astype(I32)
    tiles = (counts + MOE_TILE - 1) // MOE_TILE
    ends = jnp.cumsum(tiles)
    offs = (ends - tiles) * MOE_TILE
    pos = rank
    for c in range(N_CLASSES):
        pos = pos + jnp.where(cls == c, offs[c], 0)
    pos = pos.reshape(-1)
    tile_ids = jnp.arange(n_tiles, dtype=I32)
    tile_cls = jnp.minimum(jnp.sum((tile_ids[:, None] >= ends[None, :]).astype(I32), axis=1), N_CLASSES - 1)
    grp = tile_cls // PAIRS_PER_GROUP
    pair = tile_cls % PAIRS_PER_GROUP
    lo = grp * EXPERTS_PER_GROUP + jnp.asarray(_PAIR_LO)[pair]
    hi = grp * EXPERTS_PER_GROUP + jnp.asarray(_PAIR_HI)[pair]
    return pos.astype(I32), lo.astype(I32), hi.astype(I32), ends[-1:].astype(I32)


def _moe_kernel(d, lo_ref, hi_ref, used_ref, x_ref, w1l, w3l, w2l, w1h, w3h, w2h, y_ref, wbuf, wbuf2):
    i = pl.program_id(0)
    active = i < used_ref[0]

    @pl.when(active)
    def _():
        changed = jnp.logical_or(i == 0, jnp.logical_or(lo_ref[i] != lo_ref[jnp.maximum(i - 1, 0)],
                                                        hi_ref[i] != hi_ref[jnp.maximum(i - 1, 0)]))

        @pl.when(changed)
        def _():
            for k, w in enumerate((w1l, w3l, w1h, w3h)):
                wbuf[k] = w[0, 0].astype(BF16)
            wbuf2[0] = w2l[0, 0].astype(BF16)
            wbuf2[1] = w2h[0, 0].astype(BF16)

        x = _unpack_bf16_pairs(x_ref[:, :d // 2])
        wts = pltpu.bitcast(x_ref[:, d // 2:], F32)

        def expert(k1, k3, k2):
            a = jnp.dot(x, wbuf[k1], preferred_element_type=F32)
            b = jnp.dot(x, wbuf[k3], preferred_element_type=F32)
            return jnp.dot((_silu(a) * b).astype(BF16), wbuf2[k2], preferred_element_type=F32)

        y = wts[:, 0:1] * expert(0, 1, 0) + wts[:, 1:2] * expert(2, 3, 1)
        y_ref[...] = _pack_bf16_pairs(y.astype(BF16))

    @pl.when(jnp.logical_not(active))
    def _():
        y_ref[...] = jnp.zeros_like(y_ref)


def _moe(xs, lo, hi, n_used, layer, w1, w3, w2):
    n_pad, wd = xs.shape
    d = 2 * (wd - WEXT)
    hdim = w1.shape[-1]
    n_tiles = n_pad // MOE_TILE
    wl = lambda i, lo, hi, used: (layer, lo[i], 0, 0)
    wh = lambda i, lo, hi, used: (layer, hi[i], 0, 0)
    up = lambda imap: pl.BlockSpec((1, 1, d, hdim), imap)
    down = lambda imap: pl.BlockSpec((1, 1, hdim, d), imap)
    grid_spec = pltpu.PrefetchScalarGridSpec(
        num_scalar_prefetch=3,
        grid=(n_tiles,),
        in_specs=[pl.BlockSpec((MOE_TILE, wd), lambda i, lo, hi, used: (i, 0)),
                  up(wl), up(wl), down(wl), up(wh), up(wh), down(wh)],
        out_specs=pl.BlockSpec((MOE_TILE, d // 2), lambda i, lo, hi, used: (i, 0)),
        scratch_shapes=[pltpu.VMEM((4, d, hdim), BF16), pltpu.VMEM((2, hdim, d), BF16)],
    )
    return pl.pallas_call(
        functools.partial(_moe_kernel, d),
        grid_spec=grid_spec,
        out_shape=jax.ShapeDtypeStruct((n_pad, d // 2), U32),
        compiler_params=_cparams(("arbitrary",)),
        name="moe_experts",
    )(lo, hi, n_used, xs, w1, w3, w2, w1, w3, w2)


def _row_dma_loops(tile_rows, make_copy, index):
    def issue(grp, carry):
        for sub in range(SUBLANES):
            make_copy(grp, sub, index(grp * SUBLANES + sub)).start()
        return carry

    def drain(grp, carry):
        for _ in range(SUBLANES):
            make_copy(0, 0, 0).wait()
        return carry

    n_groups = tile_rows // SUBLANES

    def issue_all():
        for grp in range(n_groups):
            issue(grp, 0)

    return issue_all, (lambda: lax.fori_loop(0, n_groups, drain, 0))


def _scatter_rows_kernel(rows, pos_ref, src_ref, init_ref, dst_ref, sem):
    del init_ref
    issue, drain = _row_dma_loops(
        rows,
        lambda grp, sub, p: pltpu.make_async_copy(src_ref.at[grp, pl.ds(sub, 1)], dst_ref.at[pl.ds(p, 1)], sem),
        lambda j: pos_ref[0, 0, j])
    issue()
    drain()


def _scatter_rows(src, pos, n_pad):
    n, w = src.shape
    rows = PERM_ROWS if n % PERM_ROWS == 0 else TOK
    steps = n // rows
    return pl.pallas_call(
        functools.partial(_scatter_rows_kernel, rows),
        grid=(steps,),
        in_specs=[pl.BlockSpec((1, 1, rows), lambda i: (i, 0, 0), memory_space=pltpu.SMEM),
                  pl.BlockSpec((rows // SUBLANES, SUBLANES, w), lambda i: (i, 0, 0)),
                  pl.BlockSpec(memory_space=pl.ANY)],
        out_specs=pl.BlockSpec(memory_space=pl.ANY),
        out_shape=jax.ShapeDtypeStruct((n_pad, w), src.dtype),
        scratch_shapes=[pltpu.SemaphoreType.DMA(())],
        input_output_aliases={2: 0},
        compiler_params=pltpu.CompilerParams(dimension_semantics=("arbitrary",), has_side_effects=True),
        name="scatter_rows",
    )(pos.reshape(steps, 1, rows), src.reshape(n // SUBLANES, SUBLANES, w), jnp.zeros((n_pad, w), src.dtype))


def _gather_residual_kernel(n_steps, pos_ref, posn_ref, ys_ref, xm_ref, gate_ref, o_ref, buf, sem):
    step = pl.program_id(0) * pl.num_programs(1) + pl.program_id(1)
    slot = step % 2

    def loops(idx_ref, s):
        return _row_dma_loops(
            TOK,
            lambda grp, sub, p: pltpu.make_async_copy(ys_ref.at[pl.ds(p, 1)], buf.at[s, grp, pl.ds(sub, 1)],
                                                      sem.at[s]),
            lambda j: idx_ref[0, 0, j])

    @pl.when(step == 0)
    def _():
        loops(pos_ref, slot)[0]()

    @pl.when(step + 1 < n_steps)
    def _():
        loops(posn_ref, 1 - slot)[0]()

    loops(pos_ref, slot)[1]()
    y = _unpack_bf16_pairs(buf[slot].reshape(TOK, buf.shape[-1])).astype(F32)
    o_ref[0] = xm_ref[0] + gate_ref[0, 0] * y


def _gather_residual(ys, pos, xmid, gate_spec, mod_table):
    batch, t, d = xmid.shape
    nch = t // TOK
    n_steps = batch * nch
    return pl.pallas_call(
        functools.partial(_gather_residual_kernel, n_steps),
        grid=(batch, nch),
        in_specs=[pl.BlockSpec((1, 1, TOK), lambda b, c: (b * nch + c, 0, 0), memory_space=pltpu.SMEM),
                  pl.BlockSpec((1, 1, TOK), lambda b, c: (jnp.minimum(b * nch + c + 1, n_steps - 1), 0, 0),
                               memory_space=pltpu.SMEM),
                  pl.BlockSpec(memory_space=pl.ANY),
                  pl.BlockSpec((1, TOK, d), lambda b, c: (b, c, 0)),
                  gate_spec],
        out_specs=pl.BlockSpec((1, TOK, d), lambda b, c: (b, c, 0)),
        out_shape=jax.ShapeDtypeStruct((batch, t, d), F32),
        scratch_shapes=[pltpu.VMEM((2, TOK // SUBLANES, SUBLANES, d // 2), U32), pltpu.SemaphoreType.DMA((2,))],
        compiler_params=_cparams(("arbitrary", "arbitrary")),
        name="gather_residual",
    )(pos.reshape(n_steps, 1, TOK), pos.reshape(n_steps, 1, TOK), ys, xmid, mod_table)


def _moe_block(h2ext, cr, cnt, xmid, gate_spec, mod_table, layer, w1, w3, w2):
    h2ext = h2ext.reshape(-1, h2ext.shape[-1])
    n = h2ext.shape[0]
    n_tiles = n // MOE_TILE + N_CLASSES
    pos, lo, hi, n_used = _plan(cr, cnt, n_tiles)
    xs = _scatter_rows(h2ext, pos, n_tiles * MOE_TILE)
    ys = _moe(xs, lo, hi, n_used, layer, w1, w3, w2)
    return _gather_residual(ys, pos, xmid, gate_spec, mod_table)


def _qkv_kernel(ncc, x_ref, g_ref, sh_ref, sc_ref, w_ref, qn_ref, kn_ref, cos_ref, sin_ref,
                q_ref, k_ref, v_ref):
    c = pl.program_id(1)
    latent = c >= ncc
    nb = x_ref.shape[0]
    nq = ATT_HEADS * HEAD_DIM
    nk = KV_HEADS * HEAD_DIM
    cos = cos_ref[...]
    sin = sin_ref[...]
    hbs = [None] * nb

    def head(z, gain):
        z = _rms(z, gain)
        rot = z * cos + pltpu.roll(z, HEAD_DIM // 2, 1) * sin
        return jnp.where(latent, rot, z)

    def kv_row(r):
        h = _rms(x_ref[r], g_ref[...]) * (1.0 + sc_ref[0, r]) + sh_ref[0, r]
        hbs[r] = h.astype(BF16)
        kv = jnp.dot(hbs[r], w_ref[:, nq:], preferred_element_type=F32)
        yield
        k_ref[r] = jnp.concatenate(
            [head(kv[:, i * HEAD_DIM:(i + 1) * HEAD_DIM], kn_ref[...]) for i in range(KV_HEADS)],
            axis=1).astype(BF16)
        ones = jnp.ones((TOK, HEAD_DIM), BF16)
        v_ref[r] = jnp.concatenate(
            [blk for i in range(KV_HEADS)
             for blk in (kv[:, nk + i * HEAD_DIM:nk + (i + 1) * HEAD_DIM].astype(BF16), ones)], axis=1)

    def q_row(r):
        qq = jnp.dot(hbs[r], w_ref[:, :nq], preferred_element_type=F32)
        yield
        q_ref[r] = jnp.concatenate(
            [head(qq[:, i * HEAD_DIM:(i + 1) * HEAD_DIM], qn_ref[...]) * (HEAD_DIM ** -0.5)
             for i in range(ATT_HEADS)], axis=1).astype(BF16)

    _lockstep([kv_row(r) for r in range(nb)])

    @pl.when(latent)
    def _():
        _lockstep([q_row(r) for r in range(nb)])


def _qkv(ncc, x1, gain, mods, w_qkv, qn, kn, cos, sin):
    batch, t, d = x1.shape
    nch = t // TOK
    seq = t - ncc * TOK
    nq = ATT_HEADS * HEAD_DIM
    nk = KV_HEADS * HEAD_DIM
    const = lambda shape: pl.BlockSpec(shape, lambda b, c: (0,) * len(shape))
    lat = lambda b, c: (b, jnp.maximum(c - ncc, 0), 0)
    nb = QKV_ROWS
    return pl.pallas_call(
        functools.partial(_qkv_kernel, ncc),
        grid=(batch // nb, nch),
        in_specs=[pl.BlockSpec((nb, TOK, d), lambda b, c: (b, c, 0)),
                  const((1, d)), mods.spec(1, 0, nb=nb), mods.spec(1, 1, nb=nb),
                  const((d, nq + 2 * nk)), const((1, HEAD_DIM)), const((1, HEAD_DIM)),
                  pl.BlockSpec((TOK, HEAD_DIM), lambda b, c: (jnp.maximum(c - ncc, 0), 0)),
                  pl.BlockSpec((TOK, HEAD_DIM), lambda b, c: (jnp.maximum(c - ncc, 0), 0))],
        out_specs=[pl.BlockSpec((nb, TOK, nq), lat),
                   pl.BlockSpec((nb, TOK, nk), lambda b, c: (b, c, 0)),
                   pl.BlockSpec((nb, TOK, 2 * nk), lambda b, c: (b, c, 0))],
        out_shape=[jax.ShapeDtypeStruct((batch, seq, nq), BF16),
                   jax.ShapeDtypeStruct((batch, t, nk), BF16),
                   jax.ShapeDtypeStruct((batch, t, 2 * nk), BF16)],
        compiler_params=_cparams(("arbitrary", "arbitrary")),
        name="qkv",
    )(x1, gain, mods.table, mods.table, w_qkv, qn, kn, cos, sin)


def _attn_kernel(n_kb, q_ref, k_ref, v_ref, o_ref):
    q = q_ref[0]
    qs = jnp.concatenate([q[:, g * HEAD_DIM:(g + 1) * HEAD_DIM] for g in range(Q_PER_KV)], axis=0)
    kb = k_ref.shape[1] // n_kb
    m = acc = None
    for j in range(n_kb):
        s = lax.dot_general(qs, k_ref[0, j * kb:(j + 1) * kb, :], (((1,), (1,)), ((), ())),
                            preferred_element_type=F32)
        m_blk = jnp.max(s, axis=-1, keepdims=True)
        m_new = m_blk if m is None else jnp.maximum(m, m_blk)
        p = jnp.exp((s - m_new).astype(BF16))
        pv = jnp.dot(p, v_ref[0, j * kb:(j + 1) * kb, :], preferred_element_type=F32)
        acc = pv if acc is None else jnp.exp(m - m_new) * acc + pv
        m = m_new
    o = acc[:, :HEAD_DIM] / acc[:, HEAD_DIM:HEAD_DIM + 1]
    o_ref[0] = jnp.concatenate([o[g * ATT_Q:(g + 1) * ATT_Q] for g in range(Q_PER_KV)], axis=1).astype(BF16)


def _attention(q, k, v):
    batch, seq, nq = q.shape
    t = k.shape[1]
    gw = Q_PER_KV * HEAD_DIM
    n_kb = 2 if t % (2 * LANES) == 0 else 1
    return pl.pallas_call(
        functools.partial(_attn_kernel, n_kb),
        grid=(batch, KV_HEADS, seq // ATT_Q),
        in_specs=[pl.BlockSpec((1, ATT_Q, gw), lambda b, h, i: (b, i, h)),
                  pl.BlockSpec((1, t, HEAD_DIM), lambda b, h, i: (b, 0, h)),
                  pl.BlockSpec((1, t, 2 * HEAD_DIM), lambda b, h, i: (b, 0, h))],
        out_specs=pl.BlockSpec((1, ATT_Q, gw), lambda b, h, i: (b, i, h)),
        out_shape=jax.ShapeDtypeStruct((batch, seq, nq), BF16),
        compiler_params=_cparams(("arbitrary", "arbitrary", "arbitrary")),
        name="attention",
    )(q, k, v)


def _block_diag(w):
    nb, bs, _ = w.shape
    eye = jnp.eye(nb, dtype=w.dtype)
    return (eye[:, None, :, None] * w[:, :, None, :]).reshape(nb * bs, nb * bs)


def _router_weights(wg, bg, we, be):
    d = wg.shape[0]
    n = N_GROUPS + N_EXPERTS
    wr = jnp.zeros((ROUTER_ROWS, d), F32).at[:N_GROUPS].set(wg.T).at[N_GROUPS:n].set(we.T)
    br = jnp.zeros((ROUTER_ROWS, 1), F32).at[:N_GROUPS, 0].set(bg).at[N_GROUPS:n, 0].set(be)
    w_hi = wr.astype(BF16)
    w_lo = (wr - w_hi.astype(F32)).astype(BF16)
    return jnp.concatenate([w_hi, w_lo], axis=0), br


def _rope_tables(seq):
    rows = seq // GRID_W
    row = np.repeat(np.arange(rows, dtype=np.float32), GRID_W)
    col = np.tile(np.arange(GRID_W, dtype=np.float32), rows)
    ppa = HEAD_DIM // 4
    freqs = (ROPE_THETA ** (-np.arange(ppa, dtype=np.float32) / ppa)).astype(np.float32)
    ang = np.concatenate([row[:, None] * freqs, col[:, None] * freqs], axis=-1)
    cos, sin = np.cos(ang), np.sin(ang)
    return (jnp.asarray(np.concatenate([cos, cos], axis=-1), F32),
            jnp.asarray(np.concatenate([-sin, sin], axis=-1), F32))


_HALF_SPLIT = np.concatenate([np.arange(0, HEAD_DIM, 2), np.arange(1, HEAD_DIM, 2)])


def kernel(x, c, ctx, c_ctx, norm1, norm2, w_ada, b_ada, ev_w_in, ev_conv_w, ev_conv_b, ev_lru_wa, ev_lru_ba, ev_lru_wi, ev_lru_bi, ev_lru_lam, ev_gla_wg, ev_gla_bg, ev_gla_norm, ev_w_out, od_w_qkv, od_q_norm, od_k_norm, od_w_o, moe_wg, moe_bg, moe_we, moe_be, moe_w1, moe_w3, moe_w2):
    batch, seq, d = x.shape
    ctx_len = ctx.shape[1]
    assert seq % TOK == 0 and ctx_len % TOK == 0 and d % GLA_VAL == 0 and seq % GRID_W == 0
    ncc = ctx_len // TOK
    lw = d // 2

    assert batch % TOK_ROWS == 0
    rows = -(-(batch + TOK_ROWS) // SUBLANES) * SUBLANES
    cv = jnp.zeros((rows, d), F32).at[:batch].set(c).at[batch:batch + TOK_ROWS].set(c_ctx)
    table = _adaln(cv, w_ada, b_ada)
    depth = table.shape[0]
    table = table.reshape(depth, rows, 6, d).transpose(0, 2, 1, 3).reshape(depth * 6, rows, 1, d)
    mods = _Mods(table, batch, ncc)

    w_in = ev_w_in[0]
    nm = 2 * lw + 2 * GLA_KEY + 2 * GLA_VAL
    w_main = w_in[:, :nm].astype(BF16)
    w_lr = jnp.zeros((d, LANES), F32).at[:, :2 * GLA_RANK].set(w_in[:, nm:]).astype(BF16)
    pm, lr = _inproj(x, ctx, norm1[0][None], mods, w_main, w_lr, ev_conv_w[0], ev_conv_b[0][None], lw)

    mixed = []
    for dr in range(2):
        w_gate = jnp.concatenate([_block_diag(ev_lru_wa[0, dr]), _block_diag(ev_lru_wi[0, dr])], axis=1).astype(BF16)
        b_gate = jnp.concatenate([ev_lru_ba[0, dr], ev_lru_bi[0, dr]])[None]
        wg_pad = jnp.zeros((LANES, GLA_KEY), F32).at[dr * GLA_RANK:(dr + 1) * GLA_RANK].set(ev_gla_wg[0, dr]).astype(BF16)
        mixed.append(_seqmix(dr == 1, ncc, pm, lr, w_gate, b_gate,
                             ev_lru_lam[0, dr][None], wg_pad, ev_gla_bg[0, dr][None], lw))
    (lru_f, gla_f), (lru_r, gla_r) = mixed

    wr0, br0 = _router_weights(moe_wg[0], moe_bg[0], moe_we[0], moe_be[0])
    xmid0, h2e0, cr0, cnt0 = _post0(ncc, lru_f, lru_r, gla_f, gla_r, pm, x, ctx, mods, norm2[0][None],
                                    ev_gla_norm[0][None], ev_w_out[0].astype(BF16), wr0, br0, lw)
    x1 = _moe_block(h2e0, cr0, cnt0, xmid0, mods.spec(0, 5), mods.table, 0, moe_w1, moe_w3, moe_w2)

    nq = ATT_HEADS * HEAD_DIM
    nk = KV_HEADS * HEAD_DIM
    perm = np.concatenate([h * HEAD_DIM + _HALF_SPLIT for h in range(ATT_HEADS + KV_HEADS)]
                          + [np.arange(nq + nk, nq + 2 * nk)])
    w_qkv = od_w_qkv[0][:, perm].astype(BF16)
    cos, sin = _rope_tables(seq)
    q, k, v = _qkv(ncc, x1, norm1[1][None], mods, w_qkv, od_q_norm[0][_HALF_SPLIT][None],
                   od_k_norm[0][_HALF_SPLIT][None], cos, sin)
    att = _attention(q, k, v)
    wr1, br1 = _router_weights(moe_wg[1], moe_bg[1], moe_we[1], moe_be[1])
    xmid1, h2e1, cr1, cnt1 = _post1(ncc, att, x1, mods, norm2[1][None], od_w_o[0].astype(BF16), wr1, br1)
    return _moe_block(h2e1, cr1, cnt1, xmid1, mods.spec(1, 5, True), mods.table, 1, moe_w1, moe_w3, moe_w2)
```

```python
import functools
import itertools

import numpy as np
import jax
import jax.numpy as jnp
from jax import lax
from jax.experimental import pallas as pl
from jax.experimental.pallas import tpu as pltpu

F32 = jnp.float32
BF16 = jnp.bfloat16
I32 = jnp.int32
U32 = jnp.uint32
HIGHEST = lax.Precision.HIGHEST

EPS = 1e-6
GRID_W = 64
LRU_BLOCKS = 8
LRU_C = 8.0
GLA_HEADS = 4
GLA_DK = 64
GLA_DV = 128
GLA_KEY = GLA_HEADS * GLA_DK
GLA_VAL = GLA_HEADS * GLA_DV
GLA_RANK = 16
GLA_TAU = 16.0
GLA_CHUNK = 64
_CHUNK_SHIFT = GLA_CHUNK.bit_length() - 1
_DK_SHIFT = GLA_DK.bit_length() - 1
ATT_HEADS = 8
KV_HEADS = 2
Q_PER_KV = ATT_HEADS // KV_HEADS
HEAD_DIM = 128
ROPE_THETA = 10000.0
N_GROUPS = 4
EXPERTS_PER_GROUP = 4
N_EXPERTS = N_GROUPS * EXPERTS_PER_GROUP
PAIRS_PER_GROUP = 6
N_CLASSES = N_GROUPS * PAIRS_PER_GROUP

LANES = 128
SUBLANES = 8
TOK = 256
MOE_TILE = 512
ATT_Q = 256
PERM_ROWS = 512
TOK_ROWS = 4
QKV_ROWS = 2
SEQ_ROWS = 4
WEXT = LANES
ROUTER_ROWS = 32
VMEM_LIMIT = 56 * 1024 * 1024


def _cparams(sem):
    return pltpu.CompilerParams(dimension_semantics=sem, vmem_limit_bytes=VMEM_LIMIT)


def _rms(x, g):
    return x * lax.rsqrt(jnp.mean(x * x, axis=-1, keepdims=True) + EPS) * g


def _sigmoid(x):
    return 1.0 / (1.0 + jnp.exp(-x))


def _silu(x):
    return x * _sigmoid(x)


def _gelu_tanh(x):
    return 0.5 * x * (1.0 + jnp.tanh(np.sqrt(2.0 / np.pi).astype(np.float32) * (x + 0.044715 * (x * x * x))))


def _pack_bf16_pairs(xb):
    k = xb.shape[-1] // 2
    lo = lax.shift_right_logical(pltpu.bitcast(xb[:, :k].astype(F32), U32), jnp.uint32(16))
    hi = pltpu.bitcast(xb[:, k:].astype(F32), U32) & jnp.uint32(0xFFFF0000)
    return hi | lo


def _unpack_bf16_pairs(words):
    lo = pltpu.bitcast(lax.shift_left(words, jnp.uint32(16)), F32).astype(BF16)
    hi = pltpu.bitcast(words & jnp.uint32(0xFFFF0000), F32).astype(BF16)
    return jnp.concatenate([lo, hi], axis=1)


def _softplus(x):
    return jnp.maximum(x, 0.0) + jnp.log(1.0 + jnp.exp(-jnp.abs(x)))


def _log_sigmoid(x):
    return -_softplus(-x)


def _adaln_kernel(cv_ref, w_ref, b_ref, o_ref):
    s = _silu(cv_ref[...])
    o_ref[0] = jnp.dot(s, w_ref[0], precision=HIGHEST, preferred_element_type=F32) + b_ref[0]


def _adaln(cv, w_ada, b_ada):
    depth, d, n6 = w_ada.shape
    rows = cv.shape[0]
    tn = 6 * d // 4
    return pl.pallas_call(
        _adaln_kernel,
        grid=(depth, n6 // tn),
        in_specs=[pl.BlockSpec((rows, d), lambda l, j: (0, 0)),
                  pl.BlockSpec((1, d, tn), lambda l, j: (l, 0, j)),
                  pl.BlockSpec((1, 1, tn), lambda l, j: (l, 0, j))],
        out_specs=pl.BlockSpec((1, rows, tn), lambda l, j: (l, 0, j)),
        out_shape=jax.ShapeDtypeStruct((depth, rows, n6), F32),
        compiler_params=_cparams(("arbitrary", "arbitrary")),
        name="adaln",
    )(cv, w_ada, b_ada.reshape(depth, 1, n6))


class _Mods:
    def __init__(self, table, batch, ncc):
        self.table, self.batch, self.ncc = table, batch, ncc
        self.d = table.shape[-1]

    def spec(self, layer, j, latent_only=False, nb=1):
        batch, ncc = self.batch, self.ncc

        def imap(b, c):
            r = b if latent_only else jnp.where(c < ncc, batch // nb, b)
            return (layer * 6 + j, r, 0, 0)

        return pl.BlockSpec((1, nb, 1, self.d), imap)


def _lockstep(generators):
    for _ in itertools.zip_longest(*generators):
        pass


def _inproj_kernel(ncc, nch, lw, x_ref, c_ref, xp_ref, xn_ref, cp_ref, cn_ref, g_ref, sh_ref, sc_ref,
                   w_ref, wlr_ref, cw_ref, cb_ref, pm_ref, lr_ref):
    _lockstep([_inproj_row(ncc, nch, lw, i, x_ref, c_ref, xp_ref, xn_ref, cp_ref, cn_ref, g_ref, sh_ref, sc_ref,
                           w_ref, wlr_ref, cw_ref, cb_ref, pm_ref, lr_ref) for i in range(x_ref.shape[0])])


def _inproj_row(ncc, nch, lw, i, x_ref, c_ref, xp_ref, xn_ref, cp_ref, cn_ref, g_ref, sh_ref, sc_ref,
                w_ref, wlr_ref, cw_ref, cb_ref, pm_ref, lr_ref):
    c = pl.program_id(1)
    is_ctx = c < ncc

    def modulated(rows):
        return (_rms(rows, g_ref[...]) * (1.0 + sc_ref[0, i]) + sh_ref[0, i]).astype(BF16)

    hb = modulated(jnp.where(is_ctx, c_ref[i], x_ref[i]))
    pm = jnp.dot(hb, w_ref[...], preferred_element_type=F32)
    yield
    lr_ref[i] = jnp.dot(hb, wlr_ref[...], preferred_element_type=F32).astype(BF16)

    halo = jnp.concatenate([jnp.where(is_ctx, cp_ref[i], xp_ref[i]), jnp.where(is_ctx, cn_ref[i], xn_ref[i])], axis=0)
    xa_halo = jnp.dot(modulated(halo), w_ref[:, :lw], preferred_element_type=F32)
    yield
    has_prev = jnp.logical_and(c != 0, c != ncc)
    has_next = jnp.logical_and(c != ncc - 1, c != nch - 1)
    p2 = jnp.where(has_prev, xa_halo[SUBLANES - 2:SUBLANES - 1], 0.0)
    p1 = jnp.where(has_prev, xa_halo[SUBLANES - 1:SUBLANES], 0.0)
    n0 = jnp.where(has_next, xa_halo[SUBLANES:SUBLANES + 1], 0.0)
    xa = pm[:, :lw]
    row = lax.broadcasted_iota(I32, xa.shape, 0)
    x_m1 = jnp.where(row == 0, p1, pltpu.roll(xa, 1, 0))
    x_m2 = jnp.where(row == 0, p2, jnp.where(row == 1, p1, pltpu.roll(xa, 2, 0)))
    x_p1 = jnp.where(row == TOK - 1, n0, pltpu.roll(xa, TOK - 1, 0))
    cw = cw_ref[...]
    u = cw[0:1] * x_m2 + cw[1:2] * x_m1 + cw[2:3] * xa + cw[3:4] * x_p1 + cb_ref[...]
    pm_ref[i, :, :lw] = u.astype(BF16)
    pm_ref[i, :, lw:] = pm[:, lw:].astype(BF16)


def _inproj(x, ctx, gain, mods, w_main, w_lr, conv_w, conv_b, lw):
    batch, seq, d = x.shape
    ncc = ctx.shape[1] // TOK
    nl = seq // TOK
    nch = ncc + nl
    nm = w_main.shape[1]
    hb = TOK // SUBLANES
    nb = TOK_ROWS
    return pl.pallas_call(
        functools.partial(_inproj_kernel, ncc, nch, lw),
        grid=(batch // nb, nch),
        in_specs=[pl.BlockSpec((nb, TOK, d), lambda b, c: (b, jnp.maximum(c - ncc, 0), 0)),
                  pl.BlockSpec((nb, TOK, d), lambda b, c: (b, jnp.minimum(c, ncc - 1), 0)),
                  pl.BlockSpec((nb, SUBLANES, d), lambda b, c: (b, jnp.maximum((c - ncc) * hb - 1, 0), 0)),
                  pl.BlockSpec((nb, SUBLANES, d),
                               lambda b, c: (b, jnp.clip((c - ncc + 1) * hb, 0, nl * hb - 1), 0)),
                  pl.BlockSpec((nb, SUBLANES, d), lambda b, c: (b, jnp.clip(c * hb - 1, 0, ncc * hb - 1), 0)),
                  pl.BlockSpec((nb, SUBLANES, d), lambda b, c: (b, jnp.clip((c + 1) * hb, 0, ncc * hb - 1), 0)),
                  pl.BlockSpec((1, d), lambda b, c: (0, 0)),
                  mods.spec(0, 0, nb=nb), mods.spec(0, 1, nb=nb),
                  pl.BlockSpec((d, nm), lambda b, c: (0, 0)),
                  pl.BlockSpec((d, LANES), lambda b, c: (0, 0)),
                  pl.BlockSpec((4, lw), lambda b, c: (0, 0)),
                  pl.BlockSpec((1, lw), lambda b, c: (0, 0))],
        out_specs=[pl.BlockSpec((nb, TOK, nm), lambda b, c: (b, c, 0)),
                   pl.BlockSpec((nb, TOK, LANES), lambda b, c: (b, c, 0))],
        out_shape=[jax.ShapeDtypeStruct((batch, nch * TOK, nm), BF16),
                   jax.ShapeDtypeStruct((batch, nch * TOK, LANES), BF16)],
        compiler_params=_cparams(("arbitrary", "arbitrary")),
        name="inproj",
    )(x, ctx, x, x, ctx, ctx, gain, mods.table, mods.table, w_main, w_lr, conv_w, conv_b)


def _lru_scan(a, b, h0, rev):
    n_groups = TOK // SUBLANES
    a = a.reshape(n_groups, SUBLANES, a.shape[-1])
    b = b.reshape(a.shape)
    sub = lax.broadcasted_iota(I32, a.shape, 1)
    for dist in (1, 2, 4):
        shift = SUBLANES - dist if rev else dist
        a_s = pltpu.roll(a, shift, 1)
        b_s = pltpu.roll(b, shift, 1)
        m = (sub < SUBLANES - dist) if rev else (sub >= dist)
        b = jnp.where(m, a * b_s + b, b)
        a = jnp.where(m, a * a_s, a)
    order = range(n_groups - 1, -1, -1) if rev else range(n_groups)
    outs = [None] * n_groups
    h = h0
    for r in order:
        hr = a[r] * h + b[r]
        outs[r] = hr
        h = hr[0:1] if rev else hr[SUBLANES - 1:SUBLANES]
    return jnp.concatenate(outs, axis=0), h


def _seqmix_kernel(rev, *refs):
    h_scr, s_scr = refs[-2:]

    @pl.when(pl.program_id(1) == 0)
    def _():
        h_scr[...] = jnp.zeros_like(h_scr)
        s_scr[...] = jnp.zeros_like(s_scr)

    _lockstep([_seqmix_row(rev, i, *refs) for i in range(refs[0].shape[0])])


def _seqmix_row(rev, i, u_ref, q_ref, k_ref, v_ref, lr_ref,
                wgt_ref, bgt_ref, lam_ref, wg_ref, bg_ref, sel_ref,
                lru_ref, gla_ref, h_scr, s_scr):
    lw = u_ref.shape[-1]
    ub = u_ref[i]
    u = ub.astype(F32)
    gates = jnp.dot(ub, wgt_ref[...], preferred_element_type=F32) + bgt_ref[...]
    yield
    r_gate = _sigmoid(gates[:, :lw])
    i_gate = _sigmoid(gates[:, lw:])
    log_a = (-LRU_C) * r_gate * _softplus(-lam_ref[...])
    a = jnp.exp(log_a)
    bb = jnp.sqrt(1.0 - a * a) * (i_gate * u)
    hs, h_last = _lru_scan(a, bb, h_scr[i, 0:1], rev)
    h_scr[i, 0:1] = h_last
    lru_ref[i] = hs.astype(BF16)
    yield

    lg = _log_sigmoid(jnp.dot(lr_ref[i], wg_ref[...], preferred_element_type=F32) + bg_ref[...]) * (1.0 / GLA_TAU)
    lg_hi = lg.astype(BF16)
    lg_lo = (lg - lg_hi.astype(F32)).astype(BF16)
    yield
    cum = jnp.dot(sel_ref[...], jnp.concatenate([lg_hi, lg_lo], axis=1), preferred_element_type=F32)
    yield
    g = cum[:TOK, :GLA_KEY] + cum[:TOK, GLA_KEY:]
    tot = cum[TOK:, :GLA_KEY] + cum[TOK:, GLA_KEY:]
    q = q_ref[i].astype(F32)
    k = k_ref[i].astype(F32)
    dec_all = jnp.exp(tot)
    k_neg = k * jnp.exp(-g)
    qg = (q * jnp.exp(g) * (GLA_DK ** -0.5)).astype(BF16)
    kg = k_neg.astype(BF16)
    kd = (k_neg * dec_all).astype(BF16)
    v = v_ref[i]

    lane = lax.broadcasted_iota(I32, (1, GLA_KEY), 1) >> _DK_SHIFT
    head_masks = [(lane == h) for h in range(GLA_HEADS)]
    n_stack = GLA_HEADS * GLA_CHUNK
    ai = lax.broadcasted_iota(I32, (n_stack, GLA_CHUNK), 0) & (GLA_CHUNK - 1)
    aj = lax.broadcasted_iota(I32, (n_stack, GLA_CHUNK), 1)
    causal = (aj >= ai) if rev else (aj <= ai)

    n_sub = TOK // GLA_CHUNK
    order = range(n_sub - 1, -1, -1) if rev else range(n_sub)
    outs = [None] * n_sub
    st = s_scr[i]
    for n in order:
        sl = slice(n * GLA_CHUNK, (n + 1) * GLA_CHUNK)
        qn = qg[sl]
        qs = jnp.concatenate([jnp.where(head_masks[h], qn, jnp.zeros_like(qn)) for h in range(GLA_HEADS)], axis=0)
        att = lax.dot_general(qs, kg[sl], (((1,), (1,)), ((), ())), preferred_element_type=F32)
        yield
        att = jnp.where(causal, att, 0.0).astype(BF16)
        o_all = jnp.dot(att, v[sl], preferred_element_type=F32)
        yield
        o_int = lax.dot_general(qs, st.astype(BF16), (((1,), (1,)), ((), ())),
                                preferred_element_type=F32)
        yield
        outs[n] = jnp.concatenate(
            [o_all[h * GLA_CHUNK:(h + 1) * GLA_CHUNK, h * GLA_DV:(h + 1) * GLA_DV]
             + o_int[h * GLA_CHUNK:(h + 1) * GLA_CHUNK] for h in range(GLA_HEADS)], axis=1)
        vn, kdn = v[sl], kd[sl]
        v_stack = jnp.concatenate([vn[:, h * GLA_DV:(h + 1) * GLA_DV] for h in range(GLA_HEADS)], axis=0)
        kd_stack = jnp.concatenate([jnp.where(head_masks[h], kdn, jnp.zeros_like(kdn)) for h in range(GLA_HEADS)],
                                   axis=0)
        ds = lax.dot_general(v_stack, kd_stack, (((0,), (0,)), ((), ())), preferred_element_type=F32)
        yield
        st = dec_all[n * GLA_CHUNK:n * GLA_CHUNK + 1] * st + ds
    s_scr[i] = st
    gla_ref[i] = jnp.concatenate(outs, axis=0).astype(BF16)


def _cumsum_selector(rev):
    r = np.arange(TOK)[:, None]
    c = np.arange(TOK)[None, :]
    same = (r // GLA_CHUNK) == (c // GLA_CHUNK)
    tri = same & ((c >= r) if rev else (c <= r))
    return jnp.asarray(np.concatenate([tri, same], axis=0), BF16)


def _seqmix(rev, ncc, pm, lr, w_gate, b_gate, lam, wg_pad, bg, lw):
    batch, t, _ = pm.shape
    nch = t // TOK
    nl = nch - ncc

    def chunk(s):
        if rev:
            return jnp.where(s < ncc, ncc - 1 - s, ncc + (nl - 1) - (s - ncc))
        return s

    qcol = 2 * lw // GLA_KEY
    vcol = (2 * lw + 2 * GLA_KEY) // GLA_VAL
    const = lambda shape: pl.BlockSpec(shape, lambda b, s: (0,) * len(shape))
    nb = SEQ_ROWS if batch % SEQ_ROWS == 0 else 1
    return pl.pallas_call(
        functools.partial(_seqmix_kernel, rev),
        grid=(batch // nb, nch),
        in_specs=[pl.BlockSpec((nb, TOK, lw), lambda b, s: (b, chunk(s), 0)),
                  pl.BlockSpec((nb, TOK, GLA_KEY), lambda b, s: (b, chunk(s), qcol)),
                  pl.BlockSpec((nb, TOK, GLA_KEY), lambda b, s: (b, chunk(s), qcol + 1)),
                  pl.BlockSpec((nb, TOK, GLA_VAL), lambda b, s: (b, chunk(s), vcol)),
                  pl.BlockSpec((nb, TOK, LANES), lambda b, s: (b, chunk(s), 0)),
                  const((lw, 2 * lw)), const((1, 2 * lw)), const((1, lw)),
                  const((LANES, GLA_KEY)), const((1, GLA_KEY)), const((2 * TOK, TOK))],
        out_specs=[pl.BlockSpec((nb, TOK, lw), lambda b, s: (b, chunk(s), 0)),
                   pl.BlockSpec((nb, TOK, GLA_VAL), lambda b, s: (b, chunk(s), 0))],
        out_shape=[jax.ShapeDtypeStruct((batch, t, lw), BF16),
                   jax.ShapeDtypeStruct((batch, t, GLA_VAL), BF16)],
        scratch_shapes=[pltpu.VMEM((nb, SUBLANES, lw), F32), pltpu.VMEM((nb, GLA_DV, GLA_KEY), F32)],
        compiler_params=_cparams(("arbitrary", "arbitrary")),
        name="seqmix_rev" if rev else "seqmix_fwd",
    )(pm, pm, pm, pm, lr, w_gate, b_gate, lam, wg_pad, bg, _cumsum_selector(rev))


def _route_epilogue(i, first, xm, n2_ref, sh2_ref, sc2_ref, wr_ref, br_ref,
                    xmid_ref, h2_ref, cr_ref, cnt_ref, carry):
    d = xm.shape[-1]
    xmid_ref[i] = xm
    h2 = _rms(xm, n2_ref[...]) * (1.0 + sc2_ref[0, i]) + sh2_ref[0, i]
    h_hi = h2.astype(BF16)
    h_lo = (h2 - h_hi.astype(F32)).astype(BF16)
    nt = (((1,), (1,)), ((), ()))
    o1 = lax.dot_general(wr_ref[...], h_hi, nt, preferred_element_type=F32)
    o2 = lax.dot_general(wr_ref[:ROUTER_ROWS], h_lo, nt, preferred_element_type=F32)
    yield
    logits = o1[:ROUTER_ROWS] + o1[ROUTER_ROWS:] + o2 + br_ref[...]

    def col(j):
        return logits[j:j + 1]

    lgs = [col(g) for g in range(N_GROUPS)]
    gmax = functools.reduce(jnp.maximum, lgs)
    gi = jnp.where(lgs[0] == gmax, 0, jnp.where(lgs[1] == gmax, 1, jnp.where(lgs[2] == gmax, 2, 3)))
    w_group = 1.0 / functools.reduce(lambda p, q: p + q, [jnp.exp(l - gmax) for l in lgs])
    es = []
    for j in range(EXPERTS_PER_GROUP):
        acc = jnp.zeros_like(gmax)
        for g in range(N_GROUPS):
            acc = acc + jnp.where(gi == g, col(N_GROUPS + g * EXPERTS_PER_GROUP + j), 0.0)
        es.append(acc)
    m1 = functools.reduce(jnp.maximum, es)
    i1 = jnp.where(es[0] == m1, 0, jnp.where(es[1] == m1, 1, jnp.where(es[2] == m1, 2, 3)))
    rest = [jnp.where(i1 == j, -jnp.inf, es[j]) for j in range(EXPERTS_PER_GROUP)]
    m2 = functools.reduce(jnp.maximum, rest)
    i2 = jnp.where(rest[0] == m2, 0, jnp.where(rest[1] == m2, 1, jnp.where(rest[2] == m2, 2, 3)))
    e2 = jnp.exp(m2 - m1)
    w1 = w_group / (1.0 + e2)
    w2 = w_group * e2 / (1.0 + e2)
    first_lo = i1 < i2
    lo = jnp.where(first_lo, i1, i2)
    hi = jnp.where(first_lo, i2, i1)
    w_lo = jnp.where(first_lo, w1, w2)
    w_hi = jnp.where(first_lo, w2, w1)
    pidx = jnp.where(lo == 0, hi - 1, jnp.where(lo == 1, hi + 1, 5))
    cls = gi * PAIRS_PER_GROUP + pidx

    def terms(w):
        t1 = w.astype(BF16).astype(F32)
        t2 = (w - t1).astype(BF16).astype(F32)
        t3 = (w - t1 - t2).astype(BF16).astype(F32)
        return [t1, t2, t3]

    sub = lax.broadcasted_iota(I32, (2 * SUBLANES, TOK), 0)
    stacked = jnp.zeros((2 * SUBLANES, TOK), F32)
    for r, term in enumerate(terms(w_lo) + terms(w_hi)):
        stacked = jnp.where(sub == r, term, stacked)
    er = lax.broadcasted_iota(I32, (2 * SUBLANES, LANES), 0)
    el = lax.broadcasted_iota(I32, (2 * SUBLANES, LANES), 1)
    place = jnp.logical_or(jnp.logical_and(er < 3, el == 0),
                           jnp.logical_and(jnp.logical_and(er >= 3, er < 6), el == 1))
    w_cols = lax.dot_general(stacked.astype(BF16), place.astype(BF16), (((0,), (0,)), ((), ())),
                             preferred_element_type=F32)
    h2_ref[i, :, :d // 2] = _pack_bf16_pairs(h_hi)
    h2_ref[i, :, d // 2:] = pltpu.bitcast(w_cols, U32)

    krow = lax.broadcasted_iota(I32, (ROUTER_ROWS, TOK), 0)
    onehot = krow == cls
    ri = lax.broadcasted_iota(I32, (TOK, TOK), 0)
    ci = lax.broadcasted_iota(I32, (TOK, TOK), 1)
    triu = (ri <= ci).astype(BF16)
    pref = jnp.dot(onehot.astype(BF16), triu, preferred_element_type=F32)
    yield

    if i == 0:
        @pl.when(first)
        def _():
            carry[...] = jnp.zeros_like(carry)

    base = carry[:, 0:1]
    rank = jnp.sum(jnp.where(onehot, pref - 1.0 + base, 0.0), axis=0, keepdims=True)
    new_carry = jnp.broadcast_to(base + jnp.sum(onehot.astype(F32), axis=1, keepdims=True), carry.shape)
    carry[...] = new_carry
    cnt_ref[...] = new_carry
    sub8 = lax.broadcasted_iota(I32, (SUBLANES, TOK), 0)
    cr_ref[i, 0] = jnp.where(sub8 == 0, cls, jnp.where(sub8 == 1, rank.astype(I32), 0))


def _post0_kernel(ncc, *refs):
    _lockstep([_post0_row(ncc, i, *refs) for i in range(refs[0].shape[0])])


def _post0_row(ncc, i, l0_ref, l1_ref, g0_ref, g1_ref, ga_ref, r_ref, x_ref, c_ref,
               gate_ref, sh2_ref, sc2_ref, n2_ref, gn_ref, wout_ref, wr_ref, br_ref,
               xmid_ref, h2_ref, cr_ref, cnt_ref, carry):
    b, c = pl.program_id(0), pl.program_id(1)
    lru = l0_ref[i].astype(F32) + l1_ref[i].astype(F32)
    ya = lru * _gelu_tanh(ga_ref[i].astype(F32))
    gla = g0_ref[i].astype(F32) + g1_ref[i].astype(F32)
    parts = []
    for h in range(GLA_HEADS):
        parts.append(_rms(gla[:, h * GLA_DV:(h + 1) * GLA_DV], gn_ref[...]))
    yb = jnp.concatenate(parts, axis=1) * _silu(r_ref[i].astype(F32))
    ycat = jnp.concatenate([ya, yb], axis=1).astype(BF16)
    y = jnp.dot(ycat, wout_ref[...], preferred_element_type=F32)
    yield
    x0 = jnp.where(c < ncc, c_ref[i], x_ref[i])
    xm = x0 + gate_ref[0, i] * y
    yield from _route_epilogue(i, jnp.logical_and(b == 0, c == 0), xm, n2_ref, sh2_ref, sc2_ref, wr_ref, br_ref,
                               xmid_ref, h2_ref, cr_ref, cnt_ref, carry)


def _post1_kernel(*refs):
    _lockstep([_post1_row(i, *refs) for i in range(refs[0].shape[0])])


def _post1_row(i, o_ref, x_ref, gate_ref, sh2_ref, sc2_ref, n2_ref, wo_ref, wr_ref, br_ref,
               xmid_ref, h2_ref, cr_ref, cnt_ref, carry):
    b, c = pl.program_id(0), pl.program_id(1)
    y = jnp.dot(o_ref[i], wo_ref[...], preferred_element_type=F32)
    yield
    xm = x_ref[i] + gate_ref[0, i] * y
    yield from _route_epilogue(i, jnp.logical_and(b == 0, c == 0), xm, n2_ref, sh2_ref, sc2_ref, wr_ref, br_ref,
                               xmid_ref, h2_ref, cr_ref, cnt_ref, carry)


def _route_out(batch, nch, d, nb):
    specs = [pl.BlockSpec((nb, TOK, d), lambda b, c: (b, c, 0)),
             pl.BlockSpec((nb, TOK, d // 2 + WEXT), lambda b, c: (b, c, 0)),
             pl.BlockSpec((nb, 1, SUBLANES, TOK), lambda b, c: (b, c, 0, 0)),
             pl.BlockSpec((ROUTER_ROWS, LANES), lambda b, c: (0, 0))]
    shapes = [jax.ShapeDtypeStruct((batch, nch * TOK, d), F32),
              jax.ShapeDtypeStruct((batch, nch * TOK, d // 2 + WEXT), U32),
              jax.ShapeDtypeStruct((batch, nch, SUBLANES, TOK), I32),
              jax.ShapeDtypeStruct((ROUTER_ROWS, LANES), F32)]
    return specs, shapes


def _post0(ncc, lru_f, lru_r, gla_f, gla_r, pm, x, ctx, mods, n2, gn, w_out, wr, br, lw):
    batch, t, _ = pm.shape
    d = x.shape[-1]
    nch = t // TOK
    rcol = (2 * lw + 2 * GLA_KEY + GLA_VAL) // GLA_VAL
    nb = TOK_ROWS
    tokspec = lambda w: pl.BlockSpec((nb, TOK, w), lambda b, c: (b, c, 0))
    const = lambda shape: pl.BlockSpec(shape, lambda b, c: (0,) * len(shape))
    out_specs, out_shape = _route_out(batch, nch, d, nb)
    return pl.pallas_call(
        functools.partial(_post0_kernel, ncc),
        grid=(batch // nb, nch),
        in_specs=[tokspec(lw), tokspec(lw), tokspec(GLA_VAL), tokspec(GLA_VAL),
                  pl.BlockSpec((nb, TOK, lw), lambda b, c: (b, c, 1)),
                  pl.BlockSpec((nb, TOK, GLA_VAL), lambda b, c: (b, c, rcol)),
                  pl.BlockSpec((nb, TOK, d), lambda b, c: (b, jnp.maximum(c - ncc, 0), 0)),
                  pl.BlockSpec((nb, TOK, d), lambda b, c: (b, jnp.minimum(c, ncc - 1), 0)),
                  mods.spec(0, 2, nb=nb), mods.spec(0, 3, nb=nb), mods.spec(0, 4, nb=nb),
                  const((1, d)), const((1, GLA_DV)), const((lw + GLA_VAL, d)),
                  const((2 * ROUTER_ROWS, d)), const((ROUTER_ROWS, 1))],
        out_specs=out_specs, out_shape=out_shape,
        scratch_shapes=[pltpu.VMEM((ROUTER_ROWS, LANES), F32)],
        compiler_params=_cparams(("arbitrary", "arbitrary")),
        name="post0",
    )(lru_f, lru_r, gla_f, gla_r, pm, pm, x, ctx, mods.table, mods.table, mods.table, n2, gn, w_out, wr, br)


def _post1(ncc, att, x1, mods, n2, w_o, wr, br):
    batch, seq, aw = att.shape
    d = x1.shape[-1]
    nl = seq // TOK
    const = lambda shape: pl.BlockSpec(shape, lambda b, c: (0,) * len(shape))
    nb = TOK_ROWS
    out_specs, out_shape = _route_out(batch, nl, d, nb)
    return pl.pallas_call(
        _post1_kernel,
        grid=(batch // nb, nl),
        in_specs=[pl.BlockSpec((nb, TOK, aw), lambda b, c: (b, c, 0)),
                  pl.BlockSpec((nb, TOK, d), lambda b, c: (b, c + ncc, 0)),
                  mods.spec(1, 2, True, nb), mods.spec(1, 3, True, nb), mods.spec(1, 4, True, nb),
                  const((1, d)), const((aw, d)), const((2 * ROUTER_ROWS, d)), const((ROUTER_ROWS, 1))],
        out_specs=out_specs, out_shape=out_shape,
        scratch_shapes=[pltpu.VMEM((ROUTER_ROWS, LANES), F32)],
        compiler_params=_cparams(("arbitrary", "arbitrary")),
        name="post1",
    )(att, x1, mods.table, mods.table, mods.table, n2, w_o, wr, br)


_PAIR_LO = np.array([0, 0, 0, 1, 1, 2], np.int32)
_PAIR_HI = np.array([1, 2, 3, 2, 3, 3], np.int32)


def _plan(cr, cnt, n_tiles):
    cls, rank = cr[:, :, 0, :], cr[:, :, 1, :]
    counts = cnt[:N_CLASSES, 0].astype(I32)
    tiles = (counts + MOE_TILE - 1) // MOE_TILE
    ends = jnp.cumsum(tiles)
    offs = (ends - tiles) * MOE_TILE
    pos = rank
    for c in range(N_CLASSES):
        pos = pos + jnp.where(cls == c, offs[c], 0)
    pos = pos.reshape(-1)
    tile_ids = jnp.arange(n_tiles, dtype=I32)
    tile_cls = jnp.minimum(jnp.sum((tile_ids[:, None] >= ends[None, :]).astype(I32), axis=1), N_CLASSES - 1)
    grp = tile_cls // PAIRS_PER_GROUP
    pair = tile_cls % PAIRS_PER_GROUP
    lo = grp * EXPERTS_PER_GROUP + jnp.asarray(_PAIR_LO)[pair]
    hi = grp * EXPERTS_PER_GROUP + jnp.asarray(_PAIR_HI)[pair]
    return pos.astype(I32), lo.astype(I32), hi.astype(I32), ends[-1:].astype(I32)


def _moe_kernel(d, lo_ref, hi_ref, used_ref, x_ref, w1l, w3l, w2l, w1h, w3h, w2h, y_ref, wbuf, wbuf2):
    i = pl.program_id(0)
    active = i < used_ref[0]

    @pl.when(active)
    def _():
        changed = jnp.logical_or(i == 0, jnp.logical_or(lo_ref[i] != lo_ref[jnp.maximum(i - 1, 0)],
                                                        hi_ref[i] != hi_ref[jnp.maximum(i - 1, 0)]))

        @pl.when(changed)
        def _():
            for k, w in enumerate((w1l, w3l, w1h, w3h)):
                wbuf[k] = w[0, 0].astype(BF16)
            wbuf2[0] = w2l[0, 0].astype(BF16)
            wbuf2[1] = w2h[0, 0].astype(BF16)

        x = _unpack_bf16_pairs(x_ref[:, :d // 2])
        wts = pltpu.bitcast(x_ref[:, d // 2:], F32)

        def expert(k1, k3, k2):
            a = jnp.dot(x, wbuf[k1], preferred_element_type=F32)
            b = jnp.dot(x, wbuf[k3], preferred_element_type=F32)
            return jnp.dot((_silu(a) * b).astype(BF16), wbuf2[k2], preferred_element_type=F32)

        y = wts[:, 0:1] * expert(0, 1, 0) + wts[:, 1:2] * expert(2, 3, 1)
        y_ref[...] = _pack_bf16_pairs(y.astype(BF16))

    @pl.when(jnp.logical_not(active))
    def _():
        y_ref[...] = jnp.zeros_like(y_ref)


def _moe(xs, lo, hi, n_used, layer, w1, w3, w2):
    n_pad, wd = xs.shape
    d = 2 * (wd - WEXT)
    hdim = w1.shape[-1]
    n_tiles = n_pad // MOE_TILE
    wl = lambda i, lo, hi, used: (layer, lo[i], 0, 0)
    wh = lambda i, lo, hi, used: (layer, hi[i], 0, 0)
    up = lambda imap: pl.BlockSpec((1, 1, d, hdim), imap)
    down = lambda imap: pl.BlockSpec((1, 1, hdim, d), imap)
    grid_spec = pltpu.PrefetchScalarGridSpec(
        num_scalar_prefetch=3,
        grid=(n_tiles,),
        in_specs=[pl.BlockSpec((MOE_TILE, wd), lambda i, lo, hi, used: (i, 0)),
                  up(wl), up(wl), down(wl), up(wh), up(wh), down(wh)],
        out_specs=pl.BlockSpec((MOE_TILE, d // 2), lambda i, lo, hi, used: (i, 0)),
        scratch_shapes=[pltpu.VMEM((4, d, hdim), BF16), pltpu.VMEM((2, hdim, d), BF16)],
    )
    return pl.pallas_call(
        functools.partial(_moe_kernel, d),
        grid_spec=grid_spec,
        out_shape=jax.ShapeDtypeStruct((n_pad, d // 2), U32),
        compiler_params=_cparams(("arbitrary",)),
        name="moe_experts",
    )(lo, hi, n_used, xs, w1, w3, w2, w1, w3, w2)


def _row_dma_loops(tile_rows, make_copy, index):
    def issue(grp, carry):
        for sub in range(SUBLANES):
            make_copy(grp, sub, index(grp * SUBLANES + sub)).start()
        return carry

    def drain(grp, carry):
        for _ in range(SUBLANES):
            make_copy(0, 0, 0).wait()
        return carry

    n_groups = tile_rows // SUBLANES

    def issue_all():
        for grp in range(n_groups):
            issue(grp, 0)

    return issue_all, (lambda: lax.fori_loop(0, n_groups, drain, 0))


def _scatter_rows_kernel(rows, pos_ref, src_ref, init_ref, dst_ref, sem):
    del init_ref
    issue, drain = _row_dma_loops(
        rows,
        lambda grp, sub, p: pltpu.make_async_copy(src_ref.at[grp, pl.ds(sub, 1)], dst_ref.at[pl.ds(p, 1)], sem),
        lambda j: pos_ref[0, 0, j])
    issue()
    drain()


def _scatter_rows(src, pos, n_pad):
    n, w = src.shape
    rows = PERM_ROWS if n % PERM_ROWS == 0 else TOK
    steps = n // rows
    return pl.pallas_call(
        functools.partial(_scatter_rows_kernel, rows),
        grid=(steps,),
        in_specs=[pl.BlockSpec((1, 1, rows), lambda i: (i, 0, 0), memory_space=pltpu.SMEM),
                  pl.BlockSpec((rows // SUBLANES, SUBLANES, w), lambda i: (i, 0, 0)),
                  pl.BlockSpec(memory_space=pl.ANY)],
        out_specs=pl.BlockSpec(memory_space=pl.ANY),
        out_shape=jax.ShapeDtypeStruct((n_pad, w), src.dtype),
        scratch_shapes=[pltpu.SemaphoreType.DMA(())],
        input_output_aliases={2: 0},
        compiler_params=pltpu.CompilerParams(dimension_semantics=("arbitrary",), has_side_effects=True),
        name="scatter_rows",
    )(pos.reshape(steps, 1, rows), src.reshape(n // SUBLANES, SUBLANES, w), jnp.zeros((n_pad, w), src.dtype))


def _gather_residual_kernel(n_steps, pos_ref, posn_ref, ys_ref, xm_ref, gate_ref, o_ref, buf, sem):
    step = pl.program_id(0) * pl.num_programs(1) + pl.program_id(1)
    slot = step % 2

    def loops(idx_ref, s):
        return _row_dma_loops(
            TOK,
            lambda grp, sub, p: pltpu.make_async_copy(ys_ref.at[pl.ds(p, 1)], buf.at[s, grp, pl.ds(sub, 1)],
                                                      sem.at[s]),
            lambda j: idx_ref[0, 0, j])

    @pl.when(step == 0)
    def _():
        loops(pos_ref, slot)[0]()

    @pl.when(step + 1 < n_steps)
    def _():
        loops(posn_ref, 1 - slot)[0]()

    loops(pos_ref, slot)[1]()
    y = _unpack_bf16_pairs(buf[slot].reshape(TOK, buf.shape[-1])).astype(F32)
    o_ref[0] = xm_ref[0] + gate_ref[0, 0] * y


def _gather_residual(ys, pos, xmid, gate_spec, mod_table):
    batch, t, d = xmid.shape
    nch = t // TOK
    n_steps = batch * nch
    return pl.pallas_call(
        functools.partial(_gather_residual_kernel, n_steps),
        grid=(batch, nch),
        in_specs=[pl.BlockSpec((1, 1, TOK), lambda b, c: (b * nch + c, 0, 0), memory_space=pltpu.SMEM),
                  pl.BlockSpec((1, 1, TOK), lambda b, c: (jnp.minimum(b * nch + c + 1, n_steps - 1), 0, 0),
                               memory_space=pltpu.SMEM),
                  pl.BlockSpec(memory_space=pl.ANY),
                  pl.BlockSpec((1, TOK, d), lambda b, c: (b, c, 0)),
                  gate_spec],
        out_specs=pl.BlockSpec((1, TOK, d), lambda b, c: (b, c, 0)),
        out_shape=jax.ShapeDtypeStruct((batch, t, d), F32),
        scratch_shapes=[pltpu.VMEM((2, TOK // SUBLANES, SUBLANES, d // 2), U32), pltpu.SemaphoreType.DMA((2,))],
        compiler_params=_cparams(("arbitrary", "arbitrary")),
        name="gather_residual",
    )(pos.reshape(n_steps, 1, TOK), pos.reshape(n_steps, 1, TOK), ys, xmid, mod_table)


def _moe_sorted(h2ext, cr, cnt, layer, w1, w3, w2):
    h2ext = h2ext.reshape(-1, h2ext.shape[-1])
    n = h2ext.shape[0]
    n_tiles = n // MOE_TILE + N_CLASSES
    pos, lo, hi, n_used = _plan(cr, cnt, n_tiles)
    xs = _scatter_rows(h2ext, pos, n_tiles * MOE_TILE)
    return _moe(xs, lo, hi, n_used, layer, w1, w3, w2), pos


def _qkv_kernel(ncc, n_steps, pos_ref, posn_ref, ys_ref, xm_ref, gate_ref, g_ref, sh_ref, sc_ref,
                w_ref, qn_ref, kn_ref, cos_ref, sin_ref, x1_ref, q_ref, k_ref, v_ref, buf, sem):
    c = pl.program_id(1)
    step = pl.program_id(0) * pl.num_programs(1) + c
    slot = step % 2
    latent = c >= ncc
    nb = xm_ref.shape[0]
    nq = ATT_HEADS * HEAD_DIM
    nk = KV_HEADS * HEAD_DIM
    cos = cos_ref[...]
    sin = sin_ref[...]
    hbs = [None] * nb

    def row_loops(idx_ref, s, r):
        return _row_dma_loops(
            TOK,
            lambda grp, sub, p: pltpu.make_async_copy(ys_ref.at[pl.ds(p, 1)], buf.at[s, r, grp, pl.ds(sub, 1)],
                                                      sem.at[s]),
            lambda j: idx_ref[r, 0, 0, j])

    @pl.when(step == 0)
    def _():
        for r in range(nb):
            row_loops(pos_ref, slot, r)[0]()

    @pl.when(step + 1 < n_steps)
    def _():
        for r in range(nb):
            row_loops(posn_ref, 1 - slot, r)[0]()

    for r in range(nb):
        row_loops(pos_ref, slot, r)[1]()

    def head(z, gain):
        z = _rms(z, gain)
        rot = z * cos + pltpu.roll(z, HEAD_DIM // 2, 1) * sin
        return jnp.where(latent, rot, z)

    def kv_row(r):
        y = _unpack_bf16_pairs(buf[slot, r].reshape(TOK, buf.shape[-1])).astype(F32)
        x1 = xm_ref[r] + gate_ref[0, r] * y
        x1_ref[r] = x1
        h = _rms(x1, g_ref[...]) * (1.0 + sc_ref[0, r]) + sh_ref[0, r]
        hbs[r] = h.astype(BF16)
        kv = jnp.dot(hbs[r], w_ref[:, nq:], preferred_element_type=F32)
        yield
        k_ref[r] = jnp.concatenate(
            [head(kv[:, i * HEAD_DIM:(i + 1) * HEAD_DIM], kn_ref[...]) for i in range(KV_HEADS)],
            axis=1).astype(BF16)
        ones = jnp.ones((TOK, HEAD_DIM), BF16)
        v_ref[r] = jnp.concatenate(
            [blk for i in range(KV_HEADS)
             for blk in (kv[:, nk + i * HEAD_DIM:nk + (i + 1) * HEAD_DIM].astype(BF16), ones)], axis=1)

    def q_row(r):
        qq = jnp.dot(hbs[r], w_ref[:, :nq], preferred_element_type=F32)
        yield
        q_ref[r] = jnp.concatenate(
            [head(qq[:, i * HEAD_DIM:(i + 1) * HEAD_DIM], qn_ref[...]) * (HEAD_DIM ** -0.5)
             for i in range(ATT_HEADS)], axis=1).astype(BF16)

    _lockstep([kv_row(r) for r in range(nb)])

    @pl.when(latent)
    def _():
        _lockstep([q_row(r) for r in range(nb)])


def _qkv(ncc, ys, pos, xmid, gain, mods, w_qkv, qn, kn, cos, sin):
    batch, t, d = xmid.shape
    nch = t // TOK
    seq = t - ncc * TOK
    nq = ATT_HEADS * HEAD_DIM
    nk = KV_HEADS * HEAD_DIM
    const = lambda shape: pl.BlockSpec(shape, lambda b, c: (0,) * len(shape))
    lat = lambda b, c: (b, jnp.maximum(c - ncc, 0), 0)
    nb = QKV_ROWS
    nbb = batch // nb
    n_steps = nbb * nch

    def next_step(b, c):
        wrap = c + 1 == nch
        return jnp.where(wrap, jnp.minimum(b + 1, nbb - 1), b), jnp.where(wrap, 0, c + 1), 0, 0

    pos4 = pos.reshape(batch, nch, 1, TOK)
    return pl.pallas_call(
        functools.partial(_qkv_kernel, ncc, n_steps),
        grid=(nbb, nch),
        in_specs=[pl.BlockSpec((nb, 1, 1, TOK), lambda b, c: (b, c, 0, 0), memory_space=pltpu.SMEM),
                  pl.BlockSpec((nb, 1, 1, TOK), next_step, memory_space=pltpu.SMEM),
                  pl.BlockSpec(memory_space=pl.ANY),
                  pl.BlockSpec((nb, TOK, d), lambda b, c: (b, c, 0)),
                  mods.spec(0, 5, nb=nb),
                  const((1, d)), mods.spec(1, 0, nb=nb), mods.spec(1, 1, nb=nb),
                  const((d, nq + 2 * nk)), const((1, HEAD_DIM)), const((1, HEAD_DIM)),
                  pl.BlockSpec((TOK, HEAD_DIM), lambda b, c: (jnp.maximum(c - ncc, 0), 0)),
                  pl.BlockSpec((TOK, HEAD_DIM), lambda b, c: (jnp.maximum(c - ncc, 0), 0))],
        out_specs=[pl.BlockSpec((nb, TOK, d), lambda b, c: (b, c, 0)),
                   pl.BlockSpec((nb, TOK, nq), lat),
                   pl.BlockSpec((nb, TOK, nk), lambda b, c: (b, c, 0)),
                   pl.BlockSpec((nb, TOK, 2 * nk), lambda b, c: (b, c, 0))],
        out_shape=[jax.ShapeDtypeStruct((batch, t, d), F32),
                   jax.ShapeDtypeStruct((batch, seq, nq), BF16),
                   jax.ShapeDtypeStruct((batch, t, nk), BF16),
                   jax.ShapeDtypeStruct((batch, t, 2 * nk), BF16)],
        scratch_shapes=[pltpu.VMEM((2, nb, TOK // SUBLANES, SUBLANES, d // 2), U32), pltpu.SemaphoreType.DMA((2,))],
        compiler_params=_cparams(("arbitrary", "arbitrary")),
        name="qkv",
    )(pos4, pos4, ys, xmid, mods.table, gain, mods.table, mods.table, w_qkv, qn, kn, cos, sin)


def _attn_kernel(n_kb, q_ref, k_ref, v_ref, o_ref):
    q = q_ref[0]
    qs = jnp.concatenate([q[:, g * HEAD_DIM:(g + 1) * HEAD_DIM] for g in range(Q_PER_KV)], axis=0)
    kb = k_ref.shape[1] // n_kb
    m = acc = None
    for j in range(n_kb):
        s = lax.dot_general(qs, k_ref[0, j * kb:(j + 1) * kb, :], (((1,), (1,)), ((), ())),
                            preferred_element_type=F32)
        m_blk = jnp.max(s, axis=-1, keepdims=True)
        m_new = m_blk if m is None else jnp.maximum(m, m_blk)
        p = jnp.exp((s - m_new).astype(BF16))
        pv = jnp.dot(p, v_ref[0, j * kb:(j + 1) * kb, :], preferred_element_type=F32)
        acc = pv if acc is None else jnp.exp(m - m_new) * acc + pv
        m = m_new
    o = acc[:, :HEAD_DIM] / acc[:, HEAD_DIM:HEAD_DIM + 1]
    o_ref[0] = jnp.concatenate([o[g * ATT_Q:(g + 1) * ATT_Q] for g in range(Q_PER_KV)], axis=1).astype(BF16)


def _attention(q, k, v):
    batch, seq, nq = q.shape
    t = k.shape[1]
    gw = Q_PER_KV * HEAD_DIM
    n_kb = 2 if t % (2 * LANES) == 0 else 1
    return pl.pallas_call(
        functools.partial(_attn_kernel, n_kb),
        grid=(batch, KV_HEADS, seq // ATT_Q),
        in_specs=[pl.BlockSpec((1, ATT_Q, gw), lambda b, h, i: (b, i, h)),
                  pl.BlockSpec((1, t, HEAD_DIM), lambda b, h, i: (b, 0, h)),
                  pl.BlockSpec((1, t, 2 * HEAD_DIM), lambda b, h, i: (b, 0, h))],
        out_specs=pl.BlockSpec((1, ATT_Q, gw), lambda b, h, i: (b, i, h)),
        out_shape=jax.ShapeDtypeStruct((batch, seq, nq), BF16),
        compiler_params=_cparams(("arbitrary", "arbitrary", "arbitrary")),
        name="attention",
    )(q, k, v)


def _block_diag(w):
    nb, bs, _ = w.shape
    eye = jnp.eye(nb, dtype=w.dtype)
    return (eye[:, None, :, None] * w[:, :, None, :]).reshape(nb * bs, nb * bs)


def _router_weights(wg, bg, we, be):
    d = wg.shape[0]
    n = N_GROUPS + N_EXPERTS
    wr = jnp.zeros((ROUTER_ROWS, d), F32).at[:N_GROUPS].set(wg.T).at[N_GROUPS:n].set(we.T)
    br = jnp.zeros((ROUTER_ROWS, 1), F32).at[:N_GROUPS, 0].set(bg).at[N_GROUPS:n, 0].set(be)
    w_hi = wr.astype(BF16)
    w_lo = (wr - w_hi.astype(F32)).astype(BF16)
    return jnp.concatenate([w_hi, w_lo], axis=0), br


def _rope_tables(seq):
    rows = seq // GRID_W
    row = np.repeat(np.arange(rows, dtype=np.float32), GRID_W)
    col = np.tile(np.arange(GRID_W, dtype=np.float32), rows)
    ppa = HEAD_DIM // 4
    freqs = (ROPE_THETA ** (-np.arange(ppa, dtype=np.float32) / ppa)).astype(np.float32)
    ang = np.concatenate([row[:, None] * freqs, col[:, None] * freqs], axis=-1)
    cos, sin = np.cos(ang), np.sin(ang)
    return (jnp.asarray(np.concatenate([cos, cos], axis=-1), F32),
            jnp.asarray(np.concatenate([-sin, sin], axis=-1), F32))


_HALF_SPLIT = np.concatenate([np.arange(0, HEAD_DIM, 2), np.arange(1, HEAD_DIM, 2)])


def kernel(x, c, ctx, c_ctx, norm1, norm2, w_ada, b_ada, ev_w_in, ev_conv_w, ev_conv_b, ev_lru_wa, ev_lru_ba, ev_lru_wi, ev_lru_bi, ev_lru_lam, ev_gla_wg, ev_gla_bg, ev_gla_norm, ev_w_out, od_w_qkv, od_q_norm, od_k_norm, od_w_o, moe_wg, moe_bg, moe_we, moe_be, moe_w1, moe_w3, moe_w2):
    batch, seq, d = x.shape
    ctx_len = ctx.shape[1]
    assert seq % TOK == 0 and ctx_len % TOK == 0 and d % GLA_VAL == 0 and seq % GRID_W == 0
    ncc = ctx_len // TOK
    lw = d // 2

    assert batch % TOK_ROWS == 0
    rows = -(-(batch + TOK_ROWS) // SUBLANES) * SUBLANES
    cv = jnp.zeros((rows, d), F32).at[:batch].set(c).at[batch:batch + TOK_ROWS].set(c_ctx)
    table = _adaln(cv, w_ada, b_ada)
    depth = table.shape[0]
    table = table.reshape(depth, rows, 6, d).transpose(0, 2, 1, 3).reshape(depth * 6, rows, 1, d)
    mods = _Mods(table, batch, ncc)

    w_in = ev_w_in[0]
    nm = 2 * lw + 2 * GLA_KEY + 2 * GLA_VAL
    w_main = w_in[:, :nm].astype(BF16)
    w_lr = jnp.zeros((d, LANES), F32).at[:, :2 * GLA_RANK].set(w_in[:, nm:]).astype(BF16)
    pm, lr = _inproj(x, ctx, norm1[0][None], mods, w_main, w_lr, ev_conv_w[0], ev_conv_b[0][None], lw)

    mixed = []
    for dr in range(2):
        w_gate = jnp.concatenate([_block_diag(ev_lru_wa[0, dr]), _block_diag(ev_lru_wi[0, dr])], axis=1).astype(BF16)
        b_gate = jnp.concatenate([ev_lru_ba[0, dr], ev_lru_bi[0, dr]])[None]
        wg_pad = jnp.zeros((LANES, GLA_KEY), F32).at[dr * GLA_RANK:(dr + 1) * GLA_RANK].set(ev_gla_wg[0, dr]).astype(BF16)
        mixed.append(_seqmix(dr == 1, ncc, pm, lr, w_gate, b_gate,
                             ev_lru_lam[0, dr][None], wg_pad, ev_gla_bg[0, dr][None], lw))
    (lru_f, gla_f), (lru_r, gla_r) = mixed

    wr0, br0 = _router_weights(moe_wg[0], moe_bg[0], moe_we[0], moe_be[0])
    xmid0, h2e0, cr0, cnt0 = _post0(ncc, lru_f, lru_r, gla_f, gla_r, pm, x, ctx, mods, norm2[0][None],
                                    ev_gla_norm[0][None], ev_w_out[0].astype(BF16), wr0, br0, lw)
    ys0, pos0 = _moe_sorted(h2e0, cr0, cnt0, 0, moe_w1, moe_w3, moe_w2)

    nq = ATT_HEADS * HEAD_DIM
    nk = KV_HEADS * HEAD_DIM
    perm = np.concatenate([h * HEAD_DIM + _HALF_SPLIT for h in range(ATT_HEADS + KV_HEADS)]
                          + [np.arange(nq + nk, nq + 2 * nk)])
    w_qkv = od_w_qkv[0][:, perm].astype(BF16)
    cos, sin = _rope_tables(seq)
    x1, q, k, v = _qkv(ncc, ys0, pos0, xmid0, norm1[1][None], mods, w_qkv, od_q_norm[0][_HALF_SPLIT][None],
                       od_k_norm[0][_HALF_SPLIT][None], cos, sin)
    att = _attention(q, k, v)
    wr1, br1 = _router_weights(moe_wg[1], moe_bg[1], moe_we[1], moe_be[1])
    xmid1, h2e1, cr1, cnt1 = _post1(ncc, att, x1, mods, norm2[1][None], od_w_o[0].astype(BF16), wr1, br1)
    ys1, pos1 = _moe_sorted(h2e1, cr1, cnt1, 1, moe_w1, moe_w3, moe_w2)
    return _gather_residual(ys1, pos1, xmid1, mods.spec(1, 5, True), mods.table)
```

```python
import functools
import itertools

import numpy as np
import jax
import jax.numpy as jnp
from jax import lax
from jax.experimental import pallas as pl
from jax.experimental.pallas import tpu as pltpu

F32 = jnp.float32
BF16 = jnp.bfloat16
I32 = jnp.int32
U32 = jnp.uint32
HIGHEST = lax.Precision.HIGHEST

EPS = 1e-6
GRID_W = 64
LOG2_E = float(np.log2(np.e))
LRU_C = 8.0
GLA_HEADS = 4
GLA_DK = 64
GLA_DV = 128
GLA_KEY = GLA_HEADS * GLA_DK
GLA_VAL = GLA_HEADS * GLA_DV
GLA_RANK = 16
GLA_TAU = 16.0
GLA_CHUNK = 64
_DK_SHIFT = GLA_DK.bit_length() - 1
ATT_HEADS = 8
KV_HEADS = 2
Q_PER_KV = ATT_HEADS // KV_HEADS
HEAD_DIM = 128
ROPE_THETA = 10000.0
N_GROUPS = 4
EXPERTS_PER_GROUP = 4
N_EXPERTS = N_GROUPS * EXPERTS_PER_GROUP
PAIRS_PER_GROUP = 6
N_CLASSES = N_GROUPS * PAIRS_PER_GROUP

LANES = 128
SUBLANES = 8
TOK = 256
MOE_TILE = 512
ATT_Q = 256
PERM_ROWS = 1024
TOK_ROWS = 4
QKV_ROWS = 2
SEQ_ROWS = 4
WEXT = LANES
ROUTER_ROWS = 32
VMEM_LIMIT = 56 * 1024 * 1024


def _cparams(sem):
    return pltpu.CompilerParams(dimension_semantics=sem, vmem_limit_bytes=VMEM_LIMIT)


def _rms(x, g):
    return x * lax.rsqrt(jnp.mean(x * x, axis=-1, keepdims=True) + EPS) * g


def _sigmoid(x):
    return 1.0 / (1.0 + jnp.exp(-x))


def _silu(x):
    return x * _sigmoid(x)


def _gelu_tanh(x):
    return 0.5 * x * (1.0 + jnp.tanh(np.sqrt(2.0 / np.pi).astype(np.float32) * (x + 0.044715 * (x * x * x))))


def _pack_bf16_pairs(xb):
    k = xb.shape[-1] // 2
    lo = lax.shift_right_logical(pltpu.bitcast(xb[:, :k].astype(F32), U32), jnp.uint32(16))
    hi = pltpu.bitcast(xb[:, k:].astype(F32), U32) & jnp.uint32(0xFFFF0000)
    return hi | lo


def _unpack_bf16_pairs(words):
    lo = pltpu.bitcast(lax.shift_left(words, jnp.uint32(16)), F32).astype(BF16)
    hi = pltpu.bitcast(words & jnp.uint32(0xFFFF0000), F32).astype(BF16)
    return jnp.concatenate([lo, hi], axis=1)


def _softplus(x):
    return jnp.maximum(x, 0.0) + jnp.log(1.0 + jnp.exp(-jnp.abs(x)))


def _log_sigmoid(x):
    return -_softplus(-x)


def _adaln_kernel(cv_ref, w_ref, b_ref, o_ref):
    s = _silu(cv_ref[...])
    o_ref[0] = jnp.dot(s, w_ref[0], precision=HIGHEST, preferred_element_type=F32) + b_ref[0]


def _adaln(cv, w_ada, b_ada):
    depth, d, n6 = w_ada.shape
    rows = cv.shape[0]
    tn = 6 * d // 4
    return pl.pallas_call(
        _adaln_kernel,
        grid=(depth, n6 // tn),
        in_specs=[pl.BlockSpec((rows, d), lambda l, j: (0, 0)),
                  pl.BlockSpec((1, d, tn), lambda l, j: (l, 0, j)),
                  pl.BlockSpec((1, 1, tn), lambda l, j: (l, 0, j))],
        out_specs=pl.BlockSpec((1, rows, tn), lambda l, j: (l, 0, j)),
        out_shape=jax.ShapeDtypeStruct((depth, rows, n6), F32),
        compiler_params=_cparams(("arbitrary", "arbitrary")),
        name="adaln",
    )(cv, w_ada, b_ada.reshape(depth, 1, n6))


class _Mods:
    def __init__(self, table, batch, ncc):
        self.table, self.batch, self.ncc = table, batch, ncc
        self.d = table.shape[-1]

    def spec(self, layer, j, latent_only=False, nb=1):
        batch, ncc = self.batch, self.ncc

        def imap(b, c):
            r = b if latent_only else jnp.where(c < ncc, batch // nb, b)
            return (layer * 6 + j, r, 0, 0)

        return pl.BlockSpec((1, nb, 1, self.d), imap)


def _lockstep(generators):
    for _ in itertools.zip_longest(*generators):
        pass


def _inproj_kernel(ncc, nch, lw, x_ref, c_ref, xp_ref, xn_ref, cp_ref, cn_ref, g_ref, sh_ref, sc_ref,
                   w_ref, wlr_ref, cw_ref, cb_ref, pm_ref, lr_ref):
    _lockstep([_inproj_row(ncc, nch, lw, i, x_ref, c_ref, xp_ref, xn_ref, cp_ref, cn_ref, g_ref, sh_ref, sc_ref,
                           w_ref, wlr_ref, cw_ref, cb_ref, pm_ref, lr_ref) for i in range(x_ref.shape[0])])


def _inproj_row(ncc, nch, lw, i, x_ref, c_ref, xp_ref, xn_ref, cp_ref, cn_ref, g_ref, sh_ref, sc_ref,
                w_ref, wlr_ref, cw_ref, cb_ref, pm_ref, lr_ref):
    c = pl.program_id(1)
    is_ctx = c < ncc

    def modulated(rows):
        return (_rms(rows, g_ref[...]) * (1.0 + sc_ref[0, i]) + sh_ref[0, i]).astype(BF16)

    hb = modulated(jnp.where(is_ctx, c_ref[i], x_ref[i]))
    pm = jnp.dot(hb, w_ref[...], preferred_element_type=F32)
    yield
    lr_ref[i] = jnp.dot(hb, wlr_ref[...], preferred_element_type=F32).astype(BF16)

    halo = jnp.concatenate([jnp.where(is_ctx, cp_ref[i], xp_ref[i]), jnp.where(is_ctx, cn_ref[i], xn_ref[i])], axis=0)
    xa_halo = jnp.dot(modulated(halo), w_ref[:, :lw], preferred_element_type=F32)
    yield
    has_prev = jnp.logical_and(c != 0, c != ncc)
    has_next = jnp.logical_and(c != ncc - 1, c != nch - 1)
    p2 = jnp.where(has_prev, xa_halo[SUBLANES - 2:SUBLANES - 1], 0.0)
    p1 = jnp.where(has_prev, xa_halo[SUBLANES - 1:SUBLANES], 0.0)
    n0 = jnp.where(has_next, xa_halo[SUBLANES:SUBLANES + 1], 0.0)
    xa = pm[:, :lw]
    row = lax.broadcasted_iota(I32, xa.shape, 0)
    x_m1 = jnp.where(row == 0, p1, pltpu.roll(xa, 1, 0))
    x_m2 = jnp.where(row == 0, p2, jnp.where(row == 1, p1, pltpu.roll(xa, 2, 0)))
    x_p1 = jnp.where(row == TOK - 1, n0, pltpu.roll(xa, TOK - 1, 0))
    cw = cw_ref[...]
    u = cw[0:1] * x_m2 + cw[1:2] * x_m1 + cw[2:3] * xa + cw[3:4] * x_p1 + cb_ref[...]
    pm_ref[i, :, :lw] = u.astype(BF16)
    pm_ref[i, :, lw:] = pm[:, lw:].astype(BF16)


def _inproj(x, ctx, gain, mods, w_main, w_lr, conv_w, conv_b, lw):
    batch, seq, d = x.shape
    ncc = ctx.shape[1] // TOK
    nl = seq // TOK
    nch = ncc + nl
    nm = w_main.shape[1]
    hb = TOK // SUBLANES
    nb = TOK_ROWS
    return pl.pallas_call(
        functools.partial(_inproj_kernel, ncc, nch, lw),
        grid=(batch // nb, nch),
        in_specs=[pl.BlockSpec((nb, TOK, d), lambda b, c: (b, jnp.maximum(c - ncc, 0), 0)),
                  pl.BlockSpec((nb, TOK, d), lambda b, c: (b, jnp.minimum(c, ncc - 1), 0)),
                  pl.BlockSpec((nb, SUBLANES, d), lambda b, c: (b, jnp.maximum((c - ncc) * hb - 1, 0), 0)),
                  pl.BlockSpec((nb, SUBLANES, d),
                               lambda b, c: (b, jnp.clip((c - ncc + 1) * hb, 0, nl * hb - 1), 0)),
                  pl.BlockSpec((nb, SUBLANES, d), lambda b, c: (b, jnp.clip(c * hb - 1, 0, ncc * hb - 1), 0)),
                  pl.BlockSpec((nb, SUBLANES, d), lambda b, c: (b, jnp.clip((c + 1) * hb, 0, ncc * hb - 1), 0)),
                  pl.BlockSpec((1, d), lambda b, c: (0, 0)),
                  mods.spec(0, 0, nb=nb), mods.spec(0, 1, nb=nb),
                  pl.BlockSpec((d, nm), lambda b, c: (0, 0)),
                  pl.BlockSpec((d, LANES), lambda b, c: (0, 0)),
                  pl.BlockSpec((4, lw), lambda b, c: (0, 0)),
                  pl.BlockSpec((1, lw), lambda b, c: (0, 0))],
        out_specs=[pl.BlockSpec((nb, TOK, nm), lambda b, c: (b, c, 0)),
                   pl.BlockSpec((nb, TOK, LANES), lambda b, c: (b, c, 0))],
        out_shape=[jax.ShapeDtypeStruct((batch, nch * TOK, nm), BF16),
                   jax.ShapeDtypeStruct((batch, nch * TOK, LANES), BF16)],
        compiler_params=_cparams(("arbitrary", "arbitrary")),
        name="inproj",
    )(x, ctx, x, x, ctx, ctx, gain, mods.table, mods.table, w_main, w_lr, conv_w, conv_b)


def _lru_scan(a, b, h0, rev):
    n_groups = TOK // SUBLANES
    a = a.reshape(n_groups, SUBLANES, a.shape[-1])
    b = b.reshape(a.shape)
    sub = lax.broadcasted_iota(I32, a.shape, 1)
    for dist in (1, 2, 4):
        shift = SUBLANES - dist if rev else dist
        a_s = pltpu.roll(a, shift, 1)
        b_s = pltpu.roll(b, shift, 1)
        m = (sub < SUBLANES - dist) if rev else (sub >= dist)
        b = jnp.where(m, a * b_s + b, b)
        a = jnp.where(m, a * a_s, a)
    order = range(n_groups - 1, -1, -1) if rev else range(n_groups)
    outs = [None] * n_groups
    h = h0
    for r in order:
        hr = a[r] * h + b[r]
        outs[r] = hr
        h = hr[0:1] if rev else hr[SUBLANES - 1:SUBLANES]
    return jnp.concatenate(outs, axis=0), h


def _seqmix_kernel(rev, *refs):
    h_scr, s_scr = refs[-2:]

    @pl.when(pl.program_id(1) == 0)
    def _():
        h_scr[...] = jnp.zeros_like(h_scr)
        s_scr[...] = jnp.zeros_like(s_scr)

    _lockstep([_seqmix_row(rev, i, *refs) for i in range(refs[0].shape[0])])


def _seqmix_row(rev, i, u_ref, q_ref, k_ref, v_ref, lr_ref,
                wgt_ref, bgt_ref, lam_ref, wg_ref, bg_ref, sel_ref,
                lru_ref, gla_ref, h_scr, s_scr):
    lw = u_ref.shape[-1]
    ub = u_ref[i]
    u = ub.astype(F32)
    gates = jnp.dot(ub, wgt_ref[...], preferred_element_type=F32) + bgt_ref[...]
    yield
    r_gate = _sigmoid(gates[:, :lw])
    i_gate = _sigmoid(gates[:, lw:])
    log2a_unit = (-LRU_C * LOG2_E) * _softplus(-lam_ref[...])
    a = jnp.exp2(r_gate * log2a_unit)
    bb = jnp.sqrt(1.0 - a * a) * (i_gate * u)
    hs, h_last = _lru_scan(a, bb, h_scr[i, 0:1], rev)
    h_scr[i, 0:1] = h_last
    lru_ref[i] = hs.astype(BF16)
    yield

    lg = _log_sigmoid(jnp.dot(lr_ref[i], wg_ref[...], preferred_element_type=F32) + bg_ref[...]) * (1.0 / GLA_TAU)
    lg_hi = lg.astype(BF16)
    lg_lo = (lg - lg_hi.astype(F32)).astype(BF16)
    yield
    cum = jnp.dot(sel_ref[...], jnp.concatenate([lg_hi, lg_lo], axis=1), preferred_element_type=F32)
    yield
    g = cum[:TOK, :GLA_KEY] + cum[:TOK, GLA_KEY:]
    tot = cum[TOK:, :GLA_KEY] + cum[TOK:, GLA_KEY:]
    q = q_ref[i].astype(F32)
    k = k_ref[i].astype(F32)
    dec_all = jnp.exp(tot)
    k_neg = k * jnp.exp(-g)
    qg = (q * jnp.exp(g) * (GLA_DK ** -0.5)).astype(BF16)
    kg = k_neg.astype(BF16)
    kd = (k_neg * dec_all).astype(BF16)
    v = v_ref[i]

    lane = lax.broadcasted_iota(I32, (1, GLA_KEY), 1) >> _DK_SHIFT
    head_masks = [(lane == h) for h in range(GLA_HEADS)]
    n_stack = GLA_HEADS * GLA_CHUNK
    ai = lax.broadcasted_iota(I32, (n_stack, GLA_CHUNK), 0) & (GLA_CHUNK - 1)
    aj = lax.broadcasted_iota(I32, (n_stack, GLA_CHUNK), 1)
    causal = (aj >= ai) if rev else (aj <= ai)

    n_sub = TOK // GLA_CHUNK
    order = range(n_sub - 1, -1, -1) if rev else range(n_sub)
    outs = [None] * n_sub
    st = s_scr[i]
    for n in order:
        sl = slice(n * GLA_CHUNK, (n + 1) * GLA_CHUNK)
        qn = qg[sl]
        qs = jnp.concatenate([jnp.where(head_masks[h], qn, jnp.zeros_like(qn)) for h in range(GLA_HEADS)], axis=0)
        att = lax.dot_general(qs, kg[sl], (((1,), (1,)), ((), ())), preferred_element_type=F32)
        yield
        att = jnp.where(causal, att, 0.0).astype(BF16)
        o_all = jnp.dot(att, v[sl], preferred_element_type=F32)
        yield
        o_int = lax.dot_general(qs, st.astype(BF16), (((1,), (1,)), ((), ())),
                                preferred_element_type=F32)
        yield
        outs[n] = jnp.concatenate(
            [o_all[h * GLA_CHUNK:(h + 1) * GLA_CHUNK, h * GLA_DV:(h + 1) * GLA_DV]
             + o_int[h * GLA_CHUNK:(h + 1) * GLA_CHUNK] for h in range(GLA_HEADS)], axis=1)
        vn, kdn = v[sl], kd[sl]
        v_stack = jnp.concatenate([vn[:, h * GLA_DV:(h + 1) * GLA_DV] for h in range(GLA_HEADS)], axis=0)
        kd_stack = jnp.concatenate([jnp.where(head_masks[h], kdn, jnp.zeros_like(kdn)) for h in range(GLA_HEADS)],
                                   axis=0)
        ds = lax.dot_general(v_stack, kd_stack, (((0,), (0,)), ((), ())), preferred_element_type=F32)
        yield
        st = dec_all[n * GLA_CHUNK:n * GLA_CHUNK + 1] * st + ds
    s_scr[i] = st
    gla_ref[i] = jnp.concatenate(outs, axis=0).astype(BF16)


def _cumsum_selector(rev):
    r = np.arange(TOK)[:, None]
    c = np.arange(TOK)[None, :]
    same = (r // GLA_CHUNK) == (c // GLA_CHUNK)
    tri = same & ((c >= r) if rev else (c <= r))
    return jnp.asarray(np.concatenate([tri, same], axis=0), BF16)


def _seqmix(rev, ncc, pm, lr, w_gate, b_gate, lam, wg_pad, bg, lw):
    batch, t, _ = pm.shape
    nch = t // TOK
    nl = nch - ncc

    def chunk(s):
        if rev:
            return jnp.where(s < ncc, ncc - 1 - s, ncc + (nl - 1) - (s - ncc))
        return s

    qcol = 2 * lw // GLA_KEY
    vcol = (2 * lw + 2 * GLA_KEY) // GLA_VAL
    const = lambda shape: pl.BlockSpec(shape, lambda b, s: (0,) * len(shape))
    nb = SEQ_ROWS if batch % SEQ_ROWS == 0 else 1
    return pl.pallas_call(
        functools.partial(_seqmix_kernel, rev),
        grid=(batch // nb, nch),
        in_specs=[pl.BlockSpec((nb, TOK, lw), lambda b, s: (b, chunk(s), 0)),
                  pl.BlockSpec((nb, TOK, GLA_KEY), lambda b, s: (b, chunk(s), qcol)),
                  pl.BlockSpec((nb, TOK, GLA_KEY), lambda b, s: (b, chunk(s), qcol + 1)),
                  pl.BlockSpec((nb, TOK, GLA_VAL), lambda b, s: (b, chunk(s), vcol)),
                  pl.BlockSpec((nb, TOK, LANES), lambda b, s: (b, chunk(s), 0)),
                  const((lw, 2 * lw)), const((1, 2 * lw)), const((1, lw)),
                  const((LANES, GLA_KEY)), const((1, GLA_KEY)), const((2 * TOK, TOK))],
        out_specs=[pl.BlockSpec((nb, TOK, lw), lambda b, s: (b, chunk(s), 0)),
                   pl.BlockSpec((nb, TOK, GLA_VAL), lambda b, s: (b, chunk(s), 0))],
        out_shape=[jax.ShapeDtypeStruct((batch, t, lw), BF16),
                   jax.ShapeDtypeStruct((batch, t, GLA_VAL), BF16)],
        scratch_shapes=[pltpu.VMEM((nb, SUBLANES, lw), F32), pltpu.VMEM((nb, GLA_DV, GLA_KEY), F32)],
        compiler_params=_cparams(("arbitrary", "arbitrary")),
        name="seqmix_rev" if rev else "seqmix_fwd",
    )(pm, pm, pm, pm, lr, w_gate, b_gate, lam, wg_pad, bg, _cumsum_selector(rev))


def _route_epilogue(i, first, xm, n2_ref, sh2_ref, sc2_ref, wr_ref, br_ref,
                    xmid_ref, h2_ref, cr_ref, cnt_ref, carry):
    d = xm.shape[-1]
    xmid_ref[i] = xm
    h2 = _rms(xm, n2_ref[...]) * (1.0 + sc2_ref[0, i]) + sh2_ref[0, i]
    h_hi = h2.astype(BF16)
    h_lo = (h2 - h_hi.astype(F32)).astype(BF16)
    nt = (((1,), (1,)), ((), ()))
    o1 = lax.dot_general(wr_ref[...], h_hi, nt, preferred_element_type=F32)
    o2 = lax.dot_general(wr_ref[:ROUTER_ROWS], h_lo, nt, preferred_element_type=F32)
    yield
    logits = o1[:ROUTER_ROWS] + o1[ROUTER_ROWS:] + o2 + br_ref[...]

    def col(j):
        return logits[j:j + 1]

    lgs = [col(g) for g in range(N_GROUPS)]
    gmax = functools.reduce(jnp.maximum, lgs)
    gi = jnp.where(lgs[0] == gmax, 0, jnp.where(lgs[1] == gmax, 1, jnp.where(lgs[2] == gmax, 2, 3)))
    w_group = 1.0 / functools.reduce(lambda p, q: p + q, [jnp.exp(l - gmax) for l in lgs])
    es = []
    for j in range(EXPERTS_PER_GROUP):
        acc = jnp.zeros_like(gmax)
        for g in range(N_GROUPS):
            acc = acc + jnp.where(gi == g, col(N_GROUPS + g * EXPERTS_PER_GROUP + j), 0.0)
        es.append(acc)
    m1 = functools.reduce(jnp.maximum, es)
    i1 = jnp.where(es[0] == m1, 0, jnp.where(es[1] == m1, 1, jnp.where(es[2] == m1, 2, 3)))
    rest = [jnp.where(i1 == j, -jnp.inf, es[j]) for j in range(EXPERTS_PER_GROUP)]
    m2 = functools.reduce(jnp.maximum, rest)
    i2 = jnp.where(rest[0] == m2, 0, jnp.where(rest[1] == m2, 1, jnp.where(rest[2] == m2, 2, 3)))
    e2 = jnp.exp(m2 - m1)
    w1 = w_group / (1.0 + e2)
    w2 = w_group * e2 / (1.0 + e2)
    first_lo = i1 < i2
    lo = jnp.where(first_lo, i1, i2)
    hi = jnp.where(first_lo, i2, i1)
    w_lo = jnp.where(first_lo, w1, w2)
    w_hi = jnp.where(first_lo, w2, w1)
    pidx = jnp.where(lo == 0, hi - 1, jnp.where(lo == 1, hi + 1, 5))
    cls = gi * PAIRS_PER_GROUP + pidx

    def terms(w):
        t1 = w.astype(BF16).astype(F32)
        t2 = (w - t1).astype(BF16).astype(F32)
        t3 = (w - t1 - t2).astype(BF16).astype(F32)
        return [t1, t2, t3]

    sub = lax.broadcasted_iota(I32, (2 * SUBLANES, TOK), 0)
    stacked = jnp.zeros((2 * SUBLANES, TOK), F32)
    for r, term in enumerate(terms(w_lo) + terms(w_hi)):
        stacked = jnp.where(sub == r, term, stacked)
    er = lax.broadcasted_iota(I32, (2 * SUBLANES, LANES), 0)
    el = lax.broadcasted_iota(I32, (2 * SUBLANES, LANES), 1)
    place = jnp.logical_or(jnp.logical_and(er < 3, el == 0),
                           jnp.logical_and(jnp.logical_and(er >= 3, er < 6), el == 1))
    w_cols = lax.dot_general(stacked.astype(BF16), place.astype(BF16), (((0,), (0,)), ((), ())),
                             preferred_element_type=F32)
    h2_ref[i, :, :d // 2] = _pack_bf16_pairs(h_hi)
    h2_ref[i, :, d // 2:] = pltpu.bitcast(w_cols, U32)

    krow = lax.broadcasted_iota(I32, (ROUTER_ROWS, TOK), 0)
    onehot = krow == cls
    ri = lax.broadcasted_iota(I32, (TOK, TOK), 0)
    ci = lax.broadcasted_iota(I32, (TOK, TOK), 1)
    triu = (ri <= ci).astype(BF16)
    pref = jnp.dot(onehot.astype(BF16), triu, preferred_element_type=F32)
    yield

    if i == 0:
        @pl.when(first)
        def _():
            carry[...] = jnp.zeros_like(carry)

    base = carry[:, 0:1]
    rank = jnp.sum(jnp.where(onehot, pref - 1.0 + base, 0.0), axis=0, keepdims=True)
    new_carry = jnp.broadcast_to(base + jnp.sum(onehot.astype(F32), axis=1, keepdims=True), carry.shape)
    carry[...] = new_carry
    cnt_ref[...] = new_carry
    sub8 = lax.broadcasted_iota(I32, (SUBLANES, TOK), 0)
    cr_ref[i, 0] = jnp.where(sub8 == 0, cls, jnp.where(sub8 == 1, rank.astype(I32), 0))


def _post0_kernel(ncc, *refs):
    _lockstep([_post0_row(ncc, i, *refs) for i in range(refs[0].shape[0])])


def _post0_row(ncc, i, l0_ref, l1_ref, g0_ref, g1_ref, ga_ref, r_ref, x_ref, c_ref,
               gate_ref, sh2_ref, sc2_ref, n2_ref, gn_ref, wout_ref, wr_ref, br_ref,
               xmid_ref, h2_ref, cr_ref, cnt_ref, carry):
    b, c = pl.program_id(0), pl.program_id(1)
    lru = l0_ref[i].astype(F32) + l1_ref[i].astype(F32)
    ya = lru * _gelu_tanh(ga_ref[i].astype(F32))
    gla = g0_ref[i].astype(F32) + g1_ref[i].astype(F32)
    parts = []
    for h in range(GLA_HEADS):
        parts.append(_rms(gla[:, h * GLA_DV:(h + 1) * GLA_DV], gn_ref[...]))
    yb = jnp.concatenate(parts, axis=1) * _silu(r_ref[i].astype(F32))
    ycat = jnp.concatenate([ya, yb], axis=1).astype(BF16)
    y = jnp.dot(ycat, wout_ref[...], preferred_element_type=F32)
    yield
    x0 = jnp.where(c < ncc, c_ref[i], x_ref[i])
    xm = x0 + gate_ref[0, i] * y
    yield from _route_epilogue(i, jnp.logical_and(b == 0, c == 0), xm, n2_ref, sh2_ref, sc2_ref, wr_ref, br_ref,
                               xmid_ref, h2_ref, cr_ref, cnt_ref, carry)


def _post1_kernel(*refs):
    _lockstep([_post1_row(i, *refs) for i in range(refs[0].shape[0])])


def _post1_row(i, o_ref, x_ref, gate_ref, sh2_ref, sc2_ref, n2_ref, wo_ref, wr_ref, br_ref,
               xmid_ref, h2_ref, cr_ref, cnt_ref, carry):
    b, c = pl.program_id(0), pl.program_id(1)
    y = jnp.dot(o_ref[i], wo_ref[...], preferred_element_type=F32)
    yield
    xm = x_ref[i] + gate_ref[0, i] * y
    yield from _route_epilogue(i, jnp.logical_and(b == 0, c == 0), xm, n2_ref, sh2_ref, sc2_ref, wr_ref, br_ref,
                               xmid_ref, h2_ref, cr_ref, cnt_ref, carry)


def _route_out(batch, nch, d, nb):
    specs = [pl.BlockSpec((nb, TOK, d), lambda b, c: (b, c, 0)),
             pl.BlockSpec((nb, TOK, d // 2 + WEXT), lambda b, c: (b, c, 0)),
             pl.BlockSpec((nb, 1, SUBLANES, TOK), lambda b, c: (b, c, 0, 0)),
             pl.BlockSpec((ROUTER_ROWS, LANES), lambda b, c: (0, 0))]
    shapes = [jax.ShapeDtypeStruct((batch, nch * TOK, d), F32),
              jax.ShapeDtypeStruct((batch, nch * TOK, d // 2 + WEXT), U32),
              jax.ShapeDtypeStruct((batch, nch, SUBLANES, TOK), I32),
              jax.ShapeDtypeStruct((ROUTER_ROWS, LANES), F32)]
    return specs, shapes


def _post0(ncc, lru_f, lru_r, gla_f, gla_r, pm, x, ctx, mods, n2, gn, w_out, wr, br, lw):
    batch, t, _ = pm.shape
    d = x.shape[-1]
    nch = t // TOK
    rcol = (2 * lw + 2 * GLA_KEY + GLA_VAL) // GLA_VAL
    nb = TOK_ROWS
    tokspec = lambda w: pl.BlockSpec((nb, TOK, w), lambda b, c: (b, c, 0))
    const = lambda shape: pl.BlockSpec(shape, lambda b, c: (0,) * len(shape))
    out_specs, out_shape = _route_out(batch, nch, d, nb)
    return pl.pallas_call(
        functools.partial(_post0_kernel, ncc),
        grid=(batch // nb, nch),
        in_specs=[tokspec(lw), tokspec(lw), tokspec(GLA_VAL), tokspec(GLA_VAL),
                  pl.BlockSpec((nb, TOK, lw), lambda b, c: (b, c, 1)),
                  pl.BlockSpec((nb, TOK, GLA_VAL), lambda b, c: (b, c, rcol)),
                  pl.BlockSpec((nb, TOK, d), lambda b, c: (b, jnp.maximum(c - ncc, 0), 0)),
                  pl.BlockSpec((nb, TOK, d), lambda b, c: (b, jnp.minimum(c, ncc - 1), 0)),
                  mods.spec(0, 2, nb=nb), mods.spec(0, 3, nb=nb), mods.spec(0, 4, nb=nb),
                  const((1, d)), const((1, GLA_DV)), const((lw + GLA_VAL, d)),
                  const((2 * ROUTER_ROWS, d)), const((ROUTER_ROWS, 1))],
        out_specs=out_specs, out_shape=out_shape,
        scratch_shapes=[pltpu.VMEM((ROUTER_ROWS, LANES), F32)],
        compiler_params=_cparams(("arbitrary", "arbitrary")),
        name="post0",
    )(lru_f, lru_r, gla_f, gla_r, pm, pm, x, ctx, mods.table, mods.table, mods.table, n2, gn, w_out, wr, br)


def _post1(ncc, att, x1, mods, n2, w_o, wr, br):
    batch, seq, aw = att.shape
    d = x1.shape[-1]
    nl = seq // TOK
    const = lambda shape: pl.BlockSpec(shape, lambda b, c: (0,) * len(shape))
    nb = TOK_ROWS
    out_specs, out_shape = _route_out(batch, nl, d, nb)
    return pl.pallas_call(
        _post1_kernel,
        grid=(batch // nb, nl),
        in_specs=[pl.BlockSpec((nb, TOK, aw), lambda b, c: (b, c, 0)),
                  pl.BlockSpec((nb, TOK, d), lambda b, c: (b, c + ncc, 0)),
                  mods.spec(1, 2, True, nb), mods.spec(1, 3, True, nb), mods.spec(1, 4, True, nb),
                  const((1, d)), const((aw, d)), const((2 * ROUTER_ROWS, d)), const((ROUTER_ROWS, 1))],
        out_specs=out_specs, out_shape=out_shape,
        scratch_shapes=[pltpu.VMEM((ROUTER_ROWS, LANES), F32)],
        compiler_params=_cparams(("arbitrary", "arbitrary")),
        name="post1",
    )(att, x1, mods.table, mods.table, mods.table, n2, w_o, wr, br)


_PAIR_LO = np.array([0, 0, 0, 1, 1, 2], np.int32)
_PAIR_HI = np.array([1, 2, 3, 2, 3, 3], np.int32)


def _plan(cr, cnt, n_tiles):
    cls, rank = cr[:, :, 0, :], cr[:, :, 1, :]
    counts = cnt[:N_CLASSES, 0].astype(I32)
    tiles = (counts + MOE_TILE - 1) // MOE_TILE
    ends = jnp.cumsum(tiles)
    offs = (ends - tiles) * MOE_TILE
    pos = rank
    for c in range(N_CLASSES):
        pos = pos + jnp.where(cls == c, offs[c], 0)
    pos = pos.reshape(-1)
    tile_ids = jnp.arange(n_tiles, dtype=I32)
    tile_cls = jnp.minimum(jnp.sum((tile_ids[:, None] >= ends[None, :]).astype(I32), axis=1), N_CLASSES - 1)
    grp = tile_cls // PAIRS_PER_GROUP
    pair = tile_cls % PAIRS_PER_GROUP
    lo = grp * EXPERTS_PER_GROUP + jnp.asarray(_PAIR_LO)[pair]
    hi = grp * EXPERTS_PER_GROUP + jnp.asarray(_PAIR_HI)[pair]
    return pos.astype(I32), lo.astype(I32), hi.astype(I32), ends[-1:].astype(I32)


def _moe_kernel(d, lo_ref, hi_ref, used_ref, x_ref, w1l, w3l, w2l, w1h, w3h, w2h, y_ref, wbuf, wbuf2):
    i = pl.program_id(0)
    active = i < used_ref[0]

    @pl.when(active)
    def _():
        changed = jnp.logical_or(i == 0, jnp.logical_or(lo_ref[i] != lo_ref[jnp.maximum(i - 1, 0)],
                                                        hi_ref[i] != hi_ref[jnp.maximum(i - 1, 0)]))

        @pl.when(changed)
        def _():
            for k, w in enumerate((w1l, w3l, w1h, w3h)):
                wbuf[k] = w[0, 0].astype(BF16)
            wbuf2[0] = w2l[0, 0].astype(BF16)
            wbuf2[1] = w2h[0, 0].astype(BF16)

        x = _unpack_bf16_pairs(x_ref[:, :d // 2])
        wts = pltpu.bitcast(x_ref[:, d // 2:], F32)

        def expert(k1, k3, k2):
            a = jnp.dot(x, wbuf[k1], preferred_element_type=F32)
            b = jnp.dot(x, wbuf[k3], preferred_element_type=F32)
            return jnp.dot((_silu(a) * b).astype(BF16), wbuf2[k2], preferred_element_type=F32)

        y = wts[:, 0:1] * expert(0, 1, 0) + wts[:, 1:2] * expert(2, 3, 1)
        y_ref[...] = _pack_bf16_pairs(y.astype(BF16))

    @pl.when(jnp.logical_not(active))
    def _():
        y_ref[...] = jnp.zeros_like(y_ref)


def _moe(xs, lo, hi, n_used, layer, w1, w3, w2):
    n_pad, wd = xs.shape
    d = 2 * (wd - WEXT)
    hdim = w1.shape[-1]
    n_tiles = n_pad // MOE_TILE
    wl = lambda i, lo, hi, used: (layer, lo[i], 0, 0)
    wh = lambda i, lo, hi, used: (layer, hi[i], 0, 0)
    up = lambda imap: pl.BlockSpec((1, 1, d, hdim), imap)
    down = lambda imap: pl.BlockSpec((1, 1, hdim, d), imap)
    grid_spec = pltpu.PrefetchScalarGridSpec(
        num_scalar_prefetch=3,
        grid=(n_tiles,),
        in_specs=[pl.BlockSpec((MOE_TILE, wd), lambda i, lo, hi, used: (i, 0)),
                  up(wl), up(wl), down(wl), up(wh), up(wh), down(wh)],
        out_specs=pl.BlockSpec((MOE_TILE, d // 2), lambda i, lo, hi, used: (i, 0)),
        scratch_shapes=[pltpu.VMEM((4, d, hdim), BF16), pltpu.VMEM((2, hdim, d), BF16)],
    )
    return pl.pallas_call(
        functools.partial(_moe_kernel, d),
        grid_spec=grid_spec,
        out_shape=jax.ShapeDtypeStruct((n_pad, d // 2), U32),
        compiler_params=_cparams(("arbitrary",)),
        name="moe_experts",
    )(lo, hi, n_used, xs, w1, w3, w2, w1, w3, w2)


def _row_dma_loops(tile_rows, make_copy, index):
    def issue(grp, carry):
        for sub in range(SUBLANES):
            make_copy(grp, sub, index(grp * SUBLANES + sub)).start()
        return carry

    def drain(grp, carry):
        for _ in range(SUBLANES):
            make_copy(0, 0, 0).wait()
        return carry

    n_groups = tile_rows // SUBLANES

    def issue_all():
        for grp in range(n_groups):
            issue(grp, 0)

    return issue_all, (lambda: lax.fori_loop(0, n_groups, drain, 0))


def _scatter_rows_kernel(rows, pos_ref, src_ref, init_ref, dst_ref, sem):
    del init_ref
    issue, drain = _row_dma_loops(
        rows,
        lambda grp, sub, p: pltpu.make_async_copy(src_ref.at[grp, pl.ds(sub, 1)], dst_ref.at[pl.ds(p, 1)], sem),
        lambda j: pos_ref[0, 0, j])
    issue()
    drain()


def _scatter_rows(src, pos, n_pad):
    n, w = src.shape
    rows = PERM_ROWS if n % PERM_ROWS == 0 else TOK
    steps = n // rows
    return pl.pallas_call(
        functools.partial(_scatter_rows_kernel, rows),
        grid=(steps,),
        in_specs=[pl.BlockSpec((1, 1, rows), lambda i: (i, 0, 0), memory_space=pltpu.SMEM),
                  pl.BlockSpec((rows // SUBLANES, SUBLANES, w), lambda i: (i, 0, 0)),
                  pl.BlockSpec(memory_space=pl.ANY)],
        out_specs=pl.BlockSpec(memory_space=pl.ANY),
        out_shape=jax.ShapeDtypeStruct((n_pad, w), src.dtype),
        scratch_shapes=[pltpu.SemaphoreType.DMA(())],
        input_output_aliases={2: 0},
        compiler_params=pltpu.CompilerParams(dimension_semantics=("arbitrary",), has_side_effects=True),
        name="scatter_rows",
    )(pos.reshape(steps, 1, rows), src.reshape(n // SUBLANES, SUBLANES, w), jnp.zeros((n_pad, w), src.dtype))


def _gather_residual_kernel(n_steps, pos_ref, posn_ref, ys_ref, xm_ref, gate_ref, o_ref, buf, sem):
    step = pl.program_id(0) * pl.num_programs(1) + pl.program_id(1)
    slot = step % 2

    def loops(idx_ref, s):
        return _row_dma_loops(
            TOK,
            lambda grp, sub, p: pltpu.make_async_copy(ys_ref.at[pl.ds(p, 1)], buf.at[s, grp, pl.ds(sub, 1)],
                                                      sem.at[s]),
            lambda j: idx_ref[0, 0, j])

    @pl.when(step == 0)
    def _():
        loops(pos_ref, slot)[0]()

    @pl.when(step + 1 < n_steps)
    def _():
        loops(posn_ref, 1 - slot)[0]()

    loops(pos_ref, slot)[1]()
    y = _unpack_bf16_pairs(buf[slot].reshape(TOK, buf.shape[-1])).astype(F32)
    o_ref[0] = xm_ref[0] + gate_ref[0, 0] * y


def _gather_residual(ys, pos, xmid, gate_spec, mod_table):
    batch, t, d = xmid.shape
    nch = t // TOK
    n_steps = batch * nch
    return pl.pallas_call(
        functools.partial(_gather_residual_kernel, n_steps),
        grid=(batch, nch),
        in_specs=[pl.BlockSpec((1, 1, TOK), lambda b, c: (b * nch + c, 0, 0), memory_space=pltpu.SMEM),
                  pl.BlockSpec((1, 1, TOK), lambda b, c: (jnp.minimum(b * nch + c + 1, n_steps - 1), 0, 0),
                               memory_space=pltpu.SMEM),
                  pl.BlockSpec(memory_space=pl.ANY),
                  pl.BlockSpec((1, TOK, d), lambda b, c: (b, c, 0)),
                  gate_spec],
        out_specs=pl.BlockSpec((1, TOK, d), lambda b, c: (b, c, 0)),
        out_shape=jax.ShapeDtypeStruct((batch, t, d), F32),
        scratch_shapes=[pltpu.VMEM((2, TOK // SUBLANES, SUBLANES, d // 2), U32), pltpu.SemaphoreType.DMA((2,))],
        compiler_params=_cparams(("arbitrary", "arbitrary")),
        name="gather_residual",
    )(pos.reshape(n_steps, 1, TOK), pos.reshape(n_steps, 1, TOK), ys, xmid, mod_table)


def _moe_block(h2ext, cr, cnt, xmid, gate_spec, mod_table, layer, w1, w3, w2):
    h2ext = h2ext.reshape(-1, h2ext.shape[-1])
    n = h2ext.shape[0]
    n_tiles = n // MOE_TILE + N_CLASSES
    pos, lo, hi, n_used = _plan(cr, cnt, n_tiles)
    xs = _scatter_rows(h2ext, pos, n_tiles * MOE_TILE)
    ys = _moe(xs, lo, hi, n_used, layer, w1, w3, w2)
    return _gather_residual(ys, pos, xmid, gate_spec, mod_table)


def _qkv_kernel(ncc, x_ref, g_ref, sh_ref, sc_ref, w_ref, qn_ref, kn_ref, cos_ref, sin_ref,
                q_ref, k_ref, v_ref):
    c = pl.program_id(1)
    latent = c >= ncc
    nb = x_ref.shape[0]
    nq = ATT_HEADS * HEAD_DIM
    nk = KV_HEADS * HEAD_DIM
    cos = cos_ref[...]
    sin = sin_ref[...]
    hbs = [None] * nb

    def head(z, gain):
        z = _rms(z, gain)
        rot = z * cos + pltpu.roll(z, HEAD_DIM // 2, 1) * sin
        return jnp.where(latent, rot, z)

    def kv_row(r):
        h = _rms(x_ref[r], g_ref[...]) * (1.0 + sc_ref[0, r]) + sh_ref[0, r]
        hbs[r] = h.astype(BF16)
        kv = jnp.dot(hbs[r], w_ref[:, nq:], preferred_element_type=F32)
        yield
        k_ref[r] = jnp.concatenate(
            [head(kv[:, i * HEAD_DIM:(i + 1) * HEAD_DIM], kn_ref[...]) for i in range(KV_HEADS)],
            axis=1).astype(BF16)
        ones = jnp.ones((TOK, HEAD_DIM), BF16)
        v_ref[r] = jnp.concatenate(
            [blk for i in range(KV_HEADS)
             for blk in (kv[:, nk + i * HEAD_DIM:nk + (i + 1) * HEAD_DIM].astype(BF16), ones)], axis=1)

    def q_row(r):
        qq = jnp.dot(hbs[r], w_ref[:, :nq], preferred_element_type=F32)
        yield
        q_ref[r] = jnp.concatenate(
            [head(qq[:, i * HEAD_DIM:(i + 1) * HEAD_DIM], qn_ref[...]) * (HEAD_DIM ** -0.5)
             for i in range(ATT_HEADS)], axis=1).astype(BF16)

    _lockstep([kv_row(r) for r in range(nb)])

    @pl.when(latent)
    def _():
        _lockstep([q_row(r) for r in range(nb)])


def _qkv(ncc, x1, gain, mods, w_qkv, qn, kn, cos, sin):
    batch, t, d = x1.shape
    nch = t // TOK
    seq = t - ncc * TOK
    nq = ATT_HEADS * HEAD_DIM
    nk = KV_HEADS * HEAD_DIM
    const = lambda shape: pl.BlockSpec(shape, lambda b, c: (0,) * len(shape))
    lat = lambda b, c: (b, jnp.maximum(c - ncc, 0), 0)
    nb = QKV_ROWS
    return pl.pallas_call(
        functools.partial(_qkv_kernel, ncc),
        grid=(batch // nb, nch),
        in_specs=[pl.BlockSpec((nb, TOK, d), lambda b, c: (b, c, 0)),
                  const((1, d)), mods.spec(1, 0, nb=nb), mods.spec(1, 1, nb=nb),
                  const((d, nq + 2 * nk)), const((1, HEAD_DIM)), const((1, HEAD_DIM)),
                  pl.BlockSpec((TOK, HEAD_DIM), lambda b, c: (jnp.maximum(c - ncc, 0), 0)),
                  pl.BlockSpec((TOK, HEAD_DIM), lambda b, c: (jnp.maximum(c - ncc, 0), 0))],
        out_specs=[pl.BlockSpec((nb, TOK, nq), lat),
                   pl.BlockSpec((nb, TOK, nk), lambda b, c: (b, c, 0)),
                   pl.BlockSpec((nb, TOK, 2 * nk), lambda b, c: (b, c, 0))],
        out_shape=[jax.ShapeDtypeStruct((batch, seq, nq), BF16),
                   jax.ShapeDtypeStruct((batch, t, nk), BF16),
                   jax.ShapeDtypeStruct((batch, t, 2 * nk), BF16)],
        compiler_params=_cparams(("arbitrary", "arbitrary")),
        name="qkv",
    )(x1, gain, mods.table, mods.table, w_qkv, qn, kn, cos, sin)


def _attn_kernel(n_kb, q_ref, k_ref, v_ref, o_ref):
    q = q_ref[0]
    qs = jnp.concatenate([q[:, g * HEAD_DIM:(g + 1) * HEAD_DIM] for g in range(Q_PER_KV)], axis=0)
    kb = k_ref.shape[1] // n_kb
    m = acc = None
    for j in range(n_kb):
        s = lax.dot_general(qs, k_ref[0, j * kb:(j + 1) * kb, :], (((1,), (1,)), ((), ())),
                            preferred_element_type=F32)
        m_blk = jnp.max(s, axis=-1, keepdims=True)
        m_new = m_blk if m is None else jnp.maximum(m, m_blk)
        p = jnp.exp((s - m_new).astype(BF16))
        pv = jnp.dot(p, v_ref[0, j * kb:(j + 1) * kb, :], preferred_element_type=F32)
        acc = pv if acc is None else jnp.exp(m - m_new) * acc + pv
        m = m_new
    o = acc[:, :HEAD_DIM] / acc[:, HEAD_DIM:HEAD_DIM + 1]
    o_ref[0] = jnp.concatenate([o[g * ATT_Q:(g + 1) * ATT_Q] for g in range(Q_PER_KV)], axis=1).astype(BF16)


def _attention(q, k, v):
    batch, seq, nq = q.shape
    t = k.shape[1]
    gw = Q_PER_KV * HEAD_DIM
    n_kb = next(n for n in (17, 2, 1) if t % (n * LANES) == 0)
    return pl.pallas_call(
        functools.partial(_attn_kernel, n_kb),
        grid=(batch, KV_HEADS, seq // ATT_Q),
        in_specs=[pl.BlockSpec((1, ATT_Q, gw), lambda b, h, i: (b, i, h)),
                  pl.BlockSpec((1, t, HEAD_DIM), lambda b, h, i: (b, 0, h)),
                  pl.BlockSpec((1, t, 2 * HEAD_DIM), lambda b, h, i: (b, 0, h))],
        out_specs=pl.BlockSpec((1, ATT_Q, gw), lambda b, h, i: (b, i, h)),
        out_shape=jax.ShapeDtypeStruct((batch, seq, nq), BF16),
        compiler_params=_cparams(("arbitrary", "arbitrary", "arbitrary")),
        name="attention",
    )(q, k, v)


def _block_diag(w):
    nb, bs, _ = w.shape
    eye = jnp.eye(nb, dtype=w.dtype)
    return (eye[:, None, :, None] * w[:, :, None, :]).reshape(nb * bs, nb * bs)


def _router_weights(wg, bg, we, be):
    d = wg.shape[0]
    n = N_GROUPS + N_EXPERTS
    wr = jnp.zeros((ROUTER_ROWS, d), F32).at[:N_GROUPS].set(wg.T).at[N_GROUPS:n].set(we.T)
    br = jnp.zeros((ROUTER_ROWS, 1), F32).at[:N_GROUPS, 0].set(bg).at[N_GROUPS:n, 0].set(be)
    w_hi = wr.astype(BF16)
    w_lo = (wr - w_hi.astype(F32)).astype(BF16)
    return jnp.concatenate([w_hi, w_lo], axis=0), br


def _rope_tables(seq):
    rows = seq // GRID_W
    row = np.repeat(np.arange(rows, dtype=np.float32), GRID_W)
    col = np.tile(np.arange(GRID_W, dtype=np.float32), rows)
    ppa = HEAD_DIM // 4
    freqs = (ROPE_THETA ** (-np.arange(ppa, dtype=np.float32) / ppa)).astype(np.float32)
    ang = np.concatenate([row[:, None] * freqs, col[:, None] * freqs], axis=-1)
    cos, sin = np.cos(ang), np.sin(ang)
    return (jnp.asarray(np.concatenate([cos, cos], axis=-1), F32),
            jnp.asarray(np.concatenate([-sin, sin], axis=-1), F32))


_HALF_SPLIT = np.concatenate([np.arange(0, HEAD_DIM, 2), np.arange(1, HEAD_DIM, 2)])


def kernel(x, c, ctx, c_ctx, norm1, norm2, w_ada, b_ada, ev_w_in, ev_conv_w, ev_conv_b, ev_lru_wa, ev_lru_ba, ev_lru_wi, ev_lru_bi, ev_lru_lam, ev_gla_wg, ev_gla_bg, ev_gla_norm, ev_w_out, od_w_qkv, od_q_norm, od_k_norm, od_w_o, moe_wg, moe_bg, moe_we, moe_be, moe_w1, moe_w3, moe_w2):
    batch, seq, d = x.shape
    ctx_len = ctx.shape[1]
    assert seq % TOK == 0 and ctx_len % TOK == 0 and d % GLA_VAL == 0 and seq % GRID_W == 0
    ncc = ctx_len // TOK
    lw = d // 2

    assert batch % TOK_ROWS == 0
    rows = -(-(batch + TOK_ROWS) // SUBLANES) * SUBLANES
    cv = jnp.zeros((rows, d), F32).at[:batch].set(c).at[batch:batch + TOK_ROWS].set(c_ctx)
    table = _adaln(cv, w_ada, b_ada)
    depth = table.shape[0]
    table = table.reshape(depth, rows, 6, d).transpose(0, 2, 1, 3).reshape(depth * 6, rows, 1, d)
    mods = _Mods(table, batch, ncc)

    w_in = ev_w_in[0]
    nm = 2 * lw + 2 * GLA_KEY + 2 * GLA_VAL
    w_main = w_in[:, :nm].astype(BF16)
    w_lr = jnp.zeros((d, LANES), F32).at[:, :2 * GLA_RANK].set(w_in[:, nm:]).astype(BF16)
    pm, lr = _inproj(x, ctx, norm1[0][None], mods, w_main, w_lr, ev_conv_w[0], ev_conv_b[0][None], lw)

    mixed = []
    for dr in range(2):
        w_gate = jnp.concatenate([_block_diag(ev_lru_wa[0, dr]), _block_diag(ev_lru_wi[0, dr])], axis=1).astype(BF16)
        b_gate = jnp.concatenate([ev_lru_ba[0, dr], ev_lru_bi[0, dr]])[None]
        wg_pad = jnp.zeros((LANES, GLA_KEY), F32).at[dr * GLA_RANK:(dr + 1) * GLA_RANK].set(ev_gla_wg[0, dr]).astype(BF16)
        mixed.append(_seqmix(dr == 1, ncc, pm, lr, w_gate, b_gate,
                             ev_lru_lam[0, dr][None], wg_pad, ev_gla_bg[0, dr][None], lw))
    (lru_f, gla_f), (lru_r, gla_r) = mixed

    wr0, br0 = _router_weights(moe_wg[0], moe_bg[0], moe_we[0], moe_be[0])
    xmid0, h2e0, cr0, cnt0 = _post0(ncc, lru_f, lru_r, gla_f, gla_r, pm, x, ctx, mods, norm2[0][None],
                                    ev_gla_norm[0][None], ev_w_out[0].astype(BF16), wr0, br0, lw)
    x1 = _moe_block(h2e0, cr0, cnt0, xmid0, mods.spec(0, 5), mods.table, 0, moe_w1, moe_w3, moe_w2)

    nq = ATT_HEADS * HEAD_DIM
    nk = KV_HEADS * HEAD_DIM
    perm = np.concatenate([h * HEAD_DIM + _HALF_SPLIT for h in range(ATT_HEADS + KV_HEADS)]
                          + [np.arange(nq + nk, nq + 2 * nk)])
    w_qkv = od_w_qkv[0][:, perm].astype(BF16)
    cos, sin = _rope_tables(seq)
    q, k, v = _qkv(ncc, x1, norm1[1][None], mods, w_qkv, od_q_norm[0][_HALF_SPLIT][None],
                   od_k_norm[0][_HALF_SPLIT][None], cos, sin)
    att = _attention(q, k, v)
    wr1, br1 = _router_weights(moe_wg[1], moe_bg[1], moe_we[1], moe_be[1])
    xmid1, h2e1, cr1, cnt1 = _post1(ncc, att, x1, mods, norm2[1][None], od_w_o[0].astype(BF16), wr1, br1)
    return _moe_block(h2e1, cr1, cnt1, xmid1, mods.spec(1, 5, True), mods.table, 1, moe_w1, moe_w3, moe_w2)
```

```python
import functools
import itertools

import numpy as np
import jax
import jax.numpy as jnp
from jax import lax
from jax.experimental import pallas as pl
from jax.experimental.pallas import tpu as pltpu

F32 = jnp.float32
BF16 = jnp.bfloat16
I32 = jnp.int32
U32 = jnp.uint32
HIGHEST = lax.Precision.HIGHEST

EPS = 1e-6
GRID_W = 64
LOG2_E = float(np.log2(np.e))
LRU_C = 8.0
GLA_HEADS = 4
GLA_DK = 64
GLA_DV = 128
GLA_KEY = GLA_HEADS * GLA_DK
GLA_VAL = GLA_HEADS * GLA_DV
GLA_RANK = 16
GLA_TAU = 16.0
GLA_CHUNK = 64
_DK_SHIFT = GLA_DK.bit_length() - 1
ATT_HEADS = 8
KV_HEADS = 2
Q_PER_KV = ATT_HEADS // KV_HEADS
HEAD_DIM = 128
ROPE_THETA = 10000.0
N_GROUPS = 4
EXPERTS_PER_GROUP = 4
N_EXPERTS = N_GROUPS * EXPERTS_PER_GROUP
PAIRS_PER_GROUP = 6
N_CLASSES = N_GROUPS * PAIRS_PER_GROUP

LANES = 128
SUBLANES = 8
TOK = 256
MOE_TILE = 512
ATT_Q = 256
PERM_ROWS = 2048
TOK_ROWS = 4
QKV_ROWS = 2
SEQ_ROWS = 4
WEXT = LANES
ROUTER_ROWS = 32
VMEM_LIMIT = 56 * 1024 * 1024


def _cparams(sem):
    return pltpu.CompilerParams(dimension_semantics=sem, vmem_limit_bytes=VMEM_LIMIT)


def _rms(x, g):
    return x * lax.rsqrt(jnp.mean(x * x, axis=-1, keepdims=True) + EPS) * g


def _sigmoid(x):
    return 1.0 / (1.0 + jnp.exp(-x))


def _silu(x):
    return x * _sigmoid(x)


def _gelu_tanh(x):
    return 0.5 * x * (1.0 + jnp.tanh(np.sqrt(2.0 / np.pi).astype(np.float32) * (x + 0.044715 * (x * x * x))))


def _pack_bf16_pairs(xb):
    k = xb.shape[-1] // 2
    lo = lax.shift_right_logical(pltpu.bitcast(xb[:, :k].astype(F32), U32), jnp.uint32(16))
    hi = pltpu.bitcast(xb[:, k:].astype(F32), U32) & jnp.uint32(0xFFFF0000)
    return hi | lo


def _unpack_bf16_pairs(words):
    lo = pltpu.bitcast(lax.shift_left(words, jnp.uint32(16)), F32).astype(BF16)
    hi = pltpu.bitcast(words & jnp.uint32(0xFFFF0000), F32).astype(BF16)
    return jnp.concatenate([lo, hi], axis=1)


def _softplus(x):
    return jnp.maximum(x, 0.0) + jnp.log(1.0 + jnp.exp(-jnp.abs(x)))


def _log_sigmoid(x):
    return -_softplus(-x)


def _adaln_kernel(cv_ref, w_ref, b_ref, o_ref):
    s = _silu(cv_ref[...])
    o_ref[0] = jnp.dot(s, w_ref[0], precision=HIGHEST, preferred_element_type=F32) + b_ref[0]


def _adaln(cv, w_ada, b_ada):
    depth, d, n6 = w_ada.shape
    rows = cv.shape[0]
    tn = 6 * d // 4
    return pl.pallas_call(
        _adaln_kernel,
        grid=(depth, n6 // tn),
        in_specs=[pl.BlockSpec((rows, d), lambda l, j: (0, 0)),
                  pl.BlockSpec((1, d, tn), lambda l, j: (l, 0, j)),
                  pl.BlockSpec((1, 1, tn), lambda l, j: (l, 0, j))],
        out_specs=pl.BlockSpec((1, rows, tn), lambda l, j: (l, 0, j)),
        out_shape=jax.ShapeDtypeStruct((depth, rows, n6), F32),
        compiler_params=_cparams(("arbitrary", "arbitrary")),
        name="adaln",
    )(cv, w_ada, b_ada.reshape(depth, 1, n6))


class _Mods:
    def __init__(self, table, batch, ncc):
        self.table, self.batch, self.ncc = table, batch, ncc
        self.d = table.shape[-1]

    def spec(self, layer, j, latent_only=False, nb=1):
        batch, ncc = self.batch, self.ncc

        def imap(b, c):
            r = b if latent_only else jnp.where(c < ncc, batch // nb, b)
            return (layer * 6 + j, r, 0, 0)

        return pl.BlockSpec((1, nb, 1, self.d), imap)


def _lockstep(generators):
    for _ in itertools.zip_longest(*generators):
        pass


def _inproj_kernel(ncc, nch, lw, x_ref, c_ref, xp_ref, xn_ref, cp_ref, cn_ref, g_ref, sh_ref, sc_ref,
                   w_ref, wlr_ref, cw_ref, cb_ref, pm_ref, lr_ref):
    _lockstep([_inproj_row(ncc, nch, lw, i, x_ref, c_ref, xp_ref, xn_ref, cp_ref, cn_ref, g_ref, sh_ref, sc_ref,
                           w_ref, wlr_ref, cw_ref, cb_ref, pm_ref, lr_ref) for i in range(x_ref.shape[0])])


def _inproj_row(ncc, nch, lw, i, x_ref, c_ref, xp_ref, xn_ref, cp_ref, cn_ref, g_ref, sh_ref, sc_ref,
                w_ref, wlr_ref, cw_ref, cb_ref, pm_ref, lr_ref):
    c = pl.program_id(1)
    is_ctx = c < ncc

    def modulated(rows):
        return (_rms(rows, g_ref[...]) * (1.0 + sc_ref[0, i]) + sh_ref[0, i]).astype(BF16)

    hb = modulated(jnp.where(is_ctx, c_ref[i], x_ref[i]))
    pm = jnp.dot(hb, w_ref[...], preferred_element_type=F32)
    yield
    lr_ref[i] = jnp.dot(hb, wlr_ref[...], preferred_element_type=F32).astype(BF16)

    halo = jnp.concatenate([jnp.where(is_ctx, cp_ref[i], xp_ref[i]), jnp.where(is_ctx, cn_ref[i], xn_ref[i])], axis=0)
    xa_halo = jnp.dot(modulated(halo), w_ref[:, :lw], preferred_element_type=F32)
    yield
    has_prev = jnp.logical_and(c != 0, c != ncc)
    has_next = jnp.logical_and(c != ncc - 1, c != nch - 1)
    p2 = jnp.where(has_prev, xa_halo[SUBLANES - 2:SUBLANES - 1], 0.0)
    p1 = jnp.where(has_prev, xa_halo[SUBLANES - 1:SUBLANES], 0.0)
    n0 = jnp.where(has_next, xa_halo[SUBLANES:SUBLANES + 1], 0.0)
    xa = pm[:, :lw]
    row = lax.broadcasted_iota(I32, xa.shape, 0)
    x_m1 = jnp.where(row == 0, p1, pltpu.roll(xa, 1, 0))
    x_m2 = jnp.where(row == 0, p2, jnp.where(row == 1, p1, pltpu.roll(xa, 2, 0)))
    x_p1 = jnp.where(row == TOK - 1, n0, pltpu.roll(xa, TOK - 1, 0))
    cw = cw_ref[...]
    u = cw[0:1] * x_m2 + cw[1:2] * x_m1 + cw[2:3] * xa + cw[3:4] * x_p1 + cb_ref[...]
    pm_ref[i, :, :lw] = u.astype(BF16)
    pm_ref[i, :, lw:] = pm[:, lw:].astype(BF16)


def _inproj(x, ctx, gain, mods, w_main, w_lr, conv_w, conv_b, lw):
    batch, seq, d = x.shape
    ncc = ctx.shape[1] // TOK
    nl = seq // TOK
    nch = ncc + nl
    nm = w_main.shape[1]
    hb = TOK // SUBLANES
    nb = TOK_ROWS
    return pl.pallas_call(
        functools.partial(_inproj_kernel, ncc, nch, lw),
        grid=(batch // nb, nch),
        in_specs=[pl.BlockSpec((nb, TOK, d), lambda b, c: (b, jnp.maximum(c - ncc, 0), 0)),
                  pl.BlockSpec((nb, TOK, d), lambda b, c: (b, jnp.minimum(c, ncc - 1), 0)),
                  pl.BlockSpec((nb, SUBLANES, d), lambda b, c: (b, jnp.maximum((c - ncc) * hb - 1, 0), 0)),
                  pl.BlockSpec((nb, SUBLANES, d),
                               lambda b, c: (b, jnp.clip((c - ncc + 1) * hb, 0, nl * hb - 1), 0)),
                  pl.BlockSpec((nb, SUBLANES, d), lambda b, c: (b, jnp.clip(c * hb - 1, 0, ncc * hb - 1), 0)),
                  pl.BlockSpec((nb, SUBLANES, d), lambda b, c: (b, jnp.clip((c + 1) * hb, 0, ncc * hb - 1), 0)),
                  pl.BlockSpec((1, d), lambda b, c: (0, 0)),
                  mods.spec(0, 0, nb=nb), mods.spec(0, 1, nb=nb),
                  pl.BlockSpec((d, nm), lambda b, c: (0, 0)),
                  pl.BlockSpec((d, LANES), lambda b, c: (0, 0)),
                  pl.BlockSpec((4, lw), lambda b, c: (0, 0)),
                  pl.BlockSpec((1, lw), lambda b, c: (0, 0))],
        out_specs=[pl.BlockSpec((nb, TOK, nm), lambda b, c: (b, c, 0)),
                   pl.BlockSpec((nb, TOK, LANES), lambda b, c: (b, c, 0))],
        out_shape=[jax.ShapeDtypeStruct((batch, nch * TOK, nm), BF16),
                   jax.ShapeDtypeStruct((batch, nch * TOK, LANES), BF16)],
        compiler_params=_cparams(("arbitrary", "arbitrary")),
        name="inproj",
    )(x, ctx, x, x, ctx, ctx, gain, mods.table, mods.table, w_main, w_lr, conv_w, conv_b)


def _lru_scan(a, b, h0, rev):
    n_groups = TOK // SUBLANES
    a = a.reshape(n_groups, SUBLANES, a.shape[-1])
    b = b.reshape(a.shape)
    sub = lax.broadcasted_iota(I32, a.shape, 1)
    for dist in (1, 2, 4):
        shift = SUBLANES - dist if rev else dist
        a_s = pltpu.roll(a, shift, 1)
        b_s = pltpu.roll(b, shift, 1)
        m = (sub < SUBLANES - dist) if rev else (sub >= dist)
        b = jnp.where(m, a * b_s + b, b)
        a = jnp.where(m, a * a_s, a)
    order = range(n_groups - 1, -1, -1) if rev else range(n_groups)
    outs = [None] * n_groups
    h = h0
    for r in order:
        hr = a[r] * h + b[r]
        outs[r] = hr
        h = hr[0:1] if rev else hr[SUBLANES - 1:SUBLANES]
    return jnp.concatenate(outs, axis=0), h


def _seqmix_kernel(rev, *refs):
    h_scr, s_scr = refs[-2:]

    @pl.when(pl.program_id(1) == 0)
    def _():
        h_scr[...] = jnp.zeros_like(h_scr)
        s_scr[...] = jnp.zeros_like(s_scr)

    _lockstep([_seqmix_row(rev, i, *refs) for i in range(refs[0].shape[0])])


def _seqmix_row(rev, i, u_ref, q_ref, k_ref, v_ref, lr_ref,
                wgt_ref, bgt_ref, lam_ref, wg_ref, bg_ref, sel_ref,
                lru_ref, gla_ref, h_scr, s_scr):
    lw = u_ref.shape[-1]
    ub = u_ref[i]
    u = ub.astype(F32)
    gates = jnp.dot(ub, wgt_ref[...], preferred_element_type=F32) + bgt_ref[...]
    yield
    r_gate = _sigmoid(gates[:, :lw])
    i_gate = _sigmoid(gates[:, lw:])
    log2a_unit = (-LRU_C * LOG2_E) * _softplus(-lam_ref[...])
    a = jnp.exp2(r_gate * log2a_unit)
    bb = jnp.sqrt(1.0 - a * a) * (i_gate * u)
    hs, h_last = _lru_scan(a, bb, h_scr[i, 0:1], rev)
    h_scr[i, 0:1] = h_last
    lru_ref[i] = hs.astype(BF16)
    yield

    lg = _log_sigmoid(jnp.dot(lr_ref[i], wg_ref[...], preferred_element_type=F32) + bg_ref[...]) * (1.0 / GLA_TAU)
    lg_hi = lg.astype(BF16)
    lg_lo = (lg - lg_hi.astype(F32)).astype(BF16)
    yield
    cum = jnp.dot(sel_ref[...], jnp.concatenate([lg_hi, lg_lo], axis=1), preferred_element_type=F32)
    yield
    g = cum[:TOK, :GLA_KEY] + cum[:TOK, GLA_KEY:]
    tot = cum[TOK:, :GLA_KEY] + cum[TOK:, GLA_KEY:]
    q = q_ref[i].astype(F32)
    k = k_ref[i].astype(F32)
    dec_all = jnp.exp(tot)
    k_neg = k * jnp.exp(-g)
    qg = (q * jnp.exp(g) * (GLA_DK ** -0.5)).astype(BF16)
    kg = k_neg.astype(BF16)
    kd = (k_neg * dec_all).astype(BF16)
    v = v_ref[i]

    lane = lax.broadcasted_iota(I32, (1, GLA_KEY), 1) >> _DK_SHIFT
    head_masks = [(lane == h) for h in range(GLA_HEADS)]
    n_stack = GLA_HEADS * GLA_CHUNK
    ai = lax.broadcasted_iota(I32, (n_stack, GLA_CHUNK), 0) & (GLA_CHUNK - 1)
    aj = lax.broadcasted_iota(I32, (n_stack, GLA_CHUNK), 1)
    causal = (aj >= ai) if rev else (aj <= ai)

    n_sub = TOK // GLA_CHUNK
    order = range(n_sub - 1, -1, -1) if rev else range(n_sub)
    outs = [None] * n_sub
    st = s_scr[i]
    for n in order:
        sl = slice(n * GLA_CHUNK, (n + 1) * GLA_CHUNK)
        qn = qg[sl]
        qs = jnp.concatenate([jnp.where(head_masks[h], qn, jnp.zeros_like(qn)) for h in range(GLA_HEADS)], axis=0)
        att = lax.dot_general(qs, kg[sl], (((1,), (1,)), ((), ())), preferred_element_type=F32)
        yield
        att = jnp.where(causal, att, 0.0).astype(BF16)
        o_all = jnp.dot(att, v[sl], preferred_element_type=F32)
        yield
        o_int = lax.dot_general(qs, st.astype(BF16), (((1,), (1,)), ((), ())),
                                preferred_element_type=F32)
        yield
        outs[n] = jnp.concatenate(
            [o_all[h * GLA_CHUNK:(h + 1) * GLA_CHUNK, h * GLA_DV:(h + 1) * GLA_DV]
             + o_int[h * GLA_CHUNK:(h + 1) * GLA_CHUNK] for h in range(GLA_HEADS)], axis=1)
        vn, kdn = v[sl], kd[sl]
        v_stack = jnp.concatenate([vn[:, h * GLA_DV:(h + 1) * GLA_DV] for h in range(GLA_HEADS)], axis=0)
        kd_stack = jnp.concatenate([jnp.where(head_masks[h], kdn, jnp.zeros_like(kdn)) for h in range(GLA_HEADS)],
                                   axis=0)
        ds = lax.dot_general(v_stack, kd_stack, (((0,), (0,)), ((), ())), preferred_element_type=F32)
        yield
        st = dec_all[n * GLA_CHUNK:n * GLA_CHUNK + 1] * st + ds
    s_scr[i] = st
    gla_ref[i] = jnp.concatenate(outs, axis=0).astype(BF16)


def _cumsum_selector(rev):
    r = np.arange(TOK)[:, None]
    c = np.arange(TOK)[None, :]
    same = (r // GLA_CHUNK) == (c // GLA_CHUNK)
    tri = same & ((c >= r) if rev else (c <= r))
    return jnp.asarray(np.concatenate([tri, same], axis=0), BF16)


def _seqmix(rev, ncc, pm, lr, w_gate, b_gate, lam, wg_pad, bg, lw):
    batch, t, _ = pm.shape
    nch = t // TOK
    nl = nch - ncc

    def chunk(s):
        if rev:
            return jnp.where(s < ncc, ncc - 1 - s, ncc + (nl - 1) - (s - ncc))
        return s

    qcol = 2 * lw // GLA_KEY
    vcol = (2 * lw + 2 * GLA_KEY) // GLA_VAL
    const = lambda shape: pl.BlockSpec(shape, lambda b, s: (0,) * len(shape))
    nb = SEQ_ROWS if batch % SEQ_ROWS == 0 else 1
    return pl.pallas_call(
        functools.partial(_seqmix_kernel, rev),
        grid=(batch // nb, nch),
        in_specs=[pl.BlockSpec((nb, TOK, lw), lambda b, s: (b, chunk(s), 0)),
                  pl.BlockSpec((nb, TOK, GLA_KEY), lambda b, s: (b, chunk(s), qcol)),
                  pl.BlockSpec((nb, TOK, GLA_KEY), lambda b, s: (b, chunk(s), qcol + 1)),
                  pl.BlockSpec((nb, TOK, GLA_VAL), lambda b, s: (b, chunk(s), vcol)),
                  pl.BlockSpec((nb, TOK, LANES), lambda b, s: (b, chunk(s), 0)),
                  const((lw, 2 * lw)), const((1, 2 * lw)), const((1, lw)),
                  const((LANES, GLA_KEY)), const((1, GLA_KEY)), const((2 * TOK, TOK))],
        out_specs=[pl.BlockSpec((nb, TOK, lw), lambda b, s: (b, chunk(s), 0)),
                   pl.BlockSpec((nb, TOK, GLA_VAL), lambda b, s: (b, chunk(s), 0))],
        out_shape=[jax.ShapeDtypeStruct((batch, t, lw), BF16),
                   jax.ShapeDtypeStruct((batch, t, GLA_VAL), BF16)],
        scratch_shapes=[pltpu.VMEM((nb, SUBLANES, lw), F32), pltpu.VMEM((nb, GLA_DV, GLA_KEY), F32)],
        compiler_params=_cparams(("arbitrary", "arbitrary")),
        name="seqmix_rev" if rev else "seqmix_fwd",
    )(pm, pm, pm, pm, lr, w_gate, b_gate, lam, wg_pad, bg, _cumsum_selector(rev))


def _route_epilogue(i, first, xm, n2_ref, sh2_ref, sc2_ref, wr_ref, br_ref,
                    xmid_ref, h2_ref, cr_ref, cnt_ref, carry):
    d = xm.shape[-1]
    xmid_ref[i] = xm
    h2 = _rms(xm, n2_ref[...]) * (1.0 + sc2_ref[0, i]) + sh2_ref[0, i]
    h_hi = h2.astype(BF16)
    h_lo = (h2 - h_hi.astype(F32)).astype(BF16)
    nt = (((1,), (1,)), ((), ()))
    o1 = lax.dot_general(wr_ref[...], h_hi, nt, preferred_element_type=F32)
    o2 = lax.dot_general(wr_ref[:ROUTER_ROWS], h_lo, nt, preferred_element_type=F32)
    yield
    logits = o1[:ROUTER_ROWS] + o1[ROUTER_ROWS:] + o2 + br_ref[...]

    def col(j):
        return logits[j:j + 1]

    lgs = [col(g) for g in range(N_GROUPS)]
    gmax = functools.reduce(jnp.maximum, lgs)
    gi = jnp.where(lgs[0] == gmax, 0, jnp.where(lgs[1] == gmax, 1, jnp.where(lgs[2] == gmax, 2, 3)))
    w_group = 1.0 / functools.reduce(lambda p, q: p + q, [jnp.exp(l - gmax) for l in lgs])
    es = []
    for j in range(EXPERTS_PER_GROUP):
        acc = jnp.zeros_like(gmax)
        for g in range(N_GROUPS):
            acc = acc + jnp.where(gi == g, col(N_GROUPS + g * EXPERTS_PER_GROUP + j), 0.0)
        es.append(acc)
    m1 = functools.reduce(jnp.maximum, es)
    i1 = jnp.where(es[0] == m1, 0, jnp.where(es[1] == m1, 1, jnp.where(es[2] == m1, 2, 3)))
    rest = [jnp.where(i1 == j, -jnp.inf, es[j]) for j in range(EXPERTS_PER_GROUP)]
    m2 = functools.reduce(jnp.maximum, rest)
    i2 = jnp.where(rest[0] == m2, 0, jnp.where(rest[1] == m2, 1, jnp.where(rest[2] == m2, 2, 3)))
    e2 = jnp.exp(m2 - m1)
    w1 = w_group / (1.0 + e2)
    w2 = w_group * e2 / (1.0 + e2)
    first_lo = i1 < i2
    lo = jnp.where(first_lo, i1, i2)
    hi = jnp.where(first_lo, i2, i1)
    w_lo = jnp.where(first_lo, w1, w2)
    w_hi = jnp.where(first_lo, w2, w1)
    pidx = jnp.where(lo == 0, hi - 1, jnp.where(lo == 1, hi + 1, 5))
    cls = gi * PAIRS_PER_GROUP + pidx

    def terms(w):
        t1 = w.astype(BF16).astype(F32)
        t2 = (w - t1).astype(BF16).astype(F32)
        t3 = (w - t1 - t2).astype(BF16).astype(F32)
        return [t1, t2, t3]

    sub = lax.broadcasted_iota(I32, (2 * SUBLANES, TOK), 0)
    stacked = jnp.zeros((2 * SUBLANES, TOK), F32)
    for r, term in enumerate(terms(w_lo) + terms(w_hi)):
        stacked = jnp.where(sub == r, term, stacked)
    er = lax.broadcasted_iota(I32, (2 * SUBLANES, LANES), 0)
    el = lax.broadcasted_iota(I32, (2 * SUBLANES, LANES), 1)
    place = jnp.logical_or(jnp.logical_and(er < 3, el == 0),
                           jnp.logical_and(jnp.logical_and(er >= 3, er < 6), el == 1))
    w_cols = lax.dot_general(stacked.astype(BF16), place.astype(BF16), (((0,), (0,)), ((), ())),
                             preferred_element_type=F32)
    h2_ref[i, :, :d // 2] = _pack_bf16_pairs(h_hi)
    h2_ref[i, :, d // 2:] = pltpu.bitcast(w_cols, U32)

    krow = lax.broadcasted_iota(I32, (ROUTER_ROWS, TOK), 0)
    onehot = krow == cls
    ri = lax.broadcasted_iota(I32, (TOK, TOK), 0)
    ci = lax.broadcasted_iota(I32, (TOK, TOK), 1)
    triu = (ri <= ci).astype(BF16)
    pref = jnp.dot(onehot.astype(BF16), triu, preferred_element_type=F32)
    yield

    if i == 0:
        @pl.when(first)
        def _():
            carry[...] = jnp.zeros_like(carry)

    base = carry[:, 0:1]
    rank = jnp.sum(jnp.where(onehot, pref - 1.0 + base, 0.0), axis=0, keepdims=True)
    new_carry = jnp.broadcast_to(base + jnp.sum(onehot.astype(F32), axis=1, keepdims=True), carry.shape)
    carry[...] = new_carry
    cnt_ref[...] = new_carry
    sub8 = lax.broadcasted_iota(I32, (SUBLANES, TOK), 0)
    cr_ref[i, 0] = jnp.where(sub8 == 0, cls, jnp.where(sub8 == 1, rank.astype(I32), 0))


def _post0_kernel(ncc, *refs):
    _lockstep([_post0_row(ncc, i, *refs) for i in range(refs[0].shape[0])])


def _post0_row(ncc, i, l0_ref, l1_ref, g0_ref, g1_ref, ga_ref, r_ref, x_ref, c_ref,
               gate_ref, sh2_ref, sc2_ref, n2_ref, gn_ref, wout_ref, wr_ref, br_ref,
               xmid_ref, h2_ref, cr_ref, cnt_ref, carry):
    b, c = pl.program_id(0), pl.program_id(1)
    lru = l0_ref[i].astype(F32) + l1_ref[i].astype(F32)
    ya = lru * _gelu_tanh(ga_ref[i].astype(F32))
    gla = g0_ref[i].astype(F32) + g1_ref[i].astype(F32)
    parts = []
    for h in range(GLA_HEADS):
        parts.append(_rms(gla[:, h * GLA_DV:(h + 1) * GLA_DV], gn_ref[...]))
    yb = jnp.concatenate(parts, axis=1) * _silu(r_ref[i].astype(F32))
    ycat = jnp.concatenate([ya, yb], axis=1).astype(BF16)
    y = jnp.dot(ycat, wout_ref[...], preferred_element_type=F32)
    yield
    x0 = jnp.where(c < ncc, c_ref[i], x_ref[i])
    xm = x0 + gate_ref[0, i] * y
    yield from _route_epilogue(i, jnp.logical_and(b == 0, c == 0), xm, n2_ref, sh2_ref, sc2_ref, wr_ref, br_ref,
                               xmid_ref, h2_ref, cr_ref, cnt_ref, carry)


def _post1_kernel(*refs):
    _lockstep([_post1_row(i, *refs) for i in range(refs[0].shape[0])])


def _post1_row(i, o_ref, x_ref, gate_ref, sh2_ref, sc2_ref, n2_ref, wo_ref, wr_ref, br_ref,
               xmid_ref, h2_ref, cr_ref, cnt_ref, carry):
    b, c = pl.program_id(0), pl.program_id(1)
    y = jnp.dot(o_ref[i], wo_ref[...], preferred_element_type=F32)
    yield
    xm = x_ref[i] + gate_ref[0, i] * y
    yield from _route_epilogue(i, jnp.logical_and(b == 0, c == 0), xm, n2_ref, sh2_ref, sc2_ref, wr_ref, br_ref,
                               xmid_ref, h2_ref, cr_ref, cnt_ref, carry)


def _route_out(batch, nch, d, nb):
    specs = [pl.BlockSpec((nb, TOK, d), lambda b, c: (b, c, 0)),
             pl.BlockSpec((nb, TOK, d // 2 + WEXT), lambda b, c: (b, c, 0)),
             pl.BlockSpec((nb, 1, SUBLANES, TOK), lambda b, c: (b, c, 0, 0)),
             pl.BlockSpec((ROUTER_ROWS, LANES), lambda b, c: (0, 0))]
    shapes = [jax.ShapeDtypeStruct((batch, nch * TOK, d), F32),
              jax.ShapeDtypeStruct((batch, nch * TOK, d // 2 + WEXT), U32),
              jax.ShapeDtypeStruct((batch, nch, SUBLANES, TOK), I32),
              jax.ShapeDtypeStruct((ROUTER_ROWS, LANES), F32)]
    return specs, shapes


def _post0(ncc, lru_f, lru_r, gla_f, gla_r, pm, x, ctx, mods, n2, gn, w_out, wr, br, lw):
    batch, t, _ = pm.shape
    d = x.shape[-1]
    nch = t // TOK
    rcol = (2 * lw + 2 * GLA_KEY + GLA_VAL) // GLA_VAL
    nb = TOK_ROWS
    tokspec = lambda w: pl.BlockSpec((nb, TOK, w), lambda b, c: (b, c, 0))
    const = lambda shape: pl.BlockSpec(shape, lambda b, c: (0,) * len(shape))
    out_specs, out_shape = _route_out(batch, nch, d, nb)
    return pl.pallas_call(
        functools.partial(_post0_kernel, ncc),
        grid=(batch // nb, nch),
        in_specs=[tokspec(lw), tokspec(lw), tokspec(GLA_VAL), tokspec(GLA_VAL),
                  pl.BlockSpec((nb, TOK, lw), lambda b, c: (b, c, 1)),
                  pl.BlockSpec((nb, TOK, GLA_VAL), lambda b, c: (b, c, rcol)),
                  pl.BlockSpec((nb, TOK, d), lambda b, c: (b, jnp.maximum(c - ncc, 0), 0)),
                  pl.BlockSpec((nb, TOK, d), lambda b, c: (b, jnp.minimum(c, ncc - 1), 0)),
                  mods.spec(0, 2, nb=nb), mods.spec(0, 3, nb=nb), mods.spec(0, 4, nb=nb),
                  const((1, d)), const((1, GLA_DV)), const((lw + GLA_VAL, d)),
                  const((2 * ROUTER_ROWS, d)), const((ROUTER_ROWS, 1))],
        out_specs=out_specs, out_shape=out_shape,
        scratch_shapes=[pltpu.VMEM((ROUTER_ROWS, LANES), F32)],
        compiler_params=_cparams(("arbitrary", "arbitrary")),
        name="post0",
    )(lru_f, lru_r, gla_f, gla_r, pm, pm, x, ctx, mods.table, mods.table, mods.table, n2, gn, w_out, wr, br)


def _post1(ncc, att, x1, mods, n2, w_o, wr, br):
    batch, seq, aw = att.shape
    d = x1.shape[-1]
    nl = seq // TOK
    const = lambda shape: pl.BlockSpec(shape, lambda b, c: (0,) * len(shape))
    nb = TOK_ROWS
    out_specs, out_shape = _route_out(batch, nl, d, nb)
    return pl.pallas_call(
        _post1_kernel,
        grid=(batch // nb, nl),
        in_specs=[pl.BlockSpec((nb, TOK, aw), lambda b, c: (b, c, 0)),
                  pl.BlockSpec((nb, TOK, d), lambda b, c: (b, c + ncc, 0)),
                  mods.spec(1, 2, True, nb), mods.spec(1, 3, True, nb), mods.spec(1, 4, True, nb),
                  const((1, d)), const((aw, d)), const((2 * ROUTER_ROWS, d)), const((ROUTER_ROWS, 1))],
        out_specs=out_specs, out_shape=out_shape,
        scratch_shapes=[pltpu.VMEM((ROUTER_ROWS, LANES), F32)],
        compiler_params=_cparams(("arbitrary", "arbitrary")),
        name="post1",
    )(att, x1, mods.table, mods.table, mods.table, n2, w_o, wr, br)


_PAIR_LO = np.array([0, 0, 0, 1, 1, 2], np.int32)
_PAIR_HI = np.array([1, 2, 3, 2, 3, 3], np.int32)


def _plan(cr, cnt, n_tiles):
    cls, rank = cr[:, :, 0, :], cr[:, :, 1, :]
    counts = cnt[:N_CLASSES, 0].astype(I32)
    tiles = (counts + MOE_TILE - 1) // MOE_TILE
    ends = jnp.cumsum(tiles)
    offs = (ends - tiles) * MOE_TILE
    pos = rank
    for c in range(N_CLASSES):
        pos = pos + jnp.where(cls == c, offs[c], 0)
    pos = pos.reshape(-1)
    tile_ids = jnp.arange(n_tiles, dtype=I32)
    tile_cls = jnp.minimum(jnp.sum((tile_ids[:, None] >= ends[None, :]).astype(I32), axis=1), N_CLASSES - 1)
    grp = tile_cls // PAIRS_PER_GROUP
    pair = tile_cls % PAIRS_PER_GROUP
    lo = grp * EXPERTS_PER_GROUP + jnp.asarray(_PAIR_LO)[pair]
    hi = grp * EXPERTS_PER_GROUP + jnp.asarray(_PAIR_HI)[pair]
    return pos.astype(I32), lo.astype(I32), hi.astype(I32), ends[-1:].astype(I32)


def _moe_kernel(d, lo_ref, hi_ref, used_ref, x_ref, w1l, w3l, w2l, w1h, w3h, w2h, y_ref, wbuf, wbuf2):
    i = pl.program_id(0)
    active = i < used_ref[0]

    @pl.when(active)
    def _():
        changed = jnp.logical_or(i == 0, jnp.logical_or(lo_ref[i] != lo_ref[jnp.maximum(i - 1, 0)],
                                                        hi_ref[i] != hi_ref[jnp.maximum(i - 1, 0)]))

        @pl.when(changed)
        def _():
            for k, w in enumerate((w1l, w3l, w1h, w3h)):
                wbuf[k] = w[0, 0].astype(BF16)
            wbuf2[0] = w2l[0, 0].astype(BF16)
            wbuf2[1] = w2h[0, 0].astype(BF16)

        x = _unpack_bf16_pairs(x_ref[:, :d // 2])
        wts = pltpu.bitcast(x_ref[:, d // 2:], F32)

        def expert(k1, k3, k2):
            a = jnp.dot(x, wbuf[k1], preferred_element_type=F32)
            b = jnp.dot(x, wbuf[k3], preferred_element_type=F32)
            return jnp.dot((_silu(a) * b).astype(BF16), wbuf2[k2], preferred_element_type=F32)

        y = wts[:, 0:1] * expert(0, 1, 0) + wts[:, 1:2] * expert(2, 3, 1)
        y_ref[...] = _pack_bf16_pairs(y.astype(BF16))

    @pl.when(jnp.logical_not(active))
    def _():
        y_ref[...] = jnp.zeros_like(y_ref)


def _moe(xs, lo, hi, n_used, layer, w1, w3, w2):
    n_pad, wd = xs.shape
    d = 2 * (wd - WEXT)
    hdim = w1.shape[-1]
    n_tiles = n_pad // MOE_TILE
    wl = lambda i, lo, hi, used: (layer, lo[i], 0, 0)
    wh = lambda i, lo, hi, used: (layer, hi[i], 0, 0)
    up = lambda imap: pl.BlockSpec((1, 1, d, hdim), imap)
    down = lambda imap: pl.BlockSpec((1, 1, hdim, d), imap)
    grid_spec = pltpu.PrefetchScalarGridSpec(
        num_scalar_prefetch=3,
        grid=(n_tiles,),
        in_specs=[pl.BlockSpec((MOE_TILE, wd), lambda i, lo, hi, used: (i, 0)),
                  up(wl), up(wl), down(wl), up(wh), up(wh), down(wh)],
        out_specs=pl.BlockSpec((MOE_TILE, d // 2), lambda i, lo, hi, used: (i, 0)),
        scratch_shapes=[pltpu.VMEM((4, d, hdim), BF16), pltpu.VMEM((2, hdim, d), BF16)],
    )
    return pl.pallas_call(
        functools.partial(_moe_kernel, d),
        grid_spec=grid_spec,
        out_shape=jax.ShapeDtypeStruct((n_pad, d // 2), U32),
        compiler_params=_cparams(("arbitrary",)),
        name="moe_experts",
    )(lo, hi, n_used, xs, w1, w3, w2, w1, w3, w2)


def _row_dma_loops(tile_rows, make_copy, index):
    def issue(grp, carry):
        for sub in range(SUBLANES):
            make_copy(grp, sub, index(grp * SUBLANES + sub)).start()
        return carry

    def drain(grp, carry):
        for _ in range(SUBLANES):
            make_copy(0, 0, 0).wait()
        return carry

    n_groups = tile_rows // SUBLANES

    def issue_all():
        for grp in range(n_groups):
            issue(grp, 0)

    return issue_all, (lambda: lax.fori_loop(0, n_groups, drain, 0))


def _scatter_rows_kernel(rows, pos_ref, src_ref, init_ref, dst_ref, sem):
    del init_ref
    issue, drain = _row_dma_loops(
        rows,
        lambda grp, sub, p: pltpu.make_async_copy(src_ref.at[grp, pl.ds(sub, 1)], dst_ref.at[pl.ds(p, 1)], sem),
        lambda j: pos_ref[0, 0, j])
    issue()
    drain()


def _scatter_rows(src, pos, n_pad):
    n, w = src.shape
    rows = PERM_ROWS if n % PERM_ROWS == 0 else TOK
    steps = n // rows
    return pl.pallas_call(
        functools.partial(_scatter_rows_kernel, rows),
        grid=(steps,),
        in_specs=[pl.BlockSpec((1, 1, rows), lambda i: (i, 0, 0), memory_space=pltpu.SMEM),
                  pl.BlockSpec((rows // SUBLANES, SUBLANES, w), lambda i: (i, 0, 0)),
                  pl.BlockSpec(memory_space=pl.ANY)],
        out_specs=pl.BlockSpec(memory_space=pl.ANY),
        out_shape=jax.ShapeDtypeStruct((n_pad, w), src.dtype),
        scratch_shapes=[pltpu.SemaphoreType.DMA(())],
        input_output_aliases={2: 0},
        compiler_params=pltpu.CompilerParams(dimension_semantics=("arbitrary",), has_side_effects=True),
        name="scatter_rows",
    )(pos.reshape(steps, 1, rows), src.reshape(n // SUBLANES, SUBLANES, w), jnp.zeros((n_pad, w), src.dtype))


def _gather_residual_kernel(n_steps, pos_ref, posn_ref, ys_ref, xm_ref, gate_ref, o_ref, buf, sem):
    step = pl.program_id(0) * pl.num_programs(1) + pl.program_id(1)
    slot = step % 2

    def loops(idx_ref, s):
        return _row_dma_loops(
            TOK,
            lambda grp, sub, p: pltpu.make_async_copy(ys_ref.at[pl.ds(p, 1)], buf.at[s, grp, pl.ds(sub, 1)],
                                                      sem.at[s]),
            lambda j: idx_ref[0, 0, j])

    @pl.when(step == 0)
    def _():
        loops(pos_ref, slot)[0]()

    @pl.when(step + 1 < n_steps)
    def _():
        loops(posn_ref, 1 - slot)[0]()

    loops(pos_ref, slot)[1]()
    y = _unpack_bf16_pairs(buf[slot].reshape(TOK, buf.shape[-1])).astype(F32)
    o_ref[0] = xm_ref[0] + gate_ref[0, 0] * y


def _gather_residual(ys, pos, xmid, gate_spec, mod_table):
    batch, t, d = xmid.shape
    nch = t // TOK
    n_steps = batch * nch
    return pl.pallas_call(
        functools.partial(_gather_residual_kernel, n_steps),
        grid=(batch, nch),
        in_specs=[pl.BlockSpec((1, 1, TOK), lambda b, c: (b * nch + c, 0, 0), memory_space=pltpu.SMEM),
                  pl.BlockSpec((1, 1, TOK), lambda b, c: (jnp.minimum(b * nch + c + 1, n_steps - 1), 0, 0),
                               memory_space=pltpu.SMEM),
                  pl.BlockSpec(memory_space=pl.ANY),
                  pl.BlockSpec((1, TOK, d), lambda b, c: (b, c, 0)),
                  gate_spec],
        out_specs=pl.BlockSpec((1, TOK, d), lambda b, c: (b, c, 0)),
        out_shape=jax.ShapeDtypeStruct((batch, t, d), F32),
        scratch_shapes=[pltpu.VMEM((2, TOK // SUBLANES, SUBLANES, d // 2), U32), pltpu.SemaphoreType.DMA((2,))],
        compiler_params=_cparams(("arbitrary", "arbitrary")),
        name="gather_residual",
    )(pos.reshape(n_steps, 1, TOK), pos.reshape(n_steps, 1, TOK), ys, xmid, mod_table)


def _moe_block(h2ext, cr, cnt, xmid, gate_spec, mod_table, layer, w1, w3, w2):
    h2ext = h2ext.reshape(-1, h2ext.shape[-1])
    n = h2ext.shape[0]
    n_tiles = n // MOE_TILE + N_CLASSES
    pos, lo, hi, n_used = _plan(cr, cnt, n_tiles)
    xs = _scatter_rows(h2ext, pos, n_tiles * MOE_TILE)
    ys = _moe(xs, lo, hi, n_used, layer, w1, w3, w2)
    return _gather_residual(ys, pos, xmid, gate_spec, mod_table)


def _qkv_kernel(ncc, x_ref, g_ref, sh_ref, sc_ref, w_ref, qn_ref, kn_ref, cos_ref, sin_ref,
                q_ref, k_ref, v_ref):
    c = pl.program_id(1)
    latent = c >= ncc
    nb = x_ref.shape[0]
    nq = ATT_HEADS * HEAD_DIM
    nk = KV_HEADS * HEAD_DIM
    cos = cos_ref[...]
    sin = sin_ref[...]
    hbs = [None] * nb

    def head(z, gain):
        z = _rms(z, gain)
        rot = z * cos + pltpu.roll(z, HEAD_DIM // 2, 1) * sin
        return jnp.where(latent, rot, z)

    def kv_row(r):
        h = _rms(x_ref[r], g_ref[...]) * (1.0 + sc_ref[0, r]) + sh_ref[0, r]
        hbs[r] = h.astype(BF16)
        kv = jnp.dot(hbs[r], w_ref[:, nq:], preferred_element_type=F32)
        yield
        k_ref[r] = jnp.concatenate(
            [head(kv[:, i * HEAD_DIM:(i + 1) * HEAD_DIM], kn_ref[...]) for i in range(KV_HEADS)],
            axis=1).astype(BF16)
        ones = jnp.ones((TOK, HEAD_DIM), BF16)
        v_ref[r] = jnp.concatenate(
            [blk for i in range(KV_HEADS)
             for blk in (kv[:, nk + i * HEAD_DIM:nk + (i + 1) * HEAD_DIM].astype(BF16), ones)], axis=1)

    def q_row(r):
        qq = jnp.dot(hbs[r], w_ref[:, :nq], preferred_element_type=F32)
        yield
        q_ref[r] = jnp.concatenate(
            [head(qq[:, i * HEAD_DIM:(i + 1) * HEAD_DIM], qn_ref[...]) * (HEAD_DIM ** -0.5)
             for i in range(ATT_HEADS)], axis=1).astype(BF16)

    _lockstep([kv_row(r) for r in range(nb)])

    @pl.when(latent)
    def _():
        _lockstep([q_row(r) for r in range(nb)])


def _qkv(ncc, x1, gain, mods, w_qkv, qn, kn, cos, sin):
    batch, t, d = x1.shape
    nch = t // TOK
    seq = t - ncc * TOK
    nq = ATT_HEADS * HEAD_DIM
    nk = KV_HEADS * HEAD_DIM
    const = lambda shape: pl.BlockSpec(shape, lambda b, c: (0,) * len(shape))
    lat = lambda b, c: (b, jnp.maximum(c - ncc, 0), 0)
    nb = QKV_ROWS
    return pl.pallas_call(
        functools.partial(_qkv_kernel, ncc),
        grid=(batch // nb, nch),
        in_specs=[pl.BlockSpec((nb, TOK, d), lambda b, c: (b, c, 0)),
                  const((1, d)), mods.spec(1, 0, nb=nb), mods.spec(1, 1, nb=nb),
                  const((d, nq + 2 * nk)), const((1, HEAD_DIM)), const((1, HEAD_DIM)),
                  pl.BlockSpec((TOK, HEAD_DIM), lambda b, c: (jnp.maximum(c - ncc, 0), 0)),
                  pl.BlockSpec((TOK, HEAD_DIM), lambda b, c: (jnp.maximum(c - ncc, 0), 0))],
        out_specs=[pl.BlockSpec((nb, TOK, nq), lat),
                   pl.BlockSpec((nb, TOK, nk), lambda b, c: (b, c, 0)),
                   pl.BlockSpec((nb, TOK, 2 * nk), lambda b, c: (b, c, 0))],
        out_shape=[jax.ShapeDtypeStruct((batch, seq, nq), BF16),
                   jax.ShapeDtypeStruct((batch, t, nk), BF16),
                   jax.ShapeDtypeStruct((batch, t, 2 * nk), BF16)],
        compiler_params=_cparams(("arbitrary", "arbitrary")),
        name="qkv",
    )(x1, gain, mods.table, mods.table, w_qkv, qn, kn, cos, sin)


def _attn_kernel(n_kb, q_ref, k_ref, v_ref, o_ref):
    q = q_ref[0]
    qs = jnp.concatenate([q[:, g * HEAD_DIM:(g + 1) * HEAD_DIM] for g in range(Q_PER_KV)], axis=0)
    kb = k_ref.shape[1] // n_kb
    m = acc = None
    for j in range(n_kb):
        s = lax.dot_general(qs, k_ref[0, j * kb:(j + 1) * kb, :], (((1,), (1,)), ((), ())),
                            preferred_element_type=F32)
        m_blk = jnp.max(s, axis=-1, keepdims=True)
        m_new = m_blk if m is None else jnp.maximum(m, m_blk)
        p = jnp.exp((s - m_new).astype(BF16))
        pv = jnp.dot(p, v_ref[0, j * kb:(j + 1) * kb, :], preferred_element_type=F32)
        acc = pv if acc is None else jnp.exp(m - m_new) * acc + pv
        m = m_new
    o = acc[:, :HEAD_DIM] / acc[:, HEAD_DIM:HEAD_DIM + 1]
    o_ref[0] = jnp.concatenate([o[g * ATT_Q:(g + 1) * ATT_Q] for g in range(Q_PER_KV)], axis=1).astype(BF16)


def _attention(q, k, v):
    batch, seq, nq = q.shape
    t = k.shape[1]
    gw = Q_PER_KV * HEAD_DIM
    n_kb = 2 if t % (2 * LANES) == 0 else 1
    return pl.pallas_call(
        functools.partial(_attn_kernel, n_kb),
        grid=(batch, KV_HEADS, seq // ATT_Q),
        in_specs=[pl.BlockSpec((1, ATT_Q, gw), lambda b, h, i: (b, i, h)),
                  pl.BlockSpec((1, t, HEAD_DIM), lambda b, h, i: (b, 0, h)),
                  pl.BlockSpec((1, t, 2 * HEAD_DIM), lambda b, h, i: (b, 0, h))],
        out_specs=pl.BlockSpec((1, ATT_Q, gw), lambda b, h, i: (b, i, h)),
        out_shape=jax.ShapeDtypeStruct((batch, seq, nq), BF16),
        compiler_params=_cparams(("arbitrary", "arbitrary", "arbitrary")),
        name="attention",
    )(q, k, v)


def _block_diag(w):
    nb, bs, _ = w.shape
    eye = jnp.eye(nb, dtype=w.dtype)
    return (eye[:, None, :, None] * w[:, :, None, :]).reshape(nb * bs, nb * bs)


def _router_weights(wg, bg, we, be):
    d = wg.shape[0]
    n = N_GROUPS + N_EXPERTS
    wr = jnp.zeros((ROUTER_ROWS, d), F32).at[:N_GROUPS].set(wg.T).at[N_GROUPS:n].set(we.T)
    br = jnp.zeros((ROUTER_ROWS, 1), F32).at[:N_GROUPS, 0].set(bg).at[N_GROUPS:n, 0].set(be)
    w_hi = wr.astype(BF16)
    w_lo = (wr - w_hi.astype(F32)).astype(BF16)
    return jnp.concatenate([w_hi, w_lo], axis=0), br


def _rope_tables(seq):
    rows = seq // GRID_W
    row = np.repeat(np.arange(rows, dtype=np.float32), GRID_W)
    col = np.tile(np.arange(GRID_W, dtype=np.float32), rows)
    ppa = HEAD_DIM // 4
    freqs = (ROPE_THETA ** (-np.arange(ppa, dtype=np.float32) / ppa)).astype(np.float32)
    ang = np.concatenate([row[:, None] * freqs, col[:, None] * freqs], axis=-1)
    cos, sin = np.cos(ang), np.sin(ang)
    return (jnp.asarray(np.concatenate([cos, cos], axis=-1), F32),
            jnp.asarray(np.concatenate([-sin, sin], axis=-1), F32))


_HALF_SPLIT = np.concatenate([np.arange(0, HEAD_DIM, 2), np.arange(1, HEAD_DIM, 2)])


def kernel(x, c, ctx, c_ctx, norm1, norm2, w_ada, b_ada, ev_w_in, ev_conv_w, ev_conv_b, ev_lru_wa, ev_lru_ba, ev_lru_wi, ev_lru_bi, ev_lru_lam, ev_gla_wg, ev_gla_bg, ev_gla_norm, ev_w_out, od_w_qkv, od_q_norm, od_k_norm, od_w_o, moe_wg, moe_bg, moe_we, moe_be, moe_w1, moe_w3, moe_w2):
    batch, seq, d = x.shape
    ctx_len = ctx.shape[1]
    assert seq % TOK == 0 and ctx_len % TOK == 0 and d % GLA_VAL == 0 and seq % GRID_W == 0
    ncc = ctx_len // TOK
    lw = d // 2

    assert batch % TOK_ROWS == 0
    rows = -(-(batch + TOK_ROWS) // SUBLANES) * SUBLANES
    cv = jnp.zeros((rows, d), F32).at[:batch].set(c).at[batch:batch + TOK_ROWS].set(c_ctx)
    table = _adaln(cv, w_ada, b_ada)
    depth = table.shape[0]
    table = table.reshape(depth, rows, 6, d).transpose(0, 2, 1, 3).reshape(depth * 6, rows, 1, d)
    mods = _Mods(table, batch, ncc)

    w_in = ev_w_in[0]
    nm = 2 * lw + 2 * GLA_KEY + 2 * GLA_VAL
    w_main = w_in[:, :nm].astype(BF16)
    w_lr = jnp.zeros((d, LANES), F32).at[:, :2 * GLA_RANK].set(w_in[:, nm:]).astype(BF16)
    pm, lr = _inproj(x, ctx, norm1[0][None], mods, w_main, w_lr, ev_conv_w[0], ev_conv_b[0][None], lw)

    mixed = []
    for dr in range(2):
        w_gate = jnp.concatenate([_block_diag(ev_lru_wa[0, dr]), _block_diag(ev_lru_wi[0, dr])], axis=1).astype(BF16)
        b_gate = jnp.concatenate([ev_lru_ba[0, dr], ev_lru_bi[0, dr]])[None]
        wg_pad = jnp.zeros((LANES, GLA_KEY), F32).at[dr * GLA_RANK:(dr + 1) * GLA_RANK].set(ev_gla_wg[0, dr]).astype(BF16)
        mixed.append(_seqmix(dr == 1, ncc, pm, lr, w_gate, b_gate,
                             ev_lru_lam[0, dr][None], wg_pad, ev_gla_bg[0, dr][None], lw))
    (lru_f, gla_f), (lru_r, gla_r) = mixed

    wr0, br0 = _router_weights(moe_wg[0], moe_bg[0], moe_we[0], moe_be[0])
    xmid0, h2e0, cr0, cnt0 = _post0(ncc, lru_f, lru_r, gla_f, gla_r, pm, x, ctx, mods, norm2[0][None],
                                    ev_gla_norm[0][None], ev_w_out[0].astype(BF16), wr0, br0, lw)
    x1 = _moe_block(h2e0, cr0, cnt0, xmid0, mods.spec(0, 5), mods.table, 0, moe_w1, moe_w3, moe_w2)

    nq = ATT_HEADS * HEAD_DIM
    nk = KV_HEADS * HEAD_DIM
    perm = np.concatenate([h * HEAD_DIM + _HALF_SPLIT for h in range(ATT_HEADS + KV_HEADS)]
                          + [np.arange(nq + nk, nq + 2 * nk)])
    w_qkv = od_w_qkv[0][:, perm].astype(BF16)
    cos, sin = _rope_tables(seq)
    q, k, v = _qkv(ncc, x1, norm1[1][None], mods, w_qkv, od_q_norm[0][_HALF_SPLIT][None],
                   od_k_norm[0][_HALF_SPLIT][None], cos, sin)
    att = _attention(q, k, v)
    wr1, br1 = _router_weights(moe_wg[1], moe_bg[1], moe_we[1], moe_be[1])
    xmid1, h2e1, cr1, cnt1 = _post1(ncc, att, x1, mods, norm2[1][None], od_w_o[0].astype(BF16), wr1, br1)
    return _moe_block(h2e1, cr1, cnt1, xmid1, mods.spec(1, 5, True), mods.table, 1, moe_w1, moe_w3, moe_w2)
```
